```python
import math
import jax
import jax.numpy as jnp
from jax import lax
import numpy as np

D_MODEL = 2048
BATCH = 1
SEQ = 8192
DEPTH = 1

GRID_W = 64
CTX_LEN = 256
N_MOD = 9
D_FF = 5632
NORM_EPS = 1e-6

MLA_HEADS = 16
Q_LORA = 512
KV_LORA = 512
QK_NOPE = 128
QK_ROPE = 64
V_HEAD = 128
ROPE_AXIS = QK_ROPE // 2
ROPE_THETA = 10000.0
ATTN_SCALE = (QK_NOPE + QK_ROPE) ** -0.5
Q_BLOCK = 128

HY_WIDTH = 1024
HY_ORDER = 2
HY_SHORT = 3
HY_EMB = 33
HY_BANDS = (HY_EMB - 1) // 2
HY_FILTER_W = 64
HY_TARGET = 1e-2
HY_FAST_DECAY = 0.3
HY_SLOW_DECAY = 1.5
HY_MAX_DECAY = math.log(HY_TARGET) / HY_FAST_DECAY
HY_MIN_DECAY = math.log(HY_TARGET) / HY_SLOW_DECAY
HY_MOD_SHIFT = 0.05

Q_END = Q_LORA
KV_END = Q_END + KV_LORA
KR_END = KV_END + QK_ROPE
HY_END = KR_END + 3 * HY_WIDTH
D_IN = HY_END + 2 * D_MODEL

kernel_name = 'hybrid_mla_hyena_dit_block'


def rms_norm(x, g):
    xf = x.astype(jnp.float32)
    y = xf * lax.rsqrt(jnp.mean(xf * xf, axis=-1, keepdims=True) + NORM_EPS)
    return (y * g.astype(jnp.float32)).astype(x.dtype)


def modulate(h, shift, scale):
    return h * (1.0 + scale) + shift


def swiglu(h, w13, w2):
    a, b = jnp.split(h @ w13, 2, axis=-1)
    return (jax.nn.silu(a) * b) @ w2


def half_ffn(s, shift, scale, gate, g, w13, w2):
    return s + 0.5 * gate * swiglu(modulate(rms_norm(s, g), shift, scale), w13, w2)


def split_proj(p):
    return p[..., :Q_END], p[..., Q_END:KV_END], p[..., KV_END:KR_END], p[..., KR_END:HY_END], p[..., HY_END:]


def rope_half(x, ang):
    x1, x2 = jnp.split(x, 2, axis=-1)
    cos = jnp.cos(ang).astype(x.dtype)
    sin = jnp.sin(ang).astype(x.dtype)
    return jnp.concatenate([x1 * cos - x2 * sin, x1 * sin + x2 * cos], axis=-1)


def rope_2d(x, ang_row, ang_col):
    xr, xc = jnp.split(x, 2, axis=-1)
    return jnp.concatenate([rope_half(xr, ang_row), rope_half(xc, ang_col)], axis=-1)


def mla_query(p_q, g_q, w_uq):
    B, L, _ = p_q.shape
    q = (rms_norm(p_q, g_q) @ w_uq).reshape(B, L, MLA_HEADS, QK_NOPE + QK_ROPE)
    return q[..., :QK_NOPE], q[..., QK_NOPE:]


def mla_kv(p_kv, g_kv, w_ukv):
    B, L, _ = p_kv.shape
    kv = (rms_norm(p_kv, g_kv) @ w_ukv).reshape(B, L, MLA_HEADS, QK_NOPE + V_HEAD)
    return kv[..., :QK_NOPE], kv[..., QK_NOPE:]


def join_key(k_nope, k_rope):
    kr = jnp.broadcast_to(k_rope[:, :, None, :], k_nope.shape[:-1] + (QK_ROPE,))
    return jnp.concatenate([k_nope, kr], axis=-1)


def softmax_attend(q, k, v):
    s = jnp.einsum('bqhd,bkhd->bhqk', q, k, preferred_element_type=jnp.float32) * ATTN_SCALE
    p = jax.nn.softmax(s, axis=-1).astype(v.dtype)
    return jnp.einsum('bhqk,bkhd->bqhd', p, v)


def blocked_attention(q, k, v):
    B, S, H, Dq = q.shape
    nb = S // Q_BLOCK
    qb = q.reshape(B, nb, Q_BLOCK, H, Dq).transpose(1, 0, 2, 3, 4)
    out = lax.map(lambda qblk: softmax_attend(qblk, k, v), qb)
    return out.transpose(1, 0, 2, 3, 4).reshape(B, S, H, v.shape[-1])


def hyena_filters(L, w1, b1, w2, b2, w3, freq):
    f32 = lambda a: a.astype(jnp.float32)
    pos = jnp.arange(L, dtype=jnp.float32)[:, None]
    t01 = pos / max(L - 1, 1)
    bands = jnp.linspace(1e-4, HY_BANDS - 1, HY_BANDS, dtype=jnp.float32)[None, :]
    ang = bands * (2.0 * math.pi / L) * pos
    z = jnp.concatenate([t01, jnp.cos(ang), -jnp.sin(ang)], axis=-1)
    h = jnp.sin(f32(freq) * (z @ f32(w1) + f32(b1)))
    h = jnp.sin(f32(freq) * (h @ f32(w2) + f32(b2)))
    h = (h @ f32(w3)).reshape(L, HY_ORDER, HY_WIDTH)
    dist = (jnp.abs(pos - (L // 2)) / (L / 2.0))[:, :, None]
    decay = jnp.abs(jnp.linspace(HY_MIN_DECAY, HY_MAX_DECAY, HY_WIDTH, dtype=jnp.float32))
    h = h * (jnp.exp(-dist * decay) + HY_MOD_SHIFT)
    return h * lax.rsqrt(jnp.sum(h * h, axis=0, keepdims=True) + NORM_EPS)


def short_conv(u, w, b):
    L = u.shape[1]
    up = jnp.pad(u, ((0, 0), (1, 1), (0, 0)))
    return up[:, :L] * w[0] + up[:, 1:L + 1] * w[1] + up[:, 2:] * w[2] + b


def fft_long_conv(u, h):
    L = u.shape[1]
    n = 2 * L
    uf = jnp.fft.rfft(u.astype(jnp.float32), n=n, axis=1)
    hf = jnp.fft.rfft(h, n=n, axis=0)
    y = jnp.fft.irfft(uf * hf[None], n=n, axis=1)[:, L // 2:L // 2 + L]
    return y.astype(u.dtype)


def hyena(p_hy, conv_w, conv_b, filt, skip):
    v, x1, x2 = jnp.split(short_conv(p_hy, conv_w, conv_b), 3, axis=-1)
    z = x1 * (fft_long_conv(v, filt[:, 0]) + skip[0] * v)
    return x2 * (fft_long_conv(z, filt[:, 1]) + skip[1] * z)


def merge_branches(y_attn, y_hy, p_gate, w_attn_o, w_hy_o, w_out):
    g_attn, g_hy = jnp.split(p_gate, 2, axis=-1)
    y = jax.nn.sigmoid(g_attn) * (y_attn @ w_attn_o) + jax.nn.sigmoid(g_hy) * (y_hy @ w_hy_o)
    return y @ w_out


def setup_inputs(seed: int = 0) -> dict:
    key = jax.random.key(seed)
    ks = iter(jax.random.split(key, 40))
    D = D_MODEL
    L = DEPTH

    def nrm(shape, scale):
        return jax.random.normal(next(ks), shape, jnp.float32) * scale

    def gain(shape):
        return 1.0 + nrm(shape, 0.02)

    return {
        'x': nrm((BATCH, SEQ, D), 1.0),
        'c': nrm((BATCH, D), 1.0),
        'ctx': nrm((BATCH, CTX_LEN, D), 1.0),
        'c_ctx': nrm((D,), 1.0),
        'w_mod': nrm((L, D, N_MOD * D), 0.5 * D ** -0.5),
        'b_mod': nrm((L, N_MOD * D), 0.02),
        'g_ffn1': gain((L, D)),
        'w13_ffn1': nrm((L, D, 2 * D_FF), D ** -0.5),
        'w2_ffn1': nrm((L, D_FF, D), D_FF ** -0.5),
        'g_mix': gain((L, D)),
        'w_in': nrm((L, D, D_IN), D ** -0.5),
        'g_q': gain((L, Q_LORA)),
        'w_uq': nrm((L, Q_LORA, MLA_HEADS * (QK_NOPE + QK_ROPE)), Q_LORA ** -0.5),
        'g_kv': gain((L, KV_LORA)),
        'w_ukv': nrm((L, KV_LORA, MLA_HEADS * (QK_NOPE + V_HEAD)), KV_LORA ** -0.5),
        'w_attn_o': nrm((L, MLA_HEADS * V_HEAD, D), (MLA_HEADS * V_HEAD) ** -0.5),
        'hy_conv_w': nrm((L, HY_SHORT, 3 * HY_WIDTH), HY_SHORT ** -0.5),
        'hy_conv_b': nrm((L, 3 * HY_WIDTH), 0.02),
        'hy_w1': nrm((L, HY_EMB, HY_FILTER_W), HY_EMB ** -0.5),
        'hy_b1': nrm((L, HY_FILTER_W), 0.02),
        'hy_w2': nrm((L, HY_FILTER_W, HY_FILTER_W), HY_FILTER_W ** -0.5),
        'hy_b2': nrm((L, HY_FILTER_W), 0.02),
        'hy_w3': nrm((L, HY_FILTER_W, HY_ORDER * HY_WIDTH), HY_FILTER_W ** -0.5),
        'hy_freq': gain((L, HY_FILTER_W)),
        'hy_skip': nrm((L, HY_ORDER, HY_WIDTH), 1.0),
        'w_hy_o': nrm((L, HY_WIDTH, D), HY_WIDTH ** -0.5),
        'w_out': nrm((L, D, D), D ** -0.5),
        'g_ffn2': gain((L, D)),
        'w13_ffn2': nrm((L, D, 2 * D_FF), D ** -0.5),
        'w2_ffn2': nrm((L, D_FF, D), D_FF ** -0.5),
        'g_final': gain((D,)),
    }


def reference(x, c, ctx, c_ctx, w_mod, b_mod, g_ffn1, w13_ffn1, w2_ffn1, g_mix, w_in,
              g_q, w_uq, g_kv, w_ukv, w_attn_o, hy_conv_w, hy_conv_b, hy_w1, hy_b1,
              hy_w2, hy_b2, hy_w3, hy_freq, hy_skip, w_hy_o, w_out, g_ffn2, w13_ffn2,
              w2_ffn2, g_final):
    B, S, _ = x.shape
    CL = ctx.shape[1]
    ROWS = S // GRID_W
    rows = jnp.repeat(jnp.arange(ROWS, dtype=jnp.float32), GRID_W)
    cols = jnp.tile(jnp.arange(GRID_W, dtype=jnp.float32), ROWS)
    inv_freq = ROPE_THETA ** (-jnp.arange(0, ROPE_AXIS, 2, dtype=jnp.float32) / ROPE_AXIS)
    ang_row = rows[:, None] * inv_freq
    ang_col = cols[:, None] * inv_freq

    for li in range(DEPTH):
        last = li == DEPTH - 1
        mx = (jax.nn.silu(c) @ w_mod[li] + b_mod[li]).reshape(B, N_MOD, 1, D_MODEL)
        mc = (jax.nn.silu(c_ctx) @ w_mod[li] + b_mod[li]).reshape(N_MOD, D_MODEL)

        x = half_ffn(x, mx[:, 0], mx[:, 1], mx[:, 2], g_ffn1[li], w13_ffn1[li], w2_ffn1[li])
        ctx = half_ffn(ctx, mc[0], mc[1], mc[2], g_ffn1[li], w13_ffn1[li], w2_ffn1[li])

        hx = modulate(rms_norm(x, g_mix[li]), mx[:, 3], mx[:, 4])
        hc = modulate(rms_norm(ctx, g_mix[li]), mc[3], mc[4])
        xq, xkv, xkr, xhy, xgate = split_proj(hx @ w_in[li])
        if last:
            ckvr = hc @ w_in[li][:, Q_END:KR_END]
            ckv, ckr = ckvr[..., :KV_LORA], ckvr[..., KV_LORA:]
        else:
            cq, ckv, ckr, chy, cgate = split_proj(hc @ w_in[li])

        q_nope, q_rope = mla_query(xq, g_q[li], w_uq[li])
        q_lat = jnp.concatenate([q_nope, rope_2d(q_rope, ang_row[:, None], ang_col[:, None])], axis=-1)
        k_nope, v_lat = mla_kv(xkv, g_kv[li], w_ukv[li])
        k_lat = join_key(k_nope, rope_2d(xkr, ang_row, ang_col))
        kc_nope, v_ctx = mla_kv(ckv, g_kv[li], w_ukv[li])
        k_ctx = join_key(kc_nope, ckr)
        k_all = jnp.concatenate([k_lat, k_ctx], axis=1)
        v_all = jnp.concatenate([v_lat, v_ctx], axis=1)
        attn_x = blocked_attention(q_lat, k_all, v_all).reshape(B, S, MLA_HEADS * V_HEAD)

        filt_x = hyena_filters(S, hy_w1[li], hy_b1[li], hy_w2[li], hy_b2[li], hy_w3[li], hy_freq[li])
        hy_x = hyena(xhy, hy_conv_w[li], hy_conv_b[li], filt_x, hy_skip[li])
        x_mix = merge_branches(attn_x, hy_x, xgate, w_attn_o[li], w_hy_o[li], w_out[li])

        if not last:
            cq_nope, cq_rope = mla_query(cq, g_q[li], w_uq[li])
            q_ctx = jnp.concatenate([cq_nope, cq_rope], axis=-1)
            attn_c = softmax_attend(q_ctx, k_ctx, v_ctx).reshape(B, CL, MLA_HEADS * V_HEAD)
            filt_c = hyena_filters(CL, hy_w1[li], hy_b1[li], hy_w2[li], hy_b2[li], hy_w3[li], hy_freq[li])
            hy_c = hyena(chy, hy_conv_w[li], hy_conv_b[li], filt_c, hy_skip[li])
            ctx = ctx + mc[5] * merge_branches(attn_c, hy_c, cgate, w_attn_o[li], w_hy_o[li], w_out[li])
            ctx = half_ffn(ctx, mc[6], mc[7], mc[8], g_ffn2[li], w13_ffn2[li], w2_ffn2[li])

        x = x + mx[:, 5] * x_mix
        x = half_ffn(x, mx[:, 6], mx[:, 7], mx[:, 8], g_ffn2[li], w13_ffn2[li], w2_ffn2[li])

    return rms_norm(x, g_final)
```

```python
import functools
import math

import numpy as np
import jax
import jax.numpy as jnp
from jax import lax
from jax.experimental import pallas as pl
from jax.experimental.pallas import tpu as pltpu

F32 = jnp.float32
BF16 = jnp.bfloat16

D_MODEL = 2048
SEQ = 8192
GRID_W = 64
CTX_LEN = 256
N_MOD = 9
D_FF = 5632
NORM_EPS = 1e-6

MLA_HEADS = 16
Q_LORA = 512
KV_LORA = 512
QK_NOPE = 128
QK_ROPE = 64
V_HEAD = 128
ROPE_AXIS = QK_ROPE // 2
ROPE_THETA = 10000.0
ATTN_SCALE = (QK_NOPE + QK_ROPE) ** -0.5
HEAD_PAD = 256

HY_WIDTH = 1024
HY_ORDER = 2
HY_EMB = 33
HY_BANDS = (HY_EMB - 1) // 2
HY_FILTER_W = 64
HY_TARGET = 1e-2
HY_FAST_DECAY = 0.3
HY_SLOW_DECAY = 1.5
HY_MAX_DECAY = math.log(HY_TARGET) / HY_FAST_DECAY
HY_MIN_DECAY = math.log(HY_TARGET) / HY_SLOW_DECAY
HY_MOD_SHIFT = 0.05

Q_END = Q_LORA
KV_END = Q_END + KV_LORA
KR_END = KV_END + QK_ROPE
HY_END = KR_END + 3 * HY_WIDTH

FFT_N = 2 * SEQ
FFT_N1 = 128
FFT_N2 = 128
FFT_HALF = SEQ // FFT_N2

V7X_VMEM_LIMIT = 56 * 1024 * 1024


def _cparams(sem, vmem=V7X_VMEM_LIMIT):
    return pltpu.CompilerParams(dimension_semantics=sem, vmem_limit_bytes=vmem)


def _sigmoid(x):
    return 1.0 / (1.0 + jnp.exp(-x))


def _rms(x, g):
    var = jnp.mean(x * x, axis=-1, keepdims=True)
    return x * lax.rsqrt(var + NORM_EPS) * g


def _mod_kernel(c_ref, w_ref, b_ref, o_ref):
    c = c_ref[...]
    a = c * _sigmoid(c)
    o_ref[...] = jnp.dot(a, w_ref[...], preferred_element_type=F32) + b_ref[...]


def _modulation(c8, w_mod, b_mod):
    n = w_mod.shape[1]
    tn = 1024
    return pl.pallas_call(
        _mod_kernel,
        grid=(n // tn,),
        in_specs=[
            pl.BlockSpec((8, D_MODEL), lambda j: (0, 0)),
            pl.BlockSpec((D_MODEL, tn), lambda j: (0, j)),
            pl.BlockSpec((1, tn), lambda j: (0, j)),
        ],
        out_specs=pl.BlockSpec((8, tn), lambda j: (0, j)),
        out_shape=jax.ShapeDtypeStruct((8, n), F32),
        compiler_params=_cparams(("arbitrary",)),
        name="modulation",
    )(c8, w_mod, b_mod.reshape(1, n))


def _ffn_kernel(s_ref, mod_ref, g_ref, w1_ref, w3_ref, w2_ref, gf_ref, o_ref,
                h_scr, acc_scr, *, final_norm):
    f = pl.program_id(1)

    @pl.when(f == 0)
    def _():
        h = _rms(s_ref[...], g_ref[...])
        h = h * (1.0 + mod_ref[1:2, :]) + mod_ref[0:1, :]
        h_scr[...] = h.astype(BF16)
        acc_scr[...] = jnp.zeros_like(acc_scr)

    h = h_scr[...]
    a = jnp.dot(h, w1_ref[...], preferred_element_type=F32)
    b = jnp.dot(h, w3_ref[...], preferred_element_type=F32)
    act = (a * _sigmoid(a) * b).astype(BF16)
    acc_scr[...] += jnp.dot(act, w2_ref[...], preferred_element_type=F32)

    @pl.when(f == pl.num_programs(1) - 1)
    def _():
        out = s_ref[...] + 0.5 * mod_ref[2:3, :] * acc_scr[...]
        if final_norm:
            out = _rms(out, gf_ref[...])
        o_ref[...] = out


def _half_ffn(s, mod3, g, w13, w2, g_final, *, tm, final_norm):
    rows = s.shape[0]
    tf = 512
    nf = D_FF // tf
    return pl.pallas_call(
        functools.partial(_ffn_kernel, final_norm=final_norm),
        grid=(rows // tm, nf),
        in_specs=[
            pl.BlockSpec((tm, D_MODEL), lambda i, f: (i, 0)),
            pl.BlockSpec((8, D_MODEL), lambda i, f: (0, 0)),
            pl.BlockSpec((1, D_MODEL), lambda i, f: (0, 0)),
            pl.BlockSpec((D_MODEL, tf), lambda i, f: (0, f)),
            pl.BlockSpec((D_MODEL, tf), lambda i, f: (0, f + nf)),
            pl.BlockSpec((tf, D_MODEL), lambda i, f: (f, 0)),
            pl.BlockSpec((1, D_MODEL), lambda i, f: (0, 0)),
        ],
        out_specs=pl.BlockSpec((tm, D_MODEL), lambda i, f: (i, 0)),
        out_shape=jax.ShapeDtypeStruct((rows, D_MODEL), F32),
        scratch_shapes=[pltpu.VMEM((tm, D_MODEL), BF16), pltpu.VMEM((tm, D_MODEL), F32)],
        compiler_params=_cparams(("arbitrary", "arbitrary")),
        name="half_ffn",
    )(s, mod3, g.reshape(1, D_MODEL), w13, w13, w2, g_final.reshape(1, D_MODEL))


def _prenorm_kernel(s_ref, mod_ref, g_ref, o_ref):
    h = _rms(s_ref[...], g_ref[...])
    o_ref[...] = (h * (1.0 + mod_ref[1:2, :]) + mod_ref[0:1, :]).astype(BF16)


def _prenorm(s, mod3, g, *, tm):
    rows = s.shape[0]
    return pl.pallas_call(
        _prenorm_kernel,
        grid=(rows // tm,),
        in_specs=[
            pl.BlockSpec((tm, D_MODEL), lambda i: (i, 0)),
            pl.BlockSpec((8, D_MODEL), lambda i: (0, 0)),
            pl.BlockSpec((1, D_MODEL), lambda i: (0, 0)),
        ],
        out_specs=pl.BlockSpec((tm, D_MODEL), lambda i: (i, 0)),
        out_shape=jax.ShapeDtypeStruct((rows, D_MODEL), BF16),
        compiler_params=_cparams(("arbitrary",)),
        name="prenorm",
    )(s, mod3, g.reshape(1, D_MODEL))


def _proj_hg_kernel(h_ref, w_ref, o_ref, *, n_gate_tiles):
    j = pl.program_id(1)
    acc = jnp.dot(h_ref[...], w_ref[...], preferred_element_type=F32)

    @pl.when(j < n_gate_tiles)
    def _():
        o_ref[...] = _sigmoid(acc).astype(BF16)

    @pl.when(j >= n_gate_tiles)
    def _():
        o_ref[...] = acc.astype(BF16)


def _proj_hg(h, w_hg):
    rows, n = h.shape[0], w_hg.shape[1]
    tm, tn = 1024, 1024
    return pl.pallas_call(
        functools.partial(_proj_hg_kernel, n_gate_tiles=2 * D_MODEL // tn),
        grid=(rows // tm, n // tn),
        in_specs=[
            pl.BlockSpec((tm, D_MODEL), lambda i, j: (i, 0)),
            pl.BlockSpec((D_MODEL, tn), lambda i, j: (0, j)),
        ],
        out_specs=pl.BlockSpec((tm, tn), lambda i, j: (i, j)),
        out_shape=jax.ShapeDtypeStruct((rows, n), BF16),
        compiler_params=_cparams(("arbitrary", "arbitrary")),
        name="proj_gate_hyena",
    )(h, w_hg)


def _proj_attn_kernel(h_ref, wa_ref, gq_ref, gkv_ref, wuq_ref, wukv_ref, tq_ref, tk_ref,
                      *out_refs, with_q):
    if with_q:
        q_ref, k_ref, v_ref = out_refs
    else:
        k_ref, v_ref = out_refs
    p = jnp.dot(h_ref[...], wa_ref[...], preferred_element_type=F32)
    kvn = _rms(p[:, Q_LORA:Q_LORA + KV_LORA], gkv_ref[...]).astype(BF16)
    t = p[:, KV_END:KV_END + 2 * QK_ROPE] * tk_ref[...]
    krot = (t + pltpu.roll(t, QK_ROPE, 1)).astype(BF16)
    kn = jnp.dot(kvn, wukv_ref[:, :MLA_HEADS * QK_NOPE], preferred_element_type=F32)
    for hd in range(MLA_HEADS):
        k_ref[:, hd * HEAD_PAD:hd * HEAD_PAD + QK_NOPE] = (
            kn[:, hd * QK_NOPE:(hd + 1) * QK_NOPE].astype(BF16))
        k_ref[:, hd * HEAD_PAD + QK_NOPE:(hd + 1) * HEAD_PAD] = krot
    v_ref[...] = jnp.dot(kvn, wukv_ref[:, MLA_HEADS * QK_NOPE:],
                         preferred_element_type=F32).astype(BF16)
    if with_q:
        qn = _rms(p[:, :Q_LORA], gq_ref[...]).astype(BF16)
        tq = tq_ref[...]
        for hd in range(MLA_HEADS):
            qh = jnp.dot(qn, wuq_ref[:, hd * HEAD_PAD:(hd + 1) * HEAD_PAD],
                         preferred_element_type=F32)
            q_ref[:, hd * HEAD_PAD:(hd + 1) * HEAD_PAD] = (qh * tq).astype(BF16)


def _proj_attn(h, w_a, g_q, g_kv, w_uq, w_ukv, tab_q, tab_k, *, tm, with_q):
    rows = h.shape[0]
    na = w_a.shape[1]
    hp = MLA_HEADS * HEAD_PAD
    hv = MLA_HEADS * V_HEAD
    const = lambda i: (0, 0)
    row = lambda i: (i, 0)
    out_shape = [jax.ShapeDtypeStruct((rows, hp), BF16), jax.ShapeDtypeStruct((rows, hv), BF16)]
    out_specs = [pl.BlockSpec((tm, hp), row), pl.BlockSpec((tm, hv), row)]
    if with_q:
        out_shape = [jax.ShapeDtypeStruct((rows, hp), BF16)] + out_shape
        out_specs = [pl.BlockSpec((tm, hp), row)] + out_specs
    return pl.pallas_call(
        functools.partial(_proj_attn_kernel, with_q=with_q),
        grid=(rows // tm,),
        in_specs=[
            pl.BlockSpec((tm, D_MODEL), row),
            pl.BlockSpec((D_MODEL, na), const),
            pl.BlockSpec((1, Q_LORA), const),
            pl.BlockSpec((1, KV_LORA), const),
            pl.BlockSpec((Q_LORA, hp), const),
            pl.BlockSpec((KV_LORA, MLA_HEADS * (QK_NOPE + V_HEAD)), const),
            pl.BlockSpec((tm, HEAD_PAD), row),
            pl.BlockSpec((tm, 2 * QK_ROPE), row),
        ],
        out_specs=out_specs,
        out_shape=out_shape,
        compiler_params=_cparams(("arbitrary",)),
        name="proj_attn",
    )(h, w_a, g_q.reshape(1, Q_LORA), g_kv.reshape(1, KV_LORA), w_uq, w_ukv, tab_q, tab_k)


ATT_TQ = 256
ATT_KC = 2048
_NT = (((1,), (1,)), ((), ()))


def _attn_kernel(q_ref, kl_ref, kc_ref, vl_ref, vc_ref, o_ref, s_scr):
    q = q_ref[...]
    n_chunks = SEQ // ATT_KC
    m = None
    for c in range(n_chunks):
        s = lax.dot_general(q, kl_ref[c * ATT_KC:(c + 1) * ATT_KC, :], _NT,
                            preferred_element_type=F32)
        s_scr[:, c * ATT_KC:(c + 1) * ATT_KC] = s
        mc = jnp.max(s, axis=-1, keepdims=True)
        m = mc if m is None else jnp.maximum(m, mc)
    s = lax.dot_general(q, kc_ref[...], _NT, preferred_element_type=F32)
    s_scr[:, SEQ:SEQ + CTX_LEN] = s
    m = jnp.maximum(m, jnp.max(s, axis=-1, keepdims=True))

    p = jnp.exp(s_scr[:, SEQ:SEQ + CTX_LEN] - m)
    l = jnp.sum(p, axis=-1, keepdims=True)
    acc = jnp.dot(p.astype(BF16), vc_ref[...], preferred_element_type=F32)
    for c in range(n_chunks):
        p = jnp.exp(s_scr[:, c * ATT_KC:(c + 1) * ATT_KC] - m)
        l = l + jnp.sum(p, axis=-1, keepdims=True)
        acc = acc + jnp.dot(p.astype(BF16), vl_ref[c * ATT_KC:(c + 1) * ATT_KC, :],
                            preferred_element_type=F32)
    o_ref[...] = (acc / l).astype(BF16)


def _attention(q, k_lat, k_ctx, v_lat, v_ctx):
    return pl.pallas_call(
        _attn_kernel,
        grid=(MLA_HEADS, SEQ // ATT_TQ),
        in_specs=[
            pl.BlockSpec((ATT_TQ, HEAD_PAD), lambda h, i: (i, h)),
            pl.BlockSpec((SEQ, HEAD_PAD), lambda h, i: (0, h)),
            pl.BlockSpec((CTX_LEN, HEAD_PAD), lambda h, i: (0, h)),
            pl.BlockSpec((SEQ, V_HEAD), lambda h, i: (0, h)),
            pl.BlockSpec((CTX_LEN, V_HEAD), lambda h, i: (0, h)),
        ],
        out_specs=pl.BlockSpec((ATT_TQ, V_HEAD), lambda h, i: (i, h)),
        out_shape=jax.ShapeDtypeStruct((SEQ, MLA_HEADS * V_HEAD), BF16),
        scratch_shapes=[pltpu.VMEM((ATT_TQ, SEQ + CTX_LEN), F32)],
        compiler_params=_cparams(("arbitrary", "arbitrary")),
        name="attention",
    )(q, k_lat, k_ctx, v_lat, v_ctx)


SC_HALO = 16


def _short_conv_kernel(x_ref, prev_ref, next_ref, w_ref, b_ref, o_ref):
    i = pl.program_id(0)
    tm = x_ref.shape[0]
    x = x_ref[...].astype(F32)
    prev_row = prev_ref[SC_HALO - 1:SC_HALO, :].astype(F32) * (i > 0).astype(F32)
    next_row = next_ref[0:1, :].astype(F32) * (i < pl.num_programs(0) - 1).astype(F32)
    rows = lax.broadcasted_iota(jnp.int32, x.shape, 0)
    up = jnp.where(rows == 0, prev_row, pltpu.roll(x, 1, 0))
    dn = jnp.where(rows == tm - 1, next_row, pltpu.roll(x, tm - 1, 0))
    o_ref[...] = up * w_ref[0:1, :] + x * w_ref[1:2, :] + dn * w_ref[2:3, :] + b_ref[...]


def _short_conv(hg, conv_w, conv_b, *, col0):
    tm, tc = 512, 1024
    nb = 3 * HY_WIDTH // tc
    cb0 = col0 // tc
    hb = tm // SC_HALO
    last = SEQ // SC_HALO - 1
    w8 = jnp.zeros((8, 3 * HY_WIDTH), F32).at[:3].set(conv_w)
    return pl.pallas_call(
        _short_conv_kernel,
        grid=(SEQ // tm, nb),
        in_specs=[
            pl.BlockSpec((tm, tc), lambda i, j: (i, j + cb0)),
            pl.BlockSpec((SC_HALO, tc), lambda i, j: (jnp.maximum(i * hb - 1, 0), j + cb0)),
            pl.BlockSpec((SC_HALO, tc), lambda i, j: (jnp.minimum((i + 1) * hb, last), j + cb0)),
            pl.BlockSpec((8, tc), lambda i, j: (0, j)),
            pl.BlockSpec((1, tc), lambda i, j: (0, j)),
        ],
        out_specs=pl.BlockSpec((tm, tc), lambda i, j: (i, j)),
        out_shape=jax.ShapeDtypeStruct((SEQ, 3 * HY_WIDTH), F32),
        compiler_params=_cparams(("arbitrary", "arbitrary")),
        name="short_conv",
    )(hg, hg, hg, w8, conv_b.reshape(1, -1))


HY_ZW = 128


def _filter_kernel(z_ref, w1_ref, b1_ref, w2_ref, b2_ref, w3_ref, fr_ref, dec_ref,
                   h_ref, ss_ref):
    hi = lax.Precision.HIGHEST
    z = z_ref[...]
    fr = fr_ref[...]
    h = jnp.sin(fr * (jnp.dot(z, w1_ref[...], precision=hi, preferred_element_type=F32)
                      + b1_ref[...]))
    h = jnp.sin(fr * (jnp.dot(h, w2_ref[...], precision=hi, preferred_element_type=F32)
                      + b2_ref[...]))
    h = jnp.dot(h, w3_ref[...], precision=hi, preferred_element_type=F32)
    dist = z[:, HY_EMB:HY_EMB + 1]
    h = h * (jnp.exp(-dist * dec_ref[...]) + HY_MOD_SHIFT)
    h_ref[...] = h

    @pl.when(pl.program_id(0) == 0)
    def _():
        ss_ref[...] = jnp.zeros_like(ss_ref)

    ss_ref[...] += jnp.sum(h * h, axis=0, keepdims=True)


def _hyena_filters(z_p, w1p, b1, w2, b2, w3, freq, decay2):
    tl = 512
    nw = HY_ORDER * HY_WIDTH
    const = lambda i: (0, 0)
    return pl.pallas_call(
        _filter_kernel,
        grid=(SEQ // tl,),
        in_specs=[
            pl.BlockSpec((tl, HY_ZW), lambda i: (i, 0)),
            pl.BlockSpec((HY_ZW, HY_FILTER_W), const),
            pl.BlockSpec((1, HY_FILTER_W), const),
            pl.BlockSpec((HY_FILTER_W, HY_FILTER_W), const),
            pl.BlockSpec((1, HY_FILTER_W), const),
            pl.BlockSpec((HY_FILTER_W, nw), const),
            pl.BlockSpec((1, HY_FILTER_W), const),
            pl.BlockSpec((1, nw), const),
        ],
        out_specs=[pl.BlockSpec((tl, nw), lambda i: (i, 0)), pl.BlockSpec((1, nw), const)],
        out_shape=[jax.ShapeDtypeStruct((SEQ, nw), F32), jax.ShapeDtypeStruct((1, nw), F32)],
        compiler_params=_cparams(("arbitrary",)),
        name="hyena_filters",
    )(z_p, w1p, b1.reshape(1, -1), w2, b2.reshape(1, -1), w3, freq.reshape(1, -1), decay2)


FFT_G = 16


def _fft_a_kernel(x_ref, m_ref, ss_ref, o_ref, *, normalise):
    x = x_ref[...]
    if normalise:
        x = x * lax.rsqrt(ss_ref[...] + NORM_EPS)
    o_ref[...] = jnp.einsum("gkn,gnc->gkc", m_ref[...], x.astype(BF16),
                            preferred_element_type=F32).astype(BF16)


def _fft_a(x_p, m_a, sumsq, *, col0, normalise):
    tc = HY_WIDTH
    cb = col0 // tc
    return pl.pallas_call(
        functools.partial(_fft_a_kernel, normalise=normalise),
        grid=(FFT_N2 // FFT_G,),
        in_specs=[
            pl.BlockSpec((FFT_G, FFT_HALF, tc), lambda g: (g, 0, cb)),
            pl.BlockSpec((FFT_G, 2 * FFT_N1, FFT_HALF), lambda g: (g, 0, 0)),
            pl.BlockSpec((1, tc), lambda g: (0, cb)),
        ],
        out_specs=pl.BlockSpec((FFT_G, 2 * FFT_N1, tc), lambda g: (g, 0, 0)),
        out_shape=jax.ShapeDtypeStruct((FFT_N2, 2 * FFT_N1, tc), BF16),
        compiler_params=_cparams(("arbitrary",)),
        name="fft_stage_a",
    )(x_p, m_a, sumsq)


def _fft_b_kernel(au_ref, ah_ref, g2_ref, g2i_ref, o_ref):
    g2 = g2_ref[...]
    g2i = g2i_ref[...]
    for g in range(au_ref.shape[0]):
        zu = jnp.dot(g2, au_ref[g], preferred_element_type=F32)
        zh = jnp.dot(g2, ah_ref[g], preferred_element_type=F32)
        ur, ui = zu[:FFT_N2], zu[FFT_N2:]
        hr, hi = zh[:FFT_N2], zh[FFT_N2:]
        pr = ur * hr - ui * hi
        pi = ur * hi + ui * hr
        p = jnp.concatenate([pr, pi], axis=0).astype(BF16)
        o_ref[g] = jnp.dot(g2i, p, preferred_element_type=F32).astype(BF16)


def _fft_b(au, ah, g2, g2i):
    gb, tc = 8, HY_WIDTH
    blk = pl.BlockSpec((gb, 2 * FFT_N2, tc), lambda g: (g, 0, 0))
    const = pl.BlockSpec((2 * FFT_N2, 2 * FFT_N2), lambda g: (0, 0))
    return pl.pallas_call(
        _fft_b_kernel,
        grid=(FFT_N1 // gb,),
        in_specs=[blk, blk, const, const],
        out_specs=blk,
        out_shape=jax.ShapeDtypeStruct((FFT_N1, 2 * FFT_N2, tc), BF16),
        compiler_params=_cparams(("arbitrary",)),
        name="fft_stage_b",
    )(au, ah, g2, g2i)


def _fft_c_kernel(b_ref, m_ref, u_ref, x_ref, skip_ref, o_ref):
    y = jnp.einsum("gmk,gkc->gmc", m_ref[...], b_ref[...], preferred_element_type=F32)
    o_ref[...] = x_ref[...] * (y + skip_ref[...] * u_ref[...])


def _fft_c(b_t, m_c, u_p, ucol, x_p, xcol, skip):
    tc = HY_WIDTH
    ub, xb = ucol // tc, xcol // tc
    return pl.pallas_call(
        _fft_c_kernel,
        grid=(FFT_N2 // FFT_G,),
        in_specs=[
            pl.BlockSpec((FFT_G, 2 * FFT_N1, tc), lambda g: (g, 0, 0)),
            pl.BlockSpec((FFT_G, FFT_HALF, 2 * FFT_N1), lambda g: (g, 0, 0)),
            pl.BlockSpec((FFT_G, FFT_HALF, tc), lambda g: (g, 0, ub)),
            pl.BlockSpec((FFT_G, FFT_HALF, tc), lambda g: (g, 0, xb)),
            pl.BlockSpec((1, 1, tc), lambda g: (0, 0, 0)),
        ],
        out_specs=pl.BlockSpec((FFT_G, FFT_HALF, tc), lambda g: (g, 0, 0)),
        out_shape=jax.ShapeDtypeStruct((FFT_N2, FFT_HALF, tc), F32),
        compiler_params=_cparams(("arbitrary",)),
        name="fft_stage_c",
    )(b_t, m_c, u_p, x_p, skip.reshape(1, 1, tc))


def _swap_major(a):
    n, m2, c = a.shape
    m = m2 // 2
    return a.reshape(n, 2, m, c).transpose(2, 1, 0, 3).reshape(m, 2 * n, c)


def _merge_kernel(a_ref, hy_ref, ga_ref, gh_ref, wa_ref, wh_ref, o_ref):
    ya = jnp.dot(a_ref[...], wa_ref[...], preferred_element_type=F32)
    yh = jnp.dot(hy_ref[...].astype(BF16), wh_ref[...], preferred_element_type=F32)
    o_ref[...] = (ga_ref[...].astype(F32) * ya + gh_ref[...].astype(F32) * yh).astype(BF16)


def _merge(attn, hy, hg, w_attn_o, w_hy_o):
    tm = 512
    row = lambda i: (i, 0)
    const = lambda i: (0, 0)
    return pl.pallas_call(
        _merge_kernel,
        grid=(SEQ // tm,),
        in_specs=[
            pl.BlockSpec((tm, MLA_HEADS * V_HEAD), row),
            pl.BlockSpec((tm, HY_WIDTH), row),
            pl.BlockSpec((tm, D_MODEL), lambda i: (i, 0)),
            pl.BlockSpec((tm, D_MODEL), lambda i: (i, 1)),
            pl.BlockSpec((MLA_HEADS * V_HEAD, D_MODEL), const),
            pl.BlockSpec((HY_WIDTH, D_MODEL), const),
        ],
        out_specs=pl.BlockSpec((tm, D_MODEL), row),
        out_shape=jax.ShapeDtypeStruct((SEQ, D_MODEL), BF16),
        compiler_params=_cparams(("arbitrary",)),
        name="merge_branches",
    )(attn, hy, hg, hg, w_attn_o, w_hy_o)


def _out_proj_kernel(y_ref, w_ref, s_ref, mod_ref, o_ref):
    xm = jnp.dot(y_ref[...], w_ref[...], preferred_element_type=F32)
    o_ref[...] = s_ref[...] + mod_ref[0:1, :] * xm


def _out_proj(y, w_out, s, mod_gate):
    tm = 512
    row = lambda i: (i, 0)
    const = lambda i: (0, 0)
    return pl.pallas_call(
        _out_proj_kernel,
        grid=(SEQ // tm,),
        in_specs=[
            pl.BlockSpec((tm, D_MODEL), row),
            pl.BlockSpec((D_MODEL, D_MODEL), const),
            pl.BlockSpec((tm, D_MODEL), row),
            pl.BlockSpec((8, D_MODEL), const),
        ],
        out_specs=pl.BlockSpec((tm, D_MODEL), row),
        out_shape=jax.ShapeDtypeStruct((SEQ, D_MODEL), F32),
        compiler_params=_cparams(("arbitrary",)),
        name="out_proj",
    )(y, w_out, s, mod_gate)


def _rope_tables():
    t = np.arange(SEQ)
    pos = np.stack([t // GRID_W, t % GRID_W], axis=1).astype(np.float64)
    inv_freq = ROPE_THETA ** (-np.arange(0, ROPE_AXIS, 2, dtype=np.float64) / ROPE_AXIS)
    i = np.arange(QK_ROPE)
    ang = pos[:, i // ROPE_AXIS] * inv_freq[i % (ROPE_AXIS // 2)][None, :]
    cos, sin = np.cos(ang), np.sin(ang)
    tab_q = np.concatenate([np.ones((SEQ, QK_NOPE)), cos, sin], axis=1) * ATTN_SCALE
    tab_k = np.concatenate([cos, sin], axis=1)
    tab_k_ctx = np.concatenate([np.ones((CTX_LEN, QK_ROPE)), np.zeros((CTX_LEN, QK_ROPE))], axis=1)
    return (jnp.asarray(tab_q, F32), jnp.asarray(tab_k, F32), jnp.asarray(tab_k_ctx, F32))


def _rope_swap():
    i = np.arange(QK_ROPE)
    first_half = (i % ROPE_AXIS) < ROPE_AXIS // 2
    partner = np.where(first_half, i + ROPE_AXIS // 2, i - ROPE_AXIS // 2)
    sign = np.where(first_half, -1.0, 1.0)
    return partner, sign


def _n2_major(a):
    return a.reshape(FFT_HALF, FFT_N2, -1).transpose(1, 0, 2)


def _filter_features():
    pos = np.arange(SEQ, dtype=np.float64)[:, None]
    t01 = pos / (SEQ - 1)
    bands = np.linspace(1e-4, HY_BANDS - 1, HY_BANDS)[None, :]
    ang = bands * (2.0 * math.pi / SEQ) * pos
    dist = np.abs(pos - (SEQ // 2)) / (SEQ / 2.0)
    z = np.concatenate([t01, np.cos(ang), -np.sin(ang), dist], axis=1)
    z = np.pad(z, ((0, 0), (0, HY_ZW - z.shape[1])))
    return jnp.asarray(_n2_major(z).reshape(SEQ, HY_ZW), F32)


def _dft_tables():
    n1 = np.arange(FFT_HALF)
    k1 = np.arange(FFT_N1)
    n2 = np.arange(FFT_N2)
    n = FFT_N2 * n1[None, None, :] + n2[:, None, None]
    ph = (k1[None, :, None] * n) % FFT_N
    th = 2.0 * math.pi * ph / FFT_N
    m_a = np.concatenate([np.cos(th), -np.sin(th)], axis=1)
    ph2 = (n2[:, None] * n2[None, :]) % FFT_N2
    th2 = 2.0 * math.pi * ph2 / FFT_N2
    c, s = np.cos(th2), np.sin(th2)
    g2 = np.block([[c, s], [-s, c]])
    g2i = np.block([[c, -s], [s, c]])
    n = FFT_N2 * (n1[None, :, None] + FFT_HALF // 2) + n2[:, None, None]
    ph = (k1[None, None, :] * n) % FFT_N
    th = 2.0 * math.pi * ph / FFT_N
    m_c = np.concatenate([np.cos(th), -np.sin(th)], axis=2) / FFT_N
    return tuple(jnp.asarray(t, F32).astype(BF16) for t in (m_a, g2, g2i, m_c))


def _rows8(m, lo, hi):
    return jnp.zeros((8, D_MODEL), F32).at[:hi - lo].set(m[lo:hi])


def kernel(x, c, ctx, c_ctx, w_mod, b_mod, g_ffn1, w13_ffn1, w2_ffn1, g_mix, w_in, g_q, w_uq,
           g_kv, w_ukv, w_attn_o, hy_conv_w, hy_conv_b, hy_w1, hy_b1, hy_w2, hy_b2, hy_w3,
           hy_freq, hy_skip, w_hy_o, w_out, g_ffn2, w13_ffn2, w2_ffn2, g_final):
    xs = x[0]
    cs = ctx[0]
    li = 0

    c8 = jnp.zeros((8, D_MODEL), F32).at[0].set(c[0]).at[1].set(c_ctx)
    mod = _modulation(c8, w_mod[li], b_mod[li])
    mx = mod[0].reshape(N_MOD, D_MODEL)
    mc = mod[1].reshape(N_MOD, D_MODEL)

    w13 = w13_ffn1[li].astype(BF16)
    w2 = w2_ffn1[li].astype(BF16)
    x1 = _half_ffn(xs, _rows8(mx, 0, 3), g_ffn1[li], w13, w2, g_final, tm=512, final_norm=False)
    c1 = _half_ffn(cs, _rows8(mc, 0, 3), g_ffn1[li], w13, w2, g_final, tm=CTX_LEN,
                   final_norm=False)

    hx = _prenorm(x1, _rows8(mx, 3, 5), g_mix[li], tm=512)
    hc = _prenorm(c1, _rows8(mc, 3, 5), g_mix[li], tm=CTX_LEN)

    win = w_in[li]
    partner, sign = _rope_swap()
    w_kr = win[:, KV_END:KR_END]
    w_a = jnp.concatenate([win[:, :KV_END], w_kr, w_kr[:, partner] * sign], axis=1).astype(BF16)
    w_hg = jnp.concatenate([win[:, HY_END:], win[:, KR_END:HY_END]], axis=1).astype(BF16)
    wq = w_uq[li].reshape(Q_LORA, MLA_HEADS, QK_NOPE + QK_ROPE)
    wq_r = wq[:, :, QK_NOPE:]
    wq_p = jnp.concatenate([wq[:, :, :QK_NOPE], wq_r, wq_r[:, :, partner] * sign], axis=2)
    wq_p = wq_p.reshape(Q_LORA, MLA_HEADS * HEAD_PAD).astype(BF16)
    wkv = w_ukv[li].reshape(KV_LORA, MLA_HEADS, QK_NOPE + V_HEAD)
    wkv_p = jnp.concatenate([wkv[:, :, :QK_NOPE].reshape(KV_LORA, -1),
                             wkv[:, :, QK_NOPE:].reshape(KV_LORA, -1)], axis=1).astype(BF16)

    tab_q, tab_k, tab_k_ctx = _rope_tables()
    q, k_lat, v_lat = _proj_attn(hx, w_a, g_q[li], g_kv[li], wq_p, wkv_p, tab_q, tab_k,
                                 tm=512, with_q=True)
    k_ctx, v_ctx = _proj_attn(hc, w_a, g_q[li], g_kv[li], wq_p, wkv_p, tab_q[:CTX_LEN],
                              tab_k_ctx, tm=CTX_LEN, with_q=False)
    attn = _attention(q, k_lat, k_ctx, v_lat, v_ctx)

    hg = _proj_hg(hx, w_hg)
    u3 = _short_conv(hg, hy_conv_w[li], hy_conv_b[li], col0=2 * D_MODEL)
    u3p = _n2_major(u3)
    w1p = jnp.zeros((HY_ZW, HY_FILTER_W), F32).at[:HY_EMB].set(hy_w1[li])
    decay = np.abs(np.linspace(HY_MIN_DECAY, HY_MAX_DECAY, HY_WIDTH))
    decay2 = jnp.asarray(np.tile(decay, HY_ORDER)[None, :], F32)
    filt, sumsq = _hyena_filters(_filter_features(), w1p, hy_b1[li], hy_w2[li], hy_b2[li],
                                 hy_w3[li], hy_freq[li], decay2)
    filt_p = filt.reshape(FFT_N2, FFT_HALF, HY_ORDER * HY_WIDTH)
    m_a, g2, g2i, m_c = _dft_tables()
    ones = jnp.ones((1, HY_WIDTH), F32)

    def long_conv_gate(sig_p, sig_col, order, gate_p, gate_col):
        a_u = _fft_a(sig_p, m_a, ones, col0=sig_col, normalise=False)
        a_h = _fft_a(filt_p, m_a, sumsq, col0=order * HY_WIDTH, normalise=True)
        b = _fft_b(_swap_major(a_u), _swap_major(a_h), g2, g2i)
        return _fft_c(_swap_major(b), m_c, sig_p, sig_col, gate_p, gate_col, hy_skip[li][order])

    z_p = long_conv_gate(u3p, 0, 0, u3p, HY_WIDTH)
    hy_p = long_conv_gate(z_p, 0, 1, u3p, 2 * HY_WIDTH)
    hy = hy_p.transpose(1, 0, 2).reshape(SEQ, HY_WIDTH)

    y = _merge(attn, hy, hg, w_attn_o[li].astype(BF16), w_hy_o[li].astype(BF16))
    x2 = _out_proj(y, w_out[li].astype(BF16), x1, _rows8(mx, 5, 6))

    out = _half_ffn(x2, _rows8(mx, 6, 9), g_ffn2[li], w13_ffn2[li].astype(BF16),
                    w2_ffn2[li].astype(BF16), g_final, tm=512, final_norm=True)
    return out[None]
```

```python
import functools
import math

import numpy as np
import jax
import jax.numpy as jnp
from jax import lax
from jax.experimental import pallas as pl
from jax.experimental.pallas import tpu as pltpu

F32 = jnp.float32
BF16 = jnp.bfloat16

D_MODEL = 2048
SEQ = 8192
GRID_W = 64
CTX_LEN = 256
N_MOD = 9
D_FF = 5632
NORM_EPS = 1e-6

MLA_HEADS = 16
Q_LORA = 512
KV_LORA = 512
QK_NOPE = 128
QK_ROPE = 64
V_HEAD = 128
ROPE_AXIS = QK_ROPE // 2
ROPE_THETA = 10000.0
ATTN_SCALE = (QK_NOPE + QK_ROPE) ** -0.5
HEAD_PAD = 256

HY_WIDTH = 1024
HY_ORDER = 2
HY_EMB = 33
HY_BANDS = (HY_EMB - 1) // 2
HY_FILTER_W = 64
HY_TARGET = 1e-2
HY_FAST_DECAY = 0.3
HY_SLOW_DECAY = 1.5
HY_MAX_DECAY = math.log(HY_TARGET) / HY_FAST_DECAY
HY_MIN_DECAY = math.log(HY_TARGET) / HY_SLOW_DECAY
HY_MOD_SHIFT = 0.05

Q_END = Q_LORA
KV_END = Q_END + KV_LORA
KR_END = KV_END + QK_ROPE
HY_END = KR_END + 3 * HY_WIDTH

FFT_N = 2 * SEQ
FFT_N1 = 128
FFT_N2 = 128
FFT_HALF = SEQ // FFT_N2

V7X_VMEM_LIMIT = 56 * 1024 * 1024


def _cparams(sem, vmem=V7X_VMEM_LIMIT):
    return pltpu.CompilerParams(dimension_semantics=sem, vmem_limit_bytes=vmem)


def _sigmoid(x):
    return 1.0 / (1.0 + jnp.exp(-x))


def _rms(x, g):
    var = jnp.mean(x * x, axis=-1, keepdims=True)
    return x * lax.rsqrt(var + NORM_EPS) * g


def _mod_kernel(c_ref, w_ref, b_ref, o_ref):
    c = c_ref[...]
    a = c * _sigmoid(c)
    o_ref[...] = jnp.dot(a, w_ref[...], preferred_element_type=F32) + b_ref[...]


def _modulation(c8, w_mod, b_mod):
    n = w_mod.shape[1]
    tn = 1024
    return pl.pallas_call(
        _mod_kernel,
        grid=(n // tn,),
        in_specs=[
            pl.BlockSpec((8, D_MODEL), lambda j: (0, 0)),
            pl.BlockSpec((D_MODEL, tn), lambda j: (0, j)),
            pl.BlockSpec((1, tn), lambda j: (0, j)),
        ],
        out_specs=pl.BlockSpec((8, tn), lambda j: (0, j)),
        out_shape=jax.ShapeDtypeStruct((8, n), F32),
        compiler_params=_cparams(("arbitrary",)),
        name="modulation",
    )(c8, w_mod, b_mod.reshape(1, n))


def _ffn_kernel(s_ref, mod_ref, g_ref, w1_ref, w3_ref, w2_ref, gf_ref, o_ref,
                h_scr, acc_scr, *, final_norm):
    f = pl.program_id(1)

    @pl.when(f == 0)
    def _():
        h = _rms(s_ref[...], g_ref[...])
        h = h * (1.0 + mod_ref[1:2, :]) + mod_ref[0:1, :]
        h_scr[...] = h.astype(BF16)
        acc_scr[...] = jnp.zeros_like(acc_scr)

    h = h_scr[...]
    a = jnp.dot(h, w1_ref[...], preferred_element_type=F32)
    b = jnp.dot(h, w3_ref[...], preferred_element_type=F32)
    act = (a * _sigmoid(a) * b).astype(BF16)
    acc_scr[...] += jnp.dot(act, w2_ref[...], preferred_element_type=F32)

    @pl.when(f == pl.num_programs(1) - 1)
    def _():
        out = s_ref[...] + 0.5 * mod_ref[2:3, :] * acc_scr[...]
        if final_norm:
            out = _rms(out, gf_ref[...])
        o_ref[...] = out


def _half_ffn(s, mod3, g, w13, w2, g_final, *, tm, final_norm):
    rows = s.shape[0]
    tf = 512
    nf = D_FF // tf
    return pl.pallas_call(
        functools.partial(_ffn_kernel, final_norm=final_norm),
        grid=(rows // tm, nf),
        in_specs=[
            pl.BlockSpec((tm, D_MODEL), lambda i, f: (i, 0)),
            pl.BlockSpec((8, D_MODEL), lambda i, f: (0, 0)),
            pl.BlockSpec((1, D_MODEL), lambda i, f: (0, 0)),
            pl.BlockSpec((D_MODEL, tf), lambda i, f: (0, f)),
            pl.BlockSpec((D_MODEL, tf), lambda i, f: (0, f + nf)),
            pl.BlockSpec((tf, D_MODEL), lambda i, f: (f, 0)),
            pl.BlockSpec((1, D_MODEL), lambda i, f: (0, 0)),
        ],
        out_specs=pl.BlockSpec((tm, D_MODEL), lambda i, f: (i, 0)),
        out_shape=jax.ShapeDtypeStruct((rows, D_MODEL), F32),
        scratch_shapes=[pltpu.VMEM((tm, D_MODEL), BF16), pltpu.VMEM((tm, D_MODEL), F32)],
        compiler_params=_cparams(("arbitrary", "arbitrary")),
        name="half_ffn",
    )(s, mod3, g.reshape(1, D_MODEL), w13, w13, w2, g_final.reshape(1, D_MODEL))


def _prenorm_kernel(s_ref, mod_ref, g_ref, o_ref):
    h = _rms(s_ref[...], g_ref[...])
    o_ref[...] = (h * (1.0 + mod_ref[1:2, :]) + mod_ref[0:1, :]).astype(BF16)


def _prenorm(s, mod3, g, *, tm):
    rows = s.shape[0]
    return pl.pallas_call(
        _prenorm_kernel,
        grid=(rows // tm,),
        in_specs=[
            pl.BlockSpec((tm, D_MODEL), lambda i: (i, 0)),
            pl.BlockSpec((8, D_MODEL), lambda i: (0, 0)),
            pl.BlockSpec((1, D_MODEL), lambda i: (0, 0)),
        ],
        out_specs=pl.BlockSpec((tm, D_MODEL), lambda i: (i, 0)),
        out_shape=jax.ShapeDtypeStruct((rows, D_MODEL), BF16),
        compiler_params=_cparams(("arbitrary",)),
        name="prenorm",
    )(s, mod3, g.reshape(1, D_MODEL))


def _proj_hg_kernel(h_ref, w_ref, o_ref, *, n_gate_tiles):
    j = pl.program_id(1)
    acc = jnp.dot(h_ref[...], w_ref[...], preferred_element_type=F32)

    @pl.when(j < n_gate_tiles)
    def _():
        o_ref[...] = _sigmoid(acc).astype(BF16)

    @pl.when(j >= n_gate_tiles)
    def _():
        o_ref[...] = acc.astype(BF16)


def _proj_hg(h, w_hg):
    rows, n = h.shape[0], w_hg.shape[1]
    tm, tn = 1024, 1024
    return pl.pallas_call(
        functools.partial(_proj_hg_kernel, n_gate_tiles=2 * D_MODEL // tn),
        grid=(rows // tm, n // tn),
        in_specs=[
            pl.BlockSpec((tm, D_MODEL), lambda i, j: (i, 0)),
            pl.BlockSpec((D_MODEL, tn), lambda i, j: (0, j)),
        ],
        out_specs=pl.BlockSpec((tm, tn), lambda i, j: (i, j)),
        out_shape=jax.ShapeDtypeStruct((rows, n), BF16),
        compiler_params=_cparams(("arbitrary", "arbitrary")),
        name="proj_gate_hyena",
    )(h, w_hg)


def _proj_attn_kernel(h_ref, wa_ref, gq_ref, gkv_ref, wuq_ref, wukv_ref, tq_ref, tk_ref,
                      *out_refs, with_q):
    if with_q:
        q_ref, k_ref, v_ref = out_refs
    else:
        k_ref, v_ref = out_refs
    p = jnp.dot(h_ref[...], wa_ref[...], preferred_element_type=F32)
    kvn = _rms(p[:, Q_LORA:Q_LORA + KV_LORA], gkv_ref[...]).astype(BF16)
    t = p[:, KV_END:KV_END + 2 * QK_ROPE] * tk_ref[...]
    krot = (t + pltpu.roll(t, QK_ROPE, 1)).astype(BF16)
    kn = jnp.dot(kvn, wukv_ref[:, :MLA_HEADS * QK_NOPE], preferred_element_type=F32)
    for hd in range(MLA_HEADS):
        k_ref[:, hd * HEAD_PAD:hd * HEAD_PAD + QK_NOPE] = (
            kn[:, hd * QK_NOPE:(hd + 1) * QK_NOPE].astype(BF16))
        k_ref[:, hd * HEAD_PAD + QK_NOPE:(hd + 1) * HEAD_PAD] = krot
    vv = jnp.dot(kvn, wukv_ref[:, MLA_HEADS * QK_NOPE:], preferred_element_type=F32)
    lane = lax.broadcasted_iota(jnp.int32, (h_ref.shape[0], HEAD_PAD - V_HEAD), 1)
    unit = jnp.where(lane == 0, 1.0, 0.0).astype(BF16)
    for hd in range(MLA_HEADS):
        v_ref[:, hd * HEAD_PAD:hd * HEAD_PAD + V_HEAD] = (
            vv[:, hd * V_HEAD:(hd + 1) * V_HEAD].astype(BF16))
        v_ref[:, hd * HEAD_PAD + V_HEAD:(hd + 1) * HEAD_PAD] = unit
    if with_q:
        qn = _rms(p[:, :Q_LORA], gq_ref[...]).astype(BF16)
        tq = tq_ref[...]
        for hd in range(MLA_HEADS):
            qh = jnp.dot(qn, wuq_ref[:, hd * HEAD_PAD:(hd + 1) * HEAD_PAD],
                         preferred_element_type=F32)
            q_ref[:, hd * HEAD_PAD:(hd + 1) * HEAD_PAD] = (qh * tq).astype(BF16)


def _proj_attn(h, w_a, g_q, g_kv, w_uq, w_ukv, tab_q, tab_k, *, tm, with_q):
    rows = h.shape[0]
    na = w_a.shape[1]
    hp = MLA_HEADS * HEAD_PAD
    hv = MLA_HEADS * HEAD_PAD
    const = lambda i: (0, 0)
    row = lambda i: (i, 0)
    out_shape = [jax.ShapeDtypeStruct((rows, hp), BF16), jax.ShapeDtypeStruct((rows, hv), BF16)]
    out_specs = [pl.BlockSpec((tm, hp), row), pl.BlockSpec((tm, hv), row)]
    if with_q:
        out_shape = [jax.ShapeDtypeStruct((rows, hp), BF16)] + out_shape
        out_specs = [pl.BlockSpec((tm, hp), row)] + out_specs
    return pl.pallas_call(
        functools.partial(_proj_attn_kernel, with_q=with_q),
        grid=(rows // tm,),
        in_specs=[
            pl.BlockSpec((tm, D_MODEL), row),
            pl.BlockSpec((D_MODEL, na), const),
            pl.BlockSpec((1, Q_LORA), const),
            pl.BlockSpec((1, KV_LORA), const),
            pl.BlockSpec((Q_LORA, hp), const),
            pl.BlockSpec((KV_LORA, MLA_HEADS * (QK_NOPE + V_HEAD)), const),
            pl.BlockSpec((tm, HEAD_PAD), row),
            pl.BlockSpec((tm, 2 * QK_ROPE), row),
        ],
        out_specs=out_specs,
        out_shape=out_shape,
        compiler_params=_cparams(("arbitrary",)),
        name="proj_attn",
    )(h, w_a, g_q.reshape(1, Q_LORA), g_kv.reshape(1, KV_LORA), w_uq, w_ukv, tab_q, tab_k)


ATT_TQ = 512
ATT_KC = 1024
_NT = (((1,), (1,)), ((), ()))


def _attn_kernel(q_ref, kl_ref, kc_ref, vl_ref, vc_ref, o_ref):
    q = q_ref[...]
    s = lax.dot_general(q, kc_ref[...], _NT, preferred_element_type=F32)
    m = jnp.max(s, axis=-1, keepdims=True)
    p = jnp.exp2(s - m)
    acc = jnp.dot(p.astype(BF16), vc_ref[...], preferred_element_type=F32)
    for c in range(SEQ // ATT_KC):
        s = lax.dot_general(q, kl_ref[c * ATT_KC:(c + 1) * ATT_KC, :], _NT,
                            preferred_element_type=F32)
        m_new = jnp.maximum(m, jnp.max(s, axis=-1, keepdims=True))
        alpha = jnp.exp2(m - m_new)
        p = jnp.exp2(s - m_new)
        acc = acc * alpha + jnp.dot(p.astype(BF16), vl_ref[c * ATT_KC:(c + 1) * ATT_KC, :],
                                    preferred_element_type=F32)
        m = m_new
    o_ref[...] = (acc[:, :V_HEAD] / acc[:, V_HEAD:V_HEAD + 1]).astype(BF16)


def _attention(q, k_lat, k_ctx, v_lat, v_ctx):
    return pl.pallas_call(
        _attn_kernel,
        grid=(MLA_HEADS, SEQ // ATT_TQ),
        in_specs=[
            pl.BlockSpec((ATT_TQ, HEAD_PAD), lambda h, i: (i, h)),
            pl.BlockSpec((SEQ, HEAD_PAD), lambda h, i: (0, h)),
            pl.BlockSpec((CTX_LEN, HEAD_PAD), lambda h, i: (0, h)),
            pl.BlockSpec((SEQ, HEAD_PAD), lambda h, i: (0, h)),
            pl.BlockSpec((CTX_LEN, HEAD_PAD), lambda h, i: (0, h)),
        ],
        out_specs=pl.BlockSpec((ATT_TQ, V_HEAD), lambda h, i: (i, h)),
        out_shape=jax.ShapeDtypeStruct((SEQ, MLA_HEADS * V_HEAD), BF16),
        compiler_params=_cparams(("arbitrary", "arbitrary")),
        name="attention",
    )(q, k_lat, k_ctx, v_lat, v_ctx)


SC_HALO = 16


def _short_conv_kernel(x_ref, prev_ref, next_ref, w_ref, b_ref, o_ref):
    i = pl.program_id(0)
    tm = x_ref.shape[0]
    x = x_ref[...].astype(F32)
    prev_row = prev_ref[SC_HALO - 1:SC_HALO, :].astype(F32) * (i > 0).astype(F32)
    next_row = next_ref[0:1, :].astype(F32) * (i < pl.num_programs(0) - 1).astype(F32)
    rows = lax.broadcasted_iota(jnp.int32, x.shape, 0)
    up = jnp.where(rows == 0, prev_row, pltpu.roll(x, 1, 0))
    dn = jnp.where(rows == tm - 1, next_row, pltpu.roll(x, tm - 1, 0))
    o_ref[...] = up * w_ref[0:1, :] + x * w_ref[1:2, :] + dn * w_ref[2:3, :] + b_ref[...]


def _short_conv(hg, conv_w, conv_b, *, col0):
    tm, tc = 512, 1024
    nb = 3 * HY_WIDTH // tc
    cb0 = col0 // tc
    hb = tm // SC_HALO
    last = SEQ // SC_HALO - 1
    w8 = jnp.zeros((8, 3 * HY_WIDTH), F32).at[:3].set(conv_w)
    return pl.pallas_call(
        _short_conv_kernel,
        grid=(SEQ // tm, nb),
        in_specs=[
            pl.BlockSpec((tm, tc), lambda i, j: (i, j + cb0)),
            pl.BlockSpec((SC_HALO, tc), lambda i, j: (jnp.maximum(i * hb - 1, 0), j + cb0)),
            pl.BlockSpec((SC_HALO, tc), lambda i, j: (jnp.minimum((i + 1) * hb, last), j + cb0)),
            pl.BlockSpec((8, tc), lambda i, j: (0, j)),
            pl.BlockSpec((1, tc), lambda i, j: (0, j)),
        ],
        out_specs=pl.BlockSpec((tm, tc), lambda i, j: (i, j)),
        out_shape=jax.ShapeDtypeStruct((SEQ, 3 * HY_WIDTH), F32),
        compiler_params=_cparams(("arbitrary", "arbitrary")),
        name="short_conv",
    )(hg, hg, hg, w8, conv_b.reshape(1, -1))


HY_ZW = 128


def _filter_kernel(z_ref, w1_ref, b1_ref, w2_ref, b2_ref, w3_ref, fr_ref, dec_ref,
                   h_ref, ss_ref):
    hi = lax.Precision.HIGHEST
    z = z_ref[...]
    fr = fr_ref[...]
    h = jnp.sin(fr * (jnp.dot(z, w1_ref[...], precision=hi, preferred_element_type=F32)
                      + b1_ref[...]))
    h = jnp.sin(fr * (jnp.dot(h, w2_ref[...], precision=hi, preferred_element_type=F32)
                      + b2_ref[...]))
    h = jnp.dot(h, w3_ref[...], precision=hi, preferred_element_type=F32)
    dist = z[:, HY_EMB:HY_EMB + 1]
    h = h * (jnp.exp(-dist * dec_ref[...]) + HY_MOD_SHIFT)
    h_ref[...] = h

    @pl.when(pl.program_id(0) == 0)
    def _():
        ss_ref[...] = jnp.zeros_like(ss_ref)

    ss_ref[...] += jnp.sum(h * h, axis=0, keepdims=True)


def _hyena_filters(z_p, w1p, b1, w2, b2, w3, freq, decay2):
    tl = 512
    nw = HY_ORDER * HY_WIDTH
    const = lambda i: (0, 0)
    return pl.pallas_call(
        _filter_kernel,
        grid=(SEQ // tl,),
        in_specs=[
            pl.BlockSpec((tl, HY_ZW), lambda i: (i, 0)),
            pl.BlockSpec((HY_ZW, HY_FILTER_W), const),
            pl.BlockSpec((1, HY_FILTER_W), const),
            pl.BlockSpec((HY_FILTER_W, HY_FILTER_W), const),
            pl.BlockSpec((1, HY_FILTER_W), const),
            pl.BlockSpec((HY_FILTER_W, nw), const),
            pl.BlockSpec((1, HY_FILTER_W), const),
            pl.BlockSpec((1, nw), const),
        ],
        out_specs=[pl.BlockSpec((tl, nw), lambda i: (i, 0)), pl.BlockSpec((1, nw), const)],
        out_shape=[jax.ShapeDtypeStruct((SEQ, nw), F32), jax.ShapeDtypeStruct((1, nw), F32)],
        compiler_params=_cparams(("arbitrary",)),
        name="hyena_filters",
    )(z_p, w1p, b1.reshape(1, -1), w2, b2.reshape(1, -1), w3, freq.reshape(1, -1), decay2)


FFT_G = 16


def _fft_a_kernel(x_ref, m_ref, ss_ref, o_ref, *, normalise):
    x = x_ref[...]
    if normalise:
        x = x * lax.rsqrt(ss_ref[...] + NORM_EPS)
    o_ref[...] = jnp.einsum("gkn,gnc->gkc", m_ref[...], x.astype(BF16),
                            preferred_element_type=F32).astype(BF16)


def _fft_a(x_p, m_a, sumsq, *, col0, normalise):
    tc = HY_WIDTH
    cb = col0 // tc
    return pl.pallas_call(
        functools.partial(_fft_a_kernel, normalise=normalise),
        grid=(FFT_N2 // FFT_G,),
        in_specs=[
            pl.BlockSpec((FFT_G, FFT_HALF, tc), lambda g: (g, 0, cb)),
            pl.BlockSpec((FFT_G, 2 * FFT_N1, FFT_HALF), lambda g: (g, 0, 0)),
            pl.BlockSpec((1, tc), lambda g: (0, cb)),
        ],
        out_specs=pl.BlockSpec((FFT_G, 2 * FFT_N1, tc), lambda g: (g, 0, 0)),
        out_shape=jax.ShapeDtypeStruct((FFT_N2, 2 * FFT_N1, tc), BF16),
        compiler_params=_cparams(("arbitrary",)),
        name="fft_stage_a",
    )(x_p, m_a, sumsq)


def _fft_b_kernel(au_ref, ah_ref, g2_ref, g2i_ref, o_ref):
    g2 = g2_ref[...]
    g2i = g2i_ref[...]
    for g in range(au_ref.shape[0]):
        zu = jnp.dot(g2, au_ref[g], preferred_element_type=F32)
        zh = jnp.dot(g2, ah_ref[g], preferred_element_type=F32)
        ur, ui = zu[:FFT_N2], zu[FFT_N2:]
        hr, hi = zh[:FFT_N2], zh[FFT_N2:]
        pr = ur * hr - ui * hi
        pi = ur * hi + ui * hr
        p = jnp.concatenate([pr, pi], axis=0).astype(BF16)
        o_ref[g] = jnp.dot(g2i, p, preferred_element_type=F32).astype(BF16)


def _fft_b(au, ah, g2, g2i):
    gb, tc = 8, HY_WIDTH
    blk = pl.BlockSpec((gb, 2 * FFT_N2, tc), lambda g: (g, 0, 0))
    const = pl.BlockSpec((2 * FFT_N2, 2 * FFT_N2), lambda g: (0, 0))
    return pl.pallas_call(
        _fft_b_kernel,
        grid=(FFT_N1 // gb,),
        in_specs=[blk, blk, const, const],
        out_specs=blk,
        out_shape=jax.ShapeDtypeStruct((FFT_N1, 2 * FFT_N2, tc), BF16),
        compiler_params=_cparams(("arbitrary",)),
        name="fft_stage_b",
    )(au, ah, g2, g2i)


def _fft_c_kernel(b_ref, m_ref, u_ref, x_ref, skip_ref, o_ref):
    y = jnp.einsum("gmk,gkc->gmc", m_ref[...], b_ref[...], preferred_element_type=F32)
    o_ref[...] = x_ref[...] * (y + skip_ref[...] * u_ref[...])


def _fft_c(b_t, m_c, u_p, ucol, x_p, xcol, skip):
    tc = HY_WIDTH
    ub, xb = ucol // tc, xcol // tc
    return pl.pallas_call(
        _fft_c_kernel,
        grid=(FFT_N2 // FFT_G,),
        in_specs=[
            pl.BlockSpec((FFT_G, 2 * FFT_N1, tc), lambda g: (g, 0, 0)),
            pl.BlockSpec((FFT_G, FFT_HALF, 2 * FFT_N1), lambda g: (g, 0, 0)),
            pl.BlockSpec((FFT_G, FFT_HALF, tc), lambda g: (g, 0, ub)),
            pl.BlockSpec((FFT_G, FFT_HALF, tc), lambda g: (g, 0, xb)),
            pl.BlockSpec((1, 1, tc), lambda g: (0, 0, 0)),
        ],
        out_specs=pl.BlockSpec((FFT_G, FFT_HALF, tc), lambda g: (g, 0, 0)),
        out_shape=jax.ShapeDtypeStruct((FFT_N2, FFT_HALF, tc), F32),
        compiler_params=_cparams(("arbitrary",)),
        name="fft_stage_c",
    )(b_t, m_c, u_p, x_p, skip.reshape(1, 1, tc))


def _swap_major(a):
    n, m2, c = a.shape
    m = m2 // 2
    return a.reshape(n, 2, m, c).transpose(2, 1, 0, 3).reshape(m, 2 * n, c)


def _merge_kernel(a_ref, hy_ref, ga_ref, gh_ref, wa_ref, wh_ref, o_ref):
    ya = jnp.dot(a_ref[...], wa_ref[...], preferred_element_type=F32)
    yh = jnp.dot(hy_ref[...].astype(BF16), wh_ref[...], preferred_element_type=F32)
    o_ref[...] = (ga_ref[...].astype(F32) * ya + gh_ref[...].astype(F32) * yh).astype(BF16)


def _merge(attn, hy, hg, w_attn_o, w_hy_o):
    tm = 512
    row = lambda i: (i, 0)
    const = lambda i: (0, 0)
    return pl.pallas_call(
        _merge_kernel,
        grid=(SEQ // tm,),
        in_specs=[
            pl.BlockSpec((tm, MLA_HEADS * V_HEAD), row),
            pl.BlockSpec((tm, HY_WIDTH), row),
            pl.BlockSpec((tm, D_MODEL), lambda i: (i, 0)),
            pl.BlockSpec((tm, D_MODEL), lambda i: (i, 1)),
            pl.BlockSpec((MLA_HEADS * V_HEAD, D_MODEL), const),
            pl.BlockSpec((HY_WIDTH, D_MODEL), const),
        ],
        out_specs=pl.BlockSpec((tm, D_MODEL), row),
        out_shape=jax.ShapeDtypeStruct((SEQ, D_MODEL), BF16),
        compiler_params=_cparams(("arbitrary",)),
        name="merge_branches",
    )(attn, hy, hg, hg, w_attn_o, w_hy_o)


def _out_proj_kernel(y_ref, w_ref, s_ref, mod_ref, o_ref):
    xm = jnp.dot(y_ref[...], w_ref[...], preferred_element_type=F32)
    o_ref[...] = s_ref[...] + mod_ref[0:1, :] * xm


def _out_proj(y, w_out, s, mod_gate):
    tm = 512
    row = lambda i: (i, 0)
    const = lambda i: (0, 0)
    return pl.pallas_call(
        _out_proj_kernel,
        grid=(SEQ // tm,),
        in_specs=[
            pl.BlockSpec((tm, D_MODEL), row),
            pl.BlockSpec((D_MODEL, D_MODEL), const),
            pl.BlockSpec((tm, D_MODEL), row),
            pl.BlockSpec((8, D_MODEL), const),
        ],
        out_specs=pl.BlockSpec((tm, D_MODEL), row),
        out_shape=jax.ShapeDtypeStruct((SEQ, D_MODEL), F32),
        compiler_params=_cparams(("arbitrary",)),
        name="out_proj",
    )(y, w_out, s, mod_gate)


def _rope_tables():
    t = np.arange(SEQ)
    pos = np.stack([t // GRID_W, t % GRID_W], axis=1).astype(np.float64)
    inv_freq = ROPE_THETA ** (-np.arange(0, ROPE_AXIS, 2, dtype=np.float64) / ROPE_AXIS)
    i = np.arange(QK_ROPE)
    ang = pos[:, i // ROPE_AXIS] * inv_freq[i % (ROPE_AXIS // 2)][None, :]
    cos, sin = np.cos(ang), np.sin(ang)
    tab_q = np.concatenate([np.ones((SEQ, QK_NOPE)), cos, sin], axis=1) * (ATTN_SCALE * math.log2(math.e))
    tab_k = np.concatenate([cos, sin], axis=1)
    tab_k_ctx = np.concatenate([np.ones((CTX_LEN, QK_ROPE)), np.zeros((CTX_LEN, QK_ROPE))], axis=1)
    return (jnp.asarray(tab_q, F32), jnp.asarray(tab_k, F32), jnp.asarray(tab_k_ctx, F32))


def _rope_swap():
    i = np.arange(QK_ROPE)
    first_half = (i % ROPE_AXIS) < ROPE_AXIS // 2
    partner = np.where(first_half, i + ROPE_AXIS // 2, i - ROPE_AXIS // 2)
    sign = np.where(first_half, -1.0, 1.0)
    return partner, sign


def _n2_major(a):
    return a.reshape(FFT_HALF, FFT_N2, -1).transpose(1, 0, 2)


def _filter_features():
    pos = np.arange(SEQ, dtype=np.float64)[:, None]
    t01 = pos / (SEQ - 1)
    bands = np.linspace(1e-4, HY_BANDS - 1, HY_BANDS)[None, :]
    ang = bands * (2.0 * math.pi / SEQ) * pos
    dist = np.abs(pos - (SEQ // 2)) / (SEQ / 2.0)
    z = np.concatenate([t01, np.cos(ang), -np.sin(ang), dist], axis=1)
    z = np.pad(z, ((0, 0), (0, HY_ZW - z.shape[1])))
    return jnp.asarray(_n2_major(z).reshape(SEQ, HY_ZW), F32)


def _dft_tables():
    n1 = np.arange(FFT_HALF)
    k1 = np.arange(FFT_N1)
    n2 = np.arange(FFT_N2)
    n = FFT_N2 * n1[None, None, :] + n2[:, None, None]
    ph = (k1[None, :, None] * n) % FFT_N
    th = 2.0 * math.pi * ph / FFT_N
    m_a = np.concatenate([np.cos(th), -np.sin(th)], axis=1)
    ph2 = (n2[:, None] * n2[None, :]) % FFT_N2
    th2 = 2.0 * math.pi * ph2 / FFT_N2
    c, s = np.cos(th2), np.sin(th2)
    g2 = np.block([[c, s], [-s, c]])
    g2i = np.block([[c, -s], [s, c]])
    n = FFT_N2 * (n1[None, :, None] + FFT_HALF // 2) + n2[:, None, None]
    ph = (k1[None, None, :] * n) % FFT_N
    th = 2.0 * math.pi * ph / FFT_N
    m_c = np.concatenate([np.cos(th), -np.sin(th)], axis=2) / FFT_N
    return tuple(jnp.asarray(t, F32).astype(BF16) for t in (m_a, g2, g2i, m_c))


def _rows8(m, lo, hi):
    return jnp.zeros((8, D_MODEL), F32).at[:hi - lo].set(m[lo:hi])


def kernel(x, c, ctx, c_ctx, w_mod, b_mod, g_ffn1, w13_ffn1, w2_ffn1, g_mix, w_in, g_q, w_uq,
           g_kv, w_ukv, w_attn_o, hy_conv_w, hy_conv_b, hy_w1, hy_b1, hy_w2, hy_b2, hy_w3,
           hy_freq, hy_skip, w_hy_o, w_out, g_ffn2, w13_ffn2, w2_ffn2, g_final):
    xs = x[0]
    cs = ctx[0]
    li = 0

    c8 = jnp.zeros((8, D_MODEL), F32).at[0].set(c[0]).at[1].set(c_ctx)
    mod = _modulation(c8, w_mod[li], b_mod[li])
    mx = mod[0].reshape(N_MOD, D_MODEL)
    mc = mod[1].reshape(N_MOD, D_MODEL)

    w13 = w13_ffn1[li].astype(BF16)
    w2 = w2_ffn1[li].astype(BF16)
    x1 = _half_ffn(xs, _rows8(mx, 0, 3), g_ffn1[li], w13, w2, g_final, tm=512, final_norm=False)
    c1 = _half_ffn(cs, _rows8(mc, 0, 3), g_ffn1[li], w13, w2, g_final, tm=CTX_LEN,
                   final_norm=False)

    hx = _prenorm(x1, _rows8(mx, 3, 5), g_mix[li], tm=512)
    hc = _prenorm(c1, _rows8(mc, 3, 5), g_mix[li], tm=CTX_LEN)

    win = w_in[li]
    partner, sign = _rope_swap()
    w_kr = win[:, KV_END:KR_END]
    w_a = jnp.concatenate([win[:, :KV_END], w_kr, w_kr[:, partner] * sign], axis=1).astype(BF16)
    w_hg = jnp.concatenate([win[:, HY_END:], win[:, KR_END:HY_END]], axis=1).astype(BF16)
    wq = w_uq[li].reshape(Q_LORA, MLA_HEADS, QK_NOPE + QK_ROPE)
    wq_r = wq[:, :, QK_NOPE:]
    wq_p = jnp.concatenate([wq[:, :, :QK_NOPE], wq_r, wq_r[:, :, partner] * sign], axis=2)
    wq_p = wq_p.reshape(Q_LORA, MLA_HEADS * HEAD_PAD).astype(BF16)
    wkv = w_ukv[li].reshape(KV_LORA, MLA_HEADS, QK_NOPE + V_HEAD)
    wkv_p = jnp.concatenate([wkv[:, :, :QK_NOPE].reshape(KV_LORA, -1),
                             wkv[:, :, QK_NOPE:].reshape(KV_LORA, -1)], axis=1).astype(BF16)

    tab_q, tab_k, tab_k_ctx = _rope_tables()
    q, k_lat, v_lat = _proj_attn(hx, w_a, g_q[li], g_kv[li], wq_p, wkv_p, tab_q, tab_k,
                                 tm=512, with_q=True)
    k_ctx, v_ctx = _proj_attn(hc, w_a, g_q[li], g_kv[li], wq_p, wkv_p, tab_q[:CTX_LEN],
                              tab_k_ctx, tm=CTX_LEN, with_q=False)
    attn = _attention(q, k_lat, k_ctx, v_lat, v_ctx)

    hg = _proj_hg(hx, w_hg)
    u3 = _short_conv(hg, hy_conv_w[li], hy_conv_b[li], col0=2 * D_MODEL)
    u3p = _n2_major(u3)
    w1p = jnp.zeros((HY_ZW, HY_FILTER_W), F32).at[:HY_EMB].set(hy_w1[li])
    decay = np.abs(np.linspace(HY_MIN_DECAY, HY_MAX_DECAY, HY_WIDTH))
    decay2 = jnp.asarray(np.tile(decay, HY_ORDER)[None, :], F32)
    filt, sumsq = _hyena_filters(_filter_features(), w1p, hy_b1[li], hy_w2[li], hy_b2[li],
                                 hy_w3[li], hy_freq[li], decay2)
    filt_p = filt.reshape(FFT_N2, FFT_HALF, HY_ORDER * HY_WIDTH)
    m_a, g2, g2i, m_c = _dft_tables()
    ones = jnp.ones((1, HY_WIDTH), F32)

    def long_conv_gate(sig_p, sig_col, order, gate_p, gate_col):
        a_u = _fft_a(sig_p, m_a, ones, col0=sig_col, normalise=False)
        a_h = _fft_a(filt_p, m_a, sumsq, col0=order * HY_WIDTH, normalise=True)
        b = _fft_b(_swap_major(a_u), _swap_major(a_h), g2, g2i)
        return _fft_c(_swap_major(b), m_c, sig_p, sig_col, gate_p, gate_col, hy_skip[li][order])

    z_p = long_conv_gate(u3p, 0, 0, u3p, HY_WIDTH)
    hy_p = long_conv_gate(z_p, 0, 1, u3p, 2 * HY_WIDTH)
    hy = hy_p.transpose(1, 0, 2).reshape(SEQ, HY_WIDTH)

    y = _merge(attn, hy, hg, w_attn_o[li].astype(BF16), w_hy_o[li].astype(BF16))
    x2 = _out_proj(y, w_out[li].astype(BF16), x1, _rows8(mx, 5, 6))

    out = _half_ffn(x2, _rows8(mx, 6, 9), g_ffn2[li], w13_ffn2[li].astype(BF16),
                    w2_ffn2[li].astype(BF16), g_final, tm=512, final_norm=True)
    return out[None]
```

```python
import functools
import math

import numpy as np
import jax
import jax.numpy as jnp
from jax import lax
from jax.experimental import pallas as pl
from jax.experimental.pallas import tpu as pltpu

F32 = jnp.float32
BF16 = jnp.bfloat16

D_MODEL = 2048
SEQ = 8192
GRID_W = 64
CTX_LEN = 256
N_MOD = 9
D_FF = 5632
NORM_EPS = 1e-6

MLA_HEADS = 16
Q_LORA = 512
KV_LORA = 512
QK_NOPE = 128
QK_ROPE = 64
V_HEAD = 128
ROPE_AXIS = QK_ROPE // 2
ROPE_THETA = 10000.0
ATTN_SCALE = (QK_NOPE + QK_ROPE) ** -0.5
HEAD_PAD = 256

HY_WIDTH = 1024
HY_ORDER = 2
HY_EMB = 33
HY_BANDS = (HY_EMB - 1) // 2
HY_FILTER_W = 64
HY_TARGET = 1e-2
HY_FAST_DECAY = 0.3
HY_SLOW_DECAY = 1.5
HY_MAX_DECAY = math.log(HY_TARGET) / HY_FAST_DECAY
HY_MIN_DECAY = math.log(HY_TARGET) / HY_SLOW_DECAY
HY_MOD_SHIFT = 0.05

Q_END = Q_LORA
KV_END = Q_END + KV_LORA
KR_END = KV_END + QK_ROPE
HY_END = KR_END + 3 * HY_WIDTH

FFT_N = 2 * SEQ
FFT_N1 = 128
FFT_N2 = 128
FFT_HALF = SEQ // FFT_N2

V7X_VMEM_LIMIT = 60 * 1024 * 1024


def _cparams(sem, vmem=V7X_VMEM_LIMIT):
    return pltpu.CompilerParams(dimension_semantics=sem, vmem_limit_bytes=vmem)


def _sigmoid(x):
    return 1.0 / (1.0 + jnp.exp(-x))


def _rms(x, g):
    var = jnp.mean(x * x, axis=-1, keepdims=True)
    return x * lax.rsqrt(var + NORM_EPS) * g


def _mod_kernel(c_ref, w_ref, b_ref, o_ref):
    c = c_ref[...]
    a = c * _sigmoid(c)
    o_ref[...] = jnp.dot(a, w_ref[...], preferred_element_type=F32) + b_ref[...]


def _modulation(c8, w_mod, b_mod):
    n = w_mod.shape[1]
    tn = 1024
    return pl.pallas_call(
        _mod_kernel,
        grid=(n // tn,),
        in_specs=[
            pl.BlockSpec((8, D_MODEL), lambda j: (0, 0)),
            pl.BlockSpec((D_MODEL, tn), lambda j: (0, j)),
            pl.BlockSpec((1, tn), lambda j: (0, j)),
        ],
        out_specs=pl.BlockSpec((8, tn), lambda j: (0, j)),
        out_shape=jax.ShapeDtypeStruct((8, n), F32),
        compiler_params=_cparams(("arbitrary",)),
        name="modulation",
    )(c8, w_mod, b_mod.reshape(1, n))


def _ffn_kernel(s_ref, mod_ref, g_ref, w1_ref, w3_ref, w2_ref, gf_ref, o_ref, h_scr,
                *, final_norm):
    f = pl.program_id(1)

    @pl.when(f == 0)
    def _():
        h = _rms(s_ref[...], g_ref[...])
        h = h * (1.0 + mod_ref[1:2, :]) + mod_ref[0:1, :]
        h_scr[...] = h.astype(BF16)
        o_ref[...] = jnp.zeros_like(o_ref)

    h = h_scr[...]
    a = jnp.dot(h, w1_ref[...].astype(BF16), preferred_element_type=F32)
    b = jnp.dot(h, w3_ref[...].astype(BF16), preferred_element_type=F32)
    act = (a * _sigmoid(a) * b).astype(BF16)
    o_ref[...] += jnp.dot(act, w2_ref[...].astype(BF16), preferred_element_type=F32)

    @pl.when(f == pl.num_programs(1) - 1)
    def _():
        out = s_ref[...] + 0.5 * mod_ref[2:3, :] * o_ref[...]
        if final_norm:
            out = _rms(out, gf_ref[...])
        o_ref[...] = out


FFN_TM = 1024


def _half_ffn(s, mod3, g, w13, w2, g_final, *, tm, final_norm):
    rows = s.shape[0]
    tf = 256
    nf = D_FF // tf
    return pl.pallas_call(
        functools.partial(_ffn_kernel, final_norm=final_norm),
        grid=(rows // tm, nf),
        in_specs=[
            pl.BlockSpec((tm, D_MODEL), lambda i, f: (i, 0)),
            pl.BlockSpec((8, D_MODEL), lambda i, f: (0, 0)),
            pl.BlockSpec((1, D_MODEL), lambda i, f: (0, 0)),
            pl.BlockSpec((D_MODEL, tf), lambda i, f: (0, f)),
            pl.BlockSpec((D_MODEL, tf), lambda i, f: (0, f + nf)),
            pl.BlockSpec((tf, D_MODEL), lambda i, f: (f, 0)),
            pl.BlockSpec((1, D_MODEL), lambda i, f: (0, 0)),
        ],
        out_specs=pl.BlockSpec((tm, D_MODEL), lambda i, f: (i, 0)),
        out_shape=jax.ShapeDtypeStruct((rows, D_MODEL), F32),
        scratch_shapes=[pltpu.VMEM((tm, D_MODEL), BF16)],
        compiler_params=_cparams(("arbitrary", "arbitrary")),
        name="half_ffn",
    )(s, mod3, g.reshape(1, D_MODEL), w13, w13, w2, g_final.reshape(1, D_MODEL))


def _prenorm_kernel(s_ref, mod_ref, g_ref, o_ref):
    h = _rms(s_ref[...], g_ref[...])
    o_ref[...] = (h * (1.0 + mod_ref[1:2, :]) + mod_ref[0:1, :]).astype(BF16)


def _prenorm(s, mod3, g, *, tm):
    rows = s.shape[0]
    return pl.pallas_call(
        _prenorm_kernel,
        grid=(rows // tm,),
        in_specs=[
            pl.BlockSpec((tm, D_MODEL), lambda i: (i, 0)),
            pl.BlockSpec((8, D_MODEL), lambda i: (0, 0)),
            pl.BlockSpec((1, D_MODEL), lambda i: (0, 0)),
        ],
        out_specs=pl.BlockSpec((tm, D_MODEL), lambda i: (i, 0)),
        out_shape=jax.ShapeDtypeStruct((rows, D_MODEL), BF16),
        compiler_params=_cparams(("arbitrary",)),
        name="prenorm",
    )(s, mod3, g.reshape(1, D_MODEL))


def _proj_hg_kernel(h_ref, w_ref, o_ref, *, n_gate_tiles):
    j = pl.program_id(1)
    acc = jnp.dot(h_ref[...], w_ref[...], preferred_element_type=F32)

    @pl.when(j < n_gate_tiles)
    def _():
        o_ref[...] = _sigmoid(acc).astype(BF16)

    @pl.when(j >= n_gate_tiles)
    def _():
        o_ref[...] = acc.astype(BF16)


def _proj_hg(h, w_hg):
    rows, n = h.shape[0], w_hg.shape[1]
    tm, tn = 1024, 1024
    return pl.pallas_call(
        functools.partial(_proj_hg_kernel, n_gate_tiles=2 * D_MODEL // tn),
        grid=(rows // tm, n // tn),
        in_specs=[
            pl.BlockSpec((tm, D_MODEL), lambda i, j: (i, 0)),
            pl.BlockSpec((D_MODEL, tn), lambda i, j: (0, j)),
        ],
        out_specs=pl.BlockSpec((tm, tn), lambda i, j: (i, j)),
        out_shape=jax.ShapeDtypeStruct((rows, n), BF16),
        compiler_params=_cparams(("arbitrary", "arbitrary")),
        name="proj_gate_hyena",
    )(h, w_hg)


def _proj_attn_kernel(h_ref, wa_ref, gq_ref, gkv_ref, wuq_ref, wukv_ref, tq_ref, tk_ref,
                      *out_refs, with_q):
    if with_q:
        q_ref, k_ref, v_ref = out_refs
    else:
        k_ref, v_ref = out_refs
    p = jnp.dot(h_ref[...], wa_ref[...], preferred_element_type=F32)
    kvn = _rms(p[:, Q_LORA:Q_LORA + KV_LORA], gkv_ref[...]).astype(BF16)
    t = p[:, KV_END:KV_END + 2 * QK_ROPE] * tk_ref[...]
    krot = (t + pltpu.roll(t, QK_ROPE, 1)).astype(BF16)
    kn = jnp.dot(kvn, wukv_ref[:, :MLA_HEADS * QK_NOPE], preferred_element_type=F32)
    for hd in range(MLA_HEADS):
        k_ref[:, hd * HEAD_PAD:hd * HEAD_PAD + QK_NOPE] = (
            kn[:, hd * QK_NOPE:(hd + 1) * QK_NOPE].astype(BF16))
        k_ref[:, hd * HEAD_PAD + QK_NOPE:(hd + 1) * HEAD_PAD] = krot
    vv = jnp.dot(kvn, wukv_ref[:, MLA_HEADS * QK_NOPE:], preferred_element_type=F32)
    lane = lax.broadcasted_iota(jnp.int32, (h_ref.shape[0], HEAD_PAD - V_HEAD), 1)
    unit = jnp.where(lane == 0, 1.0, 0.0).astype(BF16)
    for hd in range(MLA_HEADS):
        v_ref[:, hd * HEAD_PAD:hd * HEAD_PAD + V_HEAD] = (
            vv[:, hd * V_HEAD:(hd + 1) * V_HEAD].astype(BF16))
        v_ref[:, hd * HEAD_PAD + V_HEAD:(hd + 1) * HEAD_PAD] = unit
    if with_q:
        qn = _rms(p[:, :Q_LORA], gq_ref[...]).astype(BF16)
        tq = tq_ref[...]
        for hd in range(MLA_HEADS):
            qh = jnp.dot(qn, wuq_ref[:, hd * HEAD_PAD:(hd + 1) * HEAD_PAD],
                         preferred_element_type=F32)
            q_ref[:, hd * HEAD_PAD:(hd + 1) * HEAD_PAD] = (qh * tq).astype(BF16)


def _proj_attn(h, w_a, g_q, g_kv, w_uq, w_ukv, tab_q, tab_k, *, tm, with_q):
    rows = h.shape[0]
    na = w_a.shape[1]
    hp = MLA_HEADS * HEAD_PAD
    hv = MLA_HEADS * HEAD_PAD
    const = lambda i: (0, 0)
    row = lambda i: (i, 0)
    out_shape = [jax.ShapeDtypeStruct((rows, hp), BF16), jax.ShapeDtypeStruct((rows, hv), BF16)]
    out_specs = [pl.BlockSpec((tm, hp), row), pl.BlockSpec((tm, hv), row)]
    if with_q:
        out_shape = [jax.ShapeDtypeStruct((rows, hp), BF16)] + out_shape
        out_specs = [pl.BlockSpec((tm, hp), row)] + out_specs
    return pl.pallas_call(
        functools.partial(_proj_attn_kernel, with_q=with_q),
        grid=(rows // tm,),
        in_specs=[
            pl.BlockSpec((tm, D_MODEL), row),
            pl.BlockSpec((D_MODEL, na), const),
            pl.BlockSpec((1, Q_LORA), const),
            pl.BlockSpec((1, KV_LORA), const),
            pl.BlockSpec((Q_LORA, hp), const),
            pl.BlockSpec((KV_LORA, MLA_HEADS * (QK_NOPE + V_HEAD)), const),
            pl.BlockSpec((tm, HEAD_PAD), row),
            pl.BlockSpec((tm, 2 * QK_ROPE), row),
        ],
        out_specs=out_specs,
        out_shape=out_shape,
        compiler_params=_cparams(("arbitrary",)),
        name="proj_attn",
    )(h, w_a, g_q.reshape(1, Q_LORA), g_kv.reshape(1, KV_LORA), w_uq, w_ukv, tab_q, tab_k)


ATT_TQ = 512
ATT_KC = 1024
_NT = (((1,), (1,)), ((), ()))


def _attn_kernel(q_ref, kl_ref, kc_ref, vl_ref, vc_ref, o_ref):
    q = q_ref[...]
    s = lax.dot_general(q, kc_ref[...], _NT, preferred_element_type=F32)
    m = jnp.max(s, axis=-1, keepdims=True)
    p = jnp.exp2(s - m)
    acc = jnp.dot(p.astype(BF16), vc_ref[...], preferred_element_type=F32)
    for c in range(SEQ // ATT_KC):
        s = lax.dot_general(q, kl_ref[c * ATT_KC:(c + 1) * ATT_KC, :], _NT,
                            preferred_element_type=F32)
        m_new = jnp.maximum(m, jnp.max(s, axis=-1, keepdims=True))
        alpha = jnp.exp2(m - m_new)
        p = jnp.exp2(s - m_new)
        acc = acc * alpha + jnp.dot(p.astype(BF16), vl_ref[c * ATT_KC:(c + 1) * ATT_KC, :],
                                    preferred_element_type=F32)
        m = m_new
    o_ref[...] = (acc[:, :V_HEAD] / acc[:, V_HEAD:V_HEAD + 1]).astype(BF16)


def _attention(q, k_lat, k_ctx, v_lat, v_ctx):
    return pl.pallas_call(
        _attn_kernel,
        grid=(MLA_HEADS, SEQ // ATT_TQ),
        in_specs=[
            pl.BlockSpec((ATT_TQ, HEAD_PAD), lambda h, i: (i, h)),
            pl.BlockSpec((SEQ, HEAD_PAD), lambda h, i: (0, h)),
            pl.BlockSpec((CTX_LEN, HEAD_PAD), lambda h, i: (0, h)),
            pl.BlockSpec((SEQ, HEAD_PAD), lambda h, i: (0, h)),
            pl.BlockSpec((CTX_LEN, HEAD_PAD), lambda h, i: (0, h)),
        ],
        out_specs=pl.BlockSpec((ATT_TQ, V_HEAD), lambda h, i: (i, h)),
        out_shape=jax.ShapeDtypeStruct((SEQ, MLA_HEADS * V_HEAD), BF16),
        compiler_params=_cparams(("arbitrary", "arbitrary")),
        name="attention",
    )(q, k_lat, k_ctx, v_lat, v_ctx)


SC_HALO = 16


def _short_conv_kernel(x_ref, prev_ref, next_ref, w_ref, b_ref, o_ref):
    i = pl.program_id(0)
    tm = x_ref.shape[0]
    x = x_ref[...].astype(F32)
    prev_row = prev_ref[SC_HALO - 1:SC_HALO, :].astype(F32) * (i > 0).astype(F32)
    next_row = next_ref[0:1, :].astype(F32) * (i < pl.num_programs(0) - 1).astype(F32)
    rows = lax.broadcasted_iota(jnp.int32, x.shape, 0)
    up = jnp.where(rows == 0, prev_row, pltpu.roll(x, 1, 0))
    dn = jnp.where(rows == tm - 1, next_row, pltpu.roll(x, tm - 1, 0))
    o_ref[...] = up * w_ref[0:1, :] + x * w_ref[1:2, :] + dn * w_ref[2:3, :] + b_ref[...]


def _short_conv(hg, conv_w, conv_b, *, col0):
    tm, tc = 512, 1024
    nb = 3 * HY_WIDTH // tc
    cb0 = col0 // tc
    hb = tm // SC_HALO
    last = SEQ // SC_HALO - 1
    w8 = jnp.zeros((8, 3 * HY_WIDTH), F32).at[:3].set(conv_w)
    return pl.pallas_call(
        _short_conv_kernel,
        grid=(SEQ // tm, nb),
        in_specs=[
            pl.BlockSpec((tm, tc), lambda i, j: (i, j + cb0)),
            pl.BlockSpec((SC_HALO, tc), lambda i, j: (jnp.maximum(i * hb - 1, 0), j + cb0)),
            pl.BlockSpec((SC_HALO, tc), lambda i, j: (jnp.minimum((i + 1) * hb, last), j + cb0)),
            pl.BlockSpec((8, tc), lambda i, j: (0, j)),
            pl.BlockSpec((1, tc), lambda i, j: (0, j)),
        ],
        out_specs=pl.BlockSpec((tm, tc), lambda i, j: (i, j)),
        out_shape=jax.ShapeDtypeStruct((SEQ, 3 * HY_WIDTH), F32),
        compiler_params=_cparams(("arbitrary", "arbitrary")),
        name="short_conv",
    )(hg, hg, hg, w8, conv_b.reshape(1, -1))


HY_ZW = 128


def _filter_kernel(z_ref, w1_ref, b1_ref, w2_ref, b2_ref, w3_ref, fr_ref, dec_ref,
                   h_ref, ss_ref):
    hi = lax.Precision.HIGHEST
    z = z_ref[...]
    fr = fr_ref[...]
    h = jnp.sin(fr * (jnp.dot(z, w1_ref[...], precision=hi, preferred_element_type=F32)
                      + b1_ref[...]))
    h = jnp.sin(fr * (jnp.dot(h, w2_ref[...], precision=hi, preferred_element_type=F32)
                      + b2_ref[...]))
    h = jnp.dot(h, w3_ref[...], precision=hi, preferred_element_type=F32)
    dist = z[:, HY_EMB:HY_EMB + 1]
    h = h * (jnp.exp(-dist * dec_ref[...]) + HY_MOD_SHIFT)
    h_ref[...] = h

    @pl.when(pl.program_id(0) == 0)
    def _():
        ss_ref[...] = jnp.zeros_like(ss_ref)

    ss_ref[...] += jnp.sum(h * h, axis=0, keepdims=True)


def _hyena_filters(z_p, w1p, b1, w2, b2, w3, freq, decay2):
    tl = 512
    nw = HY_ORDER * HY_WIDTH
    const = lambda i: (0, 0)
    return pl.pallas_call(
        _filter_kernel,
        grid=(SEQ // tl,),
        in_specs=[
            pl.BlockSpec((tl, HY_ZW), lambda i: (i, 0)),
            pl.BlockSpec((HY_ZW, HY_FILTER_W), const),
            pl.BlockSpec((1, HY_FILTER_W), const),
            pl.BlockSpec((HY_FILTER_W, HY_FILTER_W), const),
            pl.BlockSpec((1, HY_FILTER_W), const),
            pl.BlockSpec((HY_FILTER_W, nw), const),
            pl.BlockSpec((1, HY_FILTER_W), const),
            pl.BlockSpec((1, nw), const),
        ],
        out_specs=[pl.BlockSpec((tl, nw), lambda i: (i, 0)), pl.BlockSpec((1, nw), const)],
        out_shape=[jax.ShapeDtypeStruct((SEQ, nw), F32), jax.ShapeDtypeStruct((1, nw), F32)],
        compiler_params=_cparams(("arbitrary",)),
        name="hyena_filters",
    )(z_p, w1p, b1.reshape(1, -1), w2, b2.reshape(1, -1), w3, freq.reshape(1, -1), decay2)


FFT_G = 16


def _fft_a_kernel(x_ref, m_ref, ss_ref, o_ref, *, normalise):
    x = x_ref[...]
    if normalise:
        x = x * lax.rsqrt(ss_ref[...] + NORM_EPS)
    o_ref[...] = jnp.einsum("gkn,gnc->gkc", m_ref[...], x.astype(BF16),
                            preferred_element_type=F32).astype(BF16)


def _fft_a(x_p, m_a, sumsq, *, col0, normalise):
    tc = HY_WIDTH
    cb = col0 // tc
    return pl.pallas_call(
        functools.partial(_fft_a_kernel, normalise=normalise),
        grid=(FFT_N2 // FFT_G,),
        in_specs=[
            pl.BlockSpec((FFT_G, FFT_HALF, tc), lambda g: (g, 0, cb)),
            pl.BlockSpec((FFT_G, 2 * FFT_N1, FFT_HALF), lambda g: (g, 0, 0)),
            pl.BlockSpec((1, tc), lambda g: (0, cb)),
        ],
        out_specs=pl.BlockSpec((FFT_G, 2 * FFT_N1, tc), lambda g: (g, 0, 0)),
        out_shape=jax.ShapeDtypeStruct((FFT_N2, 2 * FFT_N1, tc), BF16),
        compiler_params=_cparams(("arbitrary",)),
        name="fft_stage_a",
    )(x_p, m_a, sumsq)


def _fft_b_kernel(au_ref, ah_ref, g2_ref, g2i_ref, o_ref):
    g2 = g2_ref[...]
    g2i = g2i_ref[...]
    for g in range(au_ref.shape[0]):
        zu = jnp.dot(g2, au_ref[g], preferred_element_type=F32)
        zh = jnp.dot(g2, ah_ref[g], preferred_element_type=F32)
        ur, ui = zu[:FFT_N2], zu[FFT_N2:]
        hr, hi = zh[:FFT_N2], zh[FFT_N2:]
        pr = ur * hr - ui * hi
        pi = ur * hi + ui * hr
        p = jnp.concatenate([pr, pi], axis=0).astype(BF16)
        o_ref[g] = jnp.dot(g2i, p, preferred_element_type=F32).astype(BF16)


def _fft_b(au, ah, g2, g2i):
    gb, tc = 8, HY_WIDTH
    blk = pl.BlockSpec((gb, 2 * FFT_N2, tc), lambda g: (g, 0, 0))
    const = pl.BlockSpec((2 * FFT_N2, 2 * FFT_N2), lambda g: (0, 0))
    return pl.pallas_call(
        _fft_b_kernel,
        grid=(FFT_N1 // gb,),
        in_specs=[blk, blk, const, const],
        out_specs=blk,
        out_shape=jax.ShapeDtypeStruct((FFT_N1, 2 * FFT_N2, tc), BF16),
        compiler_params=_cparams(("arbitrary",)),
        name="fft_stage_b",
    )(au, ah, g2, g2i)


def _fft_c_kernel(b_ref, m_ref, u_ref, x_ref, skip_ref, o_ref):
    y = jnp.einsum("gmk,gkc->gmc", m_ref[...], b_ref[...], preferred_element_type=F32)
    o_ref[...] = x_ref[...] * (y + skip_ref[...] * u_ref[...])


def _fft_c(b_t, m_c, u_p, ucol, x_p, xcol, skip):
    tc = HY_WIDTH
    ub, xb = ucol // tc, xcol // tc
    return pl.pallas_call(
        _fft_c_kernel,
        grid=(FFT_N2 // FFT_G,),
        in_specs=[
            pl.BlockSpec((FFT_G, 2 * FFT_N1, tc), lambda g: (g, 0, 0)),
            pl.BlockSpec((FFT_G, FFT_HALF, 2 * FFT_N1), lambda g: (g, 0, 0)),
            pl.BlockSpec((FFT_G, FFT_HALF, tc), lambda g: (g, 0, ub)),
            pl.BlockSpec((FFT_G, FFT_HALF, tc), lambda g: (g, 0, xb)),
            pl.BlockSpec((1, 1, tc), lambda g: (0, 0, 0)),
        ],
        out_specs=pl.BlockSpec((FFT_G, FFT_HALF, tc), lambda g: (g, 0, 0)),
        out_shape=jax.ShapeDtypeStruct((FFT_N2, FFT_HALF, tc), F32),
        compiler_params=_cparams(("arbitrary",)),
        name="fft_stage_c",
    )(b_t, m_c, u_p, x_p, skip.reshape(1, 1, tc))


def _swap_major(a):
    n, m2, c = a.shape
    m = m2 // 2
    return a.reshape(n, 2, m, c).transpose(2, 1, 0, 3).reshape(m, 2 * n, c)


def _merge_kernel(a_ref, hy_ref, ga_ref, gh_ref, wa_ref, wh_ref, o_ref):
    ya = jnp.dot(a_ref[...], wa_ref[...], preferred_element_type=F32)
    yh = jnp.dot(hy_ref[...].astype(BF16), wh_ref[...], preferred_element_type=F32)
    o_ref[...] = (ga_ref[...].astype(F32) * ya + gh_ref[...].astype(F32) * yh).astype(BF16)


def _merge(attn, hy, hg, w_attn_o, w_hy_o):
    tm = 512
    row = lambda i: (i, 0)
    const = lambda i: (0, 0)
    return pl.pallas_call(
        _merge_kernel,
        grid=(SEQ // tm,),
        in_specs=[
            pl.BlockSpec((tm, MLA_HEADS * V_HEAD), row),
            pl.BlockSpec((tm, HY_WIDTH), row),
            pl.BlockSpec((tm, D_MODEL), lambda i: (i, 0)),
            pl.BlockSpec((tm, D_MODEL), lambda i: (i, 1)),
            pl.BlockSpec((MLA_HEADS * V_HEAD, D_MODEL), const),
            pl.BlockSpec((HY_WIDTH, D_MODEL), const),
        ],
        out_specs=pl.BlockSpec((tm, D_MODEL), row),
        out_shape=jax.ShapeDtypeStruct((SEQ, D_MODEL), BF16),
        compiler_params=_cparams(("arbitrary",)),
        name="merge_branches",
    )(attn, hy, hg, hg, w_attn_o, w_hy_o)


def _out_proj_kernel(y_ref, w_ref, s_ref, mod_ref, o_ref):
    xm = jnp.dot(y_ref[...], w_ref[...], preferred_element_type=F32)
    o_ref[...] = s_ref[...] + mod_ref[0:1, :] * xm


def _out_proj(y, w_out, s, mod_gate):
    tm = 512
    row = lambda i: (i, 0)
    const = lambda i: (0, 0)
    return pl.pallas_call(
        _out_proj_kernel,
        grid=(SEQ // tm,),
        in_specs=[
            pl.BlockSpec((tm, D_MODEL), row),
            pl.BlockSpec((D_MODEL, D_MODEL), const),
            pl.BlockSpec((tm, D_MODEL), row),
            pl.BlockSpec((8, D_MODEL), const),
        ],
        out_specs=pl.BlockSpec((tm, D_MODEL), row),
        out_shape=jax.ShapeDtypeStruct((SEQ, D_MODEL), F32),
        compiler_params=_cparams(("arbitrary",)),
        name="out_proj",
    )(y, w_out, s, mod_gate)


def _rope_tables():
    t = np.arange(SEQ)
    pos = np.stack([t // GRID_W, t % GRID_W], axis=1).astype(np.float64)
    inv_freq = ROPE_THETA ** (-np.arange(0, ROPE_AXIS, 2, dtype=np.float64) / ROPE_AXIS)
    i = np.arange(QK_ROPE)
    ang = pos[:, i // ROPE_AXIS] * inv_freq[i % (ROPE_AXIS // 2)][None, :]
    cos, sin = np.cos(ang), np.sin(ang)
    tab_q = np.concatenate([np.ones((SEQ, QK_NOPE)), cos, sin], axis=1) * (ATTN_SCALE * math.log2(math.e))
    tab_k = np.concatenate([cos, sin], axis=1)
    tab_k_ctx = np.concatenate([np.ones((CTX_LEN, QK_ROPE)), np.zeros((CTX_LEN, QK_ROPE))], axis=1)
    return (jnp.asarray(tab_q, F32), jnp.asarray(tab_k, F32), jnp.asarray(tab_k_ctx, F32))


def _rope_swap():
    i = np.arange(QK_ROPE)
    first_half = (i % ROPE_AXIS) < ROPE_AXIS // 2
    partner = np.where(first_half, i + ROPE_AXIS // 2, i - ROPE_AXIS // 2)
    sign = np.where(first_half, -1.0, 1.0)
    return partner, sign


def _n2_major(a):
    return a.reshape(FFT_HALF, FFT_N2, -1).transpose(1, 0, 2)


def _filter_features():
    pos = np.arange(SEQ, dtype=np.float64)[:, None]
    t01 = pos / (SEQ - 1)
    bands = np.linspace(1e-4, HY_BANDS - 1, HY_BANDS)[None, :]
    ang = bands * (2.0 * math.pi / SEQ) * pos
    dist = np.abs(pos - (SEQ // 2)) / (SEQ / 2.0)
    z = np.concatenate([t01, np.cos(ang), -np.sin(ang), dist], axis=1)
    z = np.pad(z, ((0, 0), (0, HY_ZW - z.shape[1])))
    return jnp.asarray(_n2_major(z).reshape(SEQ, HY_ZW), F32)


def _dft_tables():
    n1 = np.arange(FFT_HALF)
    k1 = np.arange(FFT_N1)
    n2 = np.arange(FFT_N2)
    n = FFT_N2 * n1[None, None, :] + n2[:, None, None]
    ph = (k1[None, :, None] * n) % FFT_N
    th = 2.0 * math.pi * ph / FFT_N
    m_a = np.concatenate([np.cos(th), -np.sin(th)], axis=1)
    ph2 = (n2[:, None] * n2[None, :]) % FFT_N2
    th2 = 2.0 * math.pi * ph2 / FFT_N2
    c, s = np.cos(th2), np.sin(th2)
    g2 = np.block([[c, s], [-s, c]])
    g2i = np.block([[c, -s], [s, c]])
    n = FFT_N2 * (n1[None, :, None] + FFT_HALF // 2) + n2[:, None, None]
    ph = (k1[None, None, :] * n) % FFT_N
    th = 2.0 * math.pi * ph / FFT_N
    m_c = np.concatenate([np.cos(th), -np.sin(th)], axis=2) / FFT_N
    return tuple(jnp.asarray(t, F32).astype(BF16) for t in (m_a, g2, g2i, m_c))


def _rows8(m, lo, hi):
    return jnp.zeros((8, D_MODEL), F32).at[:hi - lo].set(m[lo:hi])


def kernel(x, c, ctx, c_ctx, w_mod, b_mod, g_ffn1, w13_ffn1, w2_ffn1, g_mix, w_in, g_q, w_uq,
           g_kv, w_ukv, w_attn_o, hy_conv_w, hy_conv_b, hy_w1, hy_b1, hy_w2, hy_b2, hy_w3,
           hy_freq, hy_skip, w_hy_o, w_out, g_ffn2, w13_ffn2, w2_ffn2, g_final):
    xs = x[0]
    cs = ctx[0]
    li = 0

    c8 = jnp.zeros((8, D_MODEL), F32).at[0].set(c[0]).at[1].set(c_ctx)
    mod = _modulation(c8, w_mod[li], b_mod[li])
    mx = mod[0].reshape(N_MOD, D_MODEL)
    mc = mod[1].reshape(N_MOD, D_MODEL)

    w13 = w13_ffn1[li]
    w2 = w2_ffn1[li]
    x1 = _half_ffn(xs, _rows8(mx, 0, 3), g_ffn1[li], w13, w2, g_final, tm=FFN_TM, final_norm=False)
    c1 = _half_ffn(cs, _rows8(mc, 0, 3), g_ffn1[li], w13, w2, g_final, tm=CTX_LEN,
                   final_norm=False)

    hx = _prenorm(x1, _rows8(mx, 3, 5), g_mix[li], tm=512)
    hc = _prenorm(c1, _rows8(mc, 3, 5), g_mix[li], tm=CTX_LEN)

    win = w_in[li]
    partner, sign = _rope_swap()
    w_kr = win[:, KV_END:KR_END]
    w_a = jnp.concatenate([win[:, :KV_END], w_kr, w_kr[:, partner] * sign], axis=1).astype(BF16)
    w_hg = jnp.concatenate([win[:, HY_END:], win[:, KR_END:HY_END]], axis=1).astype(BF16)
    wq = w_uq[li].reshape(Q_LORA, MLA_HEADS, QK_NOPE + QK_ROPE)
    wq_r = wq[:, :, QK_NOPE:]
    wq_p = jnp.concatenate([wq[:, :, :QK_NOPE], wq_r, wq_r[:, :, partner] * sign], axis=2)
    wq_p = wq_p.reshape(Q_LORA, MLA_HEADS * HEAD_PAD).astype(BF16)
    wkv = w_ukv[li].reshape(KV_LORA, MLA_HEADS, QK_NOPE + V_HEAD)
    wkv_p = jnp.concatenate([wkv[:, :, :QK_NOPE].reshape(KV_LORA, -1),
                             wkv[:, :, QK_NOPE:].reshape(KV_LORA, -1)], axis=1).astype(BF16)

    tab_q, tab_k, tab_k_ctx = _rope_tables()
    q, k_lat, v_lat = _proj_attn(hx, w_a, g_q[li], g_kv[li], wq_p, wkv_p, tab_q, tab_k,
                                 tm=512, with_q=True)
    k_ctx, v_ctx = _proj_attn(hc, w_a, g_q[li], g_kv[li], wq_p, wkv_p, tab_q[:CTX_LEN],
                              tab_k_ctx, tm=CTX_LEN, with_q=False)
    attn = _attention(q, k_lat, k_ctx, v_lat, v_ctx)

    hg = _proj_hg(hx, w_hg)
    u3 = _short_conv(hg, hy_conv_w[li], hy_conv_b[li], col0=2 * D_MODEL)
    u3p = _n2_major(u3)
    w1p = jnp.zeros((HY_ZW, HY_FILTER_W), F32).at[:HY_EMB].set(hy_w1[li])
    decay = np.abs(np.linspace(HY_MIN_DECAY, HY_MAX_DECAY, HY_WIDTH))
    decay2 = jnp.asarray(np.tile(decay, HY_ORDER)[None, :], F32)
    filt, sumsq = _hyena_filters(_filter_features(), w1p, hy_b1[li], hy_w2[li], hy_b2[li],
                                 hy_w3[li], hy_freq[li], decay2)
    filt_p = filt.reshape(FFT_N2, FFT_HALF, HY_ORDER * HY_WIDTH)
    m_a, g2, g2i, m_c = _dft_tables()
    ones = jnp.ones((1, HY_WIDTH), F32)

    def long_conv_gate(sig_p, sig_col, order, gate_p, gate_col):
        a_u = _fft_a(sig_p, m_a, ones, col0=sig_col, normalise=False)
        a_h = _fft_a(filt_p, m_a, sumsq, col0=order * HY_WIDTH, normalise=True)
        b = _fft_b(_swap_major(a_u), _swap_major(a_h), g2, g2i)
        return _fft_c(_swap_major(b), m_c, sig_p, sig_col, gate_p, gate_col, hy_skip[li][order])

    z_p = long_conv_gate(u3p, 0, 0, u3p, HY_WIDTH)
    hy_p = long_conv_gate(z_p, 0, 1, u3p, 2 * HY_WIDTH)
    hy = hy_p.transpose(1, 0, 2).reshape(SEQ, HY_WIDTH)

    y = _merge(attn, hy, hg, w_attn_o[li].astype(BF16), w_hy_o[li].astype(BF16))
    x2 = _out_proj(y, w_out[li].astype(BF16), x1, _rows8(mx, 5, 6))

    out = _half_ffn(x2, _rows8(mx, 6, 9), g_ffn2[li], w13_ffn2[li], w2_ffn2[li], g_final,
                    tm=FFN_TM, final_norm=True)
    return out[None]
```

```python
import functools
import math

import numpy as np
import jax
import jax.numpy as jnp
from jax import lax
from jax.experimental import pallas as pl
from jax.experimental.pallas import tpu as pltpu

F32 = jnp.float32
BF16 = jnp.bfloat16

D_MODEL = 2048
SEQ = 8192
GRID_W = 64
CTX_LEN = 256
N_MOD = 9
D_FF = 5632
NORM_EPS = 1e-6

MLA_HEADS = 16
Q_LORA = 512
KV_LORA = 512
QK_NOPE = 128
QK_ROPE = 64
V_HEAD = 128
ROPE_AXIS = QK_ROPE // 2
ROPE_THETA = 10000.0
ATTN_SCALE = (QK_NOPE + QK_ROPE) ** -0.5
HEAD_PAD = 256

HY_WIDTH = 1024
HY_ORDER = 2
HY_EMB = 33
HY_BANDS = (HY_EMB - 1) // 2
HY_FILTER_W = 64
HY_TARGET = 1e-2
HY_FAST_DECAY = 0.3
HY_SLOW_DECAY = 1.5
HY_MAX_DECAY = math.log(HY_TARGET) / HY_FAST_DECAY
HY_MIN_DECAY = math.log(HY_TARGET) / HY_SLOW_DECAY
HY_MOD_SHIFT = 0.05

Q_END = Q_LORA
KV_END = Q_END + KV_LORA
KR_END = KV_END + QK_ROPE
HY_END = KR_END + 3 * HY_WIDTH

FFT_N = 2 * SEQ
FFT_N1 = 128
FFT_N2 = 128
FFT_HALF = SEQ // FFT_N2

V7X_VMEM_LIMIT = 60 * 1024 * 1024


def _cparams(sem, vmem=V7X_VMEM_LIMIT):
    return pltpu.CompilerParams(dimension_semantics=sem, vmem_limit_bytes=vmem)


def _sigmoid(x):
    return 1.0 / (1.0 + jnp.exp(-x))


def _rms(x, g):
    var = jnp.mean(x * x, axis=-1, keepdims=True)
    return x * lax.rsqrt(var + NORM_EPS) * g


def _mod_kernel(c_ref, w_ref, b_ref, o_ref):
    c = c_ref[...]
    a = c * _sigmoid(c)
    o_ref[...] = jnp.dot(a, w_ref[...], preferred_element_type=F32) + b_ref[...]


def _modulation(c8, w_mod, b_mod):
    n = w_mod.shape[1]
    tn = 1024
    return pl.pallas_call(
        _mod_kernel,
        grid=(n // tn,),
        in_specs=[
            pl.BlockSpec((8, D_MODEL), lambda j: (0, 0)),
            pl.BlockSpec((D_MODEL, tn), lambda j: (0, j)),
            pl.BlockSpec((1, tn), lambda j: (0, j)),
        ],
        out_specs=pl.BlockSpec((8, tn), lambda j: (0, j)),
        out_shape=jax.ShapeDtypeStruct((8, n), F32),
        compiler_params=_cparams(("arbitrary",)),
        name="modulation",
    )(c8, w_mod, b_mod.reshape(1, n))


def _ffn_kernel(s_ref, mod_ref, g_ref, w1_ref, w3_ref, w2_ref, gf_ref, o_ref, h_scr,
                *, final_norm):
    f = pl.program_id(1)

    @pl.when(f == 0)
    def _():
        h = _rms(s_ref[...], g_ref[...])
        h = h * (1.0 + mod_ref[1:2, :]) + mod_ref[0:1, :]
        h_scr[...] = h.astype(BF16)
        o_ref[...] = jnp.zeros_like(o_ref)

    h = h_scr[...]
    a = jnp.dot(h, w1_ref[...].astype(BF16), preferred_element_type=F32)
    b = jnp.dot(h, w3_ref[...].astype(BF16), preferred_element_type=F32)
    act = (a * _sigmoid(a) * b).astype(BF16)
    o_ref[...] += jnp.dot(act, w2_ref[...].astype(BF16), preferred_element_type=F32)

    @pl.when(f == pl.num_programs(1) - 1)
    def _():
        out = s_ref[...] + 0.5 * mod_ref[2:3, :] * o_ref[...]
        if final_norm:
            out = _rms(out, gf_ref[...])
        o_ref[...] = out


FFN_TM = 1024


def _half_ffn(s, mod3, g, w13, w2, g_final, *, tm, final_norm):
    rows = s.shape[0]
    tf = 256
    nf = D_FF // tf
    return pl.pallas_call(
        functools.partial(_ffn_kernel, final_norm=final_norm),
        grid=(rows // tm, nf),
        in_specs=[
            pl.BlockSpec((tm, D_MODEL), lambda i, f: (i, 0)),
            pl.BlockSpec((8, D_MODEL), lambda i, f: (0, 0)),
            pl.BlockSpec((1, D_MODEL), lambda i, f: (0, 0)),
            pl.BlockSpec((D_MODEL, tf), lambda i, f: (0, f)),
            pl.BlockSpec((D_MODEL, tf), lambda i, f: (0, f + nf)),
            pl.BlockSpec((tf, D_MODEL), lambda i, f: (f, 0)),
            pl.BlockSpec((1, D_MODEL), lambda i, f: (0, 0)),
        ],
        out_specs=pl.BlockSpec((tm, D_MODEL), lambda i, f: (i, 0)),
        out_shape=jax.ShapeDtypeStruct((rows, D_MODEL), F32),
        scratch_shapes=[pltpu.VMEM((tm, D_MODEL), BF16)],
        compiler_params=_cparams(("arbitrary", "arbitrary")),
        name="half_ffn",
    )(s, mod3, g.reshape(1, D_MODEL), w13, w13, w2, g_final.reshape(1, D_MODEL))


def _prenorm_kernel(s_ref, mod_ref, g_ref, o_ref):
    h = _rms(s_ref[...], g_ref[...])
    o_ref[...] = (h * (1.0 + mod_ref[1:2, :]) + mod_ref[0:1, :]).astype(BF16)


def _prenorm(s, mod3, g, *, tm):
    rows = s.shape[0]
    return pl.pallas_call(
        _prenorm_kernel,
        grid=(rows // tm,),
        in_specs=[
            pl.BlockSpec((tm, D_MODEL), lambda i: (i, 0)),
            pl.BlockSpec((8, D_MODEL), lambda i: (0, 0)),
            pl.BlockSpec((1, D_MODEL), lambda i: (0, 0)),
        ],
        out_specs=pl.BlockSpec((tm, D_MODEL), lambda i: (i, 0)),
        out_shape=jax.ShapeDtypeStruct((rows, D_MODEL), BF16),
        compiler_params=_cparams(("arbitrary",)),
        name="prenorm",
    )(s, mod3, g.reshape(1, D_MODEL))


def _proj_hg_kernel(h_ref, w_ref, o_ref, *, n_gate_tiles):
    j = pl.program_id(1)
    acc = jnp.dot(h_ref[...], w_ref[...], preferred_element_type=F32)

    @pl.when(j < n_gate_tiles)
    def _():
        o_ref[...] = _sigmoid(acc).astype(BF16)

    @pl.when(j >= n_gate_tiles)
    def _():
        o_ref[...] = acc.astype(BF16)


def _proj_hg(h, w_hg):
    rows, n = h.shape[0], w_hg.shape[1]
    tm, tn = 1024, 1024
    return pl.pallas_call(
        functools.partial(_proj_hg_kernel, n_gate_tiles=2 * D_MODEL // tn),
        grid=(rows // tm, n // tn),
        in_specs=[
            pl.BlockSpec((tm, D_MODEL), lambda i, j: (i, 0)),
            pl.BlockSpec((D_MODEL, tn), lambda i, j: (0, j)),
        ],
        out_specs=pl.BlockSpec((tm, tn), lambda i, j: (i, j)),
        out_shape=jax.ShapeDtypeStruct((rows, n), BF16),
        compiler_params=_cparams(("arbitrary", "arbitrary")),
        name="proj_gate_hyena",
    )(h, w_hg)


def _proj_attn_kernel(h_ref, wa_ref, gq_ref, gkv_ref, wuq_ref, wukv_ref, tq_ref, tk_ref,
                      *out_refs, with_q):
    if with_q:
        q_ref, k_ref, v_ref = out_refs
    else:
        k_ref, v_ref = out_refs
    p = jnp.dot(h_ref[...], wa_ref[...], preferred_element_type=F32)
    kvn = _rms(p[:, Q_LORA:Q_LORA + KV_LORA], gkv_ref[...]).astype(BF16)
    t = p[:, KV_END:KV_END + 2 * QK_ROPE] * tk_ref[...]
    krot = (t + pltpu.roll(t, QK_ROPE, 1)).astype(BF16)
    kn = jnp.dot(kvn, wukv_ref[:, :MLA_HEADS * QK_NOPE], preferred_element_type=F32)
    for hd in range(MLA_HEADS):
        k_ref[:, hd * HEAD_PAD:hd * HEAD_PAD + QK_NOPE] = (
            kn[:, hd * QK_NOPE:(hd + 1) * QK_NOPE].astype(BF16))
        k_ref[:, hd * HEAD_PAD + QK_NOPE:(hd + 1) * HEAD_PAD] = krot
    vv = jnp.dot(kvn, wukv_ref[:, MLA_HEADS * QK_NOPE:], preferred_element_type=F32)
    lane = lax.broadcasted_iota(jnp.int32, (h_ref.shape[0], HEAD_PAD - V_HEAD), 1)
    unit = jnp.where(lane == 0, 1.0, 0.0).astype(BF16)
    for hd in range(MLA_HEADS):
        v_ref[:, hd * HEAD_PAD:hd * HEAD_PAD + V_HEAD] = (
            vv[:, hd * V_HEAD:(hd + 1) * V_HEAD].astype(BF16))
        v_ref[:, hd * HEAD_PAD + V_HEAD:(hd + 1) * HEAD_PAD] = unit
    if with_q:
        qn = _rms(p[:, :Q_LORA], gq_ref[...]).astype(BF16)
        tq = tq_ref[...]
        for hd in range(MLA_HEADS):
            qh = jnp.dot(qn, wuq_ref[:, hd * HEAD_PAD:(hd + 1) * HEAD_PAD],
                         preferred_element_type=F32)
            q_ref[:, hd * HEAD_PAD:(hd + 1) * HEAD_PAD] = (qh * tq).astype(BF16)


def _proj_attn(h, w_a, g_q, g_kv, w_uq, w_ukv, tab_q, tab_k, *, tm, with_q):
    rows = h.shape[0]
    na = w_a.shape[1]
    hp = MLA_HEADS * HEAD_PAD
    hv = MLA_HEADS * HEAD_PAD
    const = lambda i: (0, 0)
    row = lambda i: (i, 0)
    out_shape = [jax.ShapeDtypeStruct((rows, hp), BF16), jax.ShapeDtypeStruct((rows, hv), BF16)]
    out_specs = [pl.BlockSpec((tm, hp), row), pl.BlockSpec((tm, hv), row)]
    if with_q:
        out_shape = [jax.ShapeDtypeStruct((rows, hp), BF16)] + out_shape
        out_specs = [pl.BlockSpec((tm, hp), row)] + out_specs
    return pl.pallas_call(
        functools.partial(_proj_attn_kernel, with_q=with_q),
        grid=(rows // tm,),
        in_specs=[
            pl.BlockSpec((tm, D_MODEL), row),
            pl.BlockSpec((D_MODEL, na), const),
            pl.BlockSpec((1, Q_LORA), const),
            pl.BlockSpec((1, KV_LORA), const),
            pl.BlockSpec((Q_LORA, hp), const),
            pl.BlockSpec((KV_LORA, MLA_HEADS * (QK_NOPE + V_HEAD)), const),
            pl.BlockSpec((tm, HEAD_PAD), row),
            pl.BlockSpec((tm, 2 * QK_ROPE), row),
        ],
        out_specs=out_specs,
        out_shape=out_shape,
        compiler_params=_cparams(("arbitrary",)),
        name="proj_attn",
    )(h, w_a, g_q.reshape(1, Q_LORA), g_kv.reshape(1, KV_LORA), w_uq, w_ukv, tab_q, tab_k)


ATT_TQ = 1024
ATT_SUB = 512
ATT_KC = 1024
_NT = (((1,), (1,)), ((), ()))


def _attn_kernel(q_ref, kl_ref, kc_ref, vl_ref, vc_ref, o_ref):
    n_sub = ATT_TQ // ATT_SUB
    qs, ms, accs = [], [], []
    for r in range(n_sub):
        q = q_ref[r * ATT_SUB:(r + 1) * ATT_SUB, :]
        s = lax.dot_general(q, kc_ref[...], _NT, preferred_element_type=F32)
        m = jnp.max(s, axis=-1, keepdims=True)
        p = jnp.exp2(s - m)
        qs.append(q)
        ms.append(m)
        accs.append(jnp.dot(p.astype(BF16), vc_ref[...], preferred_element_type=F32))
    for c in range(SEQ // ATT_KC):
        for r in range(n_sub):
            s = lax.dot_general(qs[r], kl_ref[c * ATT_KC:(c + 1) * ATT_KC, :], _NT,
                                preferred_element_type=F32)
            m_new = jnp.maximum(ms[r], jnp.max(s, axis=-1, keepdims=True))
            alpha = jnp.exp2(ms[r] - m_new)
            p = jnp.exp2(s - m_new)
            accs[r] = accs[r] * alpha + jnp.dot(
                p.astype(BF16), vl_ref[c * ATT_KC:(c + 1) * ATT_KC, :],
                preferred_element_type=F32)
            ms[r] = m_new
    for r in range(n_sub):
        acc = accs[r]
        o_ref[r * ATT_SUB:(r + 1) * ATT_SUB, :] = (
            acc[:, :V_HEAD] / acc[:, V_HEAD:V_HEAD + 1]).astype(BF16)


def _attention(q, k_lat, k_ctx, v_lat, v_ctx):
    return pl.pallas_call(
        _attn_kernel,
        grid=(MLA_HEADS, SEQ // ATT_TQ),
        in_specs=[
            pl.BlockSpec((ATT_TQ, HEAD_PAD), lambda h, i: (i, h)),
            pl.BlockSpec((SEQ, HEAD_PAD), lambda h, i: (0, h)),
            pl.BlockSpec((CTX_LEN, HEAD_PAD), lambda h, i: (0, h)),
            pl.BlockSpec((SEQ, HEAD_PAD), lambda h, i: (0, h)),
            pl.BlockSpec((CTX_LEN, HEAD_PAD), lambda h, i: (0, h)),
        ],
        out_specs=pl.BlockSpec((ATT_TQ, V_HEAD), lambda h, i: (i, h)),
        out_shape=jax.ShapeDtypeStruct((SEQ, MLA_HEADS * V_HEAD), BF16),
        compiler_params=_cparams(("arbitrary", "arbitrary")),
        name="attention",
    )(q, k_lat, k_ctx, v_lat, v_ctx)


SC_HALO = 16


def _short_conv_kernel(x_ref, prev_ref, next_ref, w_ref, b_ref, o_ref):
    i = pl.program_id(0)
    tm = x_ref.shape[0]
    x = x_ref[...].astype(F32)
    prev_row = prev_ref[SC_HALO - 1:SC_HALO, :].astype(F32) * (i > 0).astype(F32)
    next_row = next_ref[0:1, :].astype(F32) * (i < pl.num_programs(0) - 1).astype(F32)
    rows = lax.broadcasted_iota(jnp.int32, x.shape, 0)
    up = jnp.where(rows == 0, prev_row, pltpu.roll(x, 1, 0))
    dn = jnp.where(rows == tm - 1, next_row, pltpu.roll(x, tm - 1, 0))
    o_ref[...] = up * w_ref[0:1, :] + x * w_ref[1:2, :] + dn * w_ref[2:3, :] + b_ref[...]


def _short_conv(hg, conv_w, conv_b, *, col0):
    tm, tc = 512, 1024
    nb = 3 * HY_WIDTH // tc
    cb0 = col0 // tc
    hb = tm // SC_HALO
    last = SEQ // SC_HALO - 1
    w8 = jnp.zeros((8, 3 * HY_WIDTH), F32).at[:3].set(conv_w)
    return pl.pallas_call(
        _short_conv_kernel,
        grid=(SEQ // tm, nb),
        in_specs=[
            pl.BlockSpec((tm, tc), lambda i, j: (i, j + cb0)),
            pl.BlockSpec((SC_HALO, tc), lambda i, j: (jnp.maximum(i * hb - 1, 0), j + cb0)),
            pl.BlockSpec((SC_HALO, tc), lambda i, j: (jnp.minimum((i + 1) * hb, last), j + cb0)),
            pl.BlockSpec((8, tc), lambda i, j: (0, j)),
            pl.BlockSpec((1, tc), lambda i, j: (0, j)),
        ],
        out_specs=pl.BlockSpec((tm, tc), lambda i, j: (i, j)),
        out_shape=jax.ShapeDtypeStruct((SEQ, 3 * HY_WIDTH), F32),
        compiler_params=_cparams(("arbitrary", "arbitrary")),
        name="short_conv",
    )(hg, hg, hg, w8, conv_b.reshape(1, -1))


HY_ZW = 128


def _filter_kernel(z_ref, w1_ref, b1_ref, w2_ref, b2_ref, w3_ref, fr_ref, dec_ref,
                   h_ref, ss_ref):
    hi = lax.Precision.HIGHEST
    z = z_ref[...]
    fr = fr_ref[...]
    h = jnp.sin(fr * (jnp.dot(z, w1_ref[...], precision=hi, preferred_element_type=F32)
                      + b1_ref[...]))
    h = jnp.sin(fr * (jnp.dot(h, w2_ref[...], precision=hi, preferred_element_type=F32)
                      + b2_ref[...]))
    h = jnp.dot(h, w3_ref[...], precision=hi, preferred_element_type=F32)
    dist = z[:, HY_EMB:HY_EMB + 1]
    h = h * (jnp.exp(-dist * dec_ref[...]) + HY_MOD_SHIFT)
    h_ref[...] = h

    @pl.when(pl.program_id(0) == 0)
    def _():
        ss_ref[...] = jnp.zeros_like(ss_ref)

    ss_ref[...] += jnp.sum(h * h, axis=0, keepdims=True)


def _hyena_filters(z_p, w1p, b1, w2, b2, w3, freq, decay2):
    tl = 512
    nw = HY_ORDER * HY_WIDTH
    const = lambda i: (0, 0)
    return pl.pallas_call(
        _filter_kernel,
        grid=(SEQ // tl,),
        in_specs=[
            pl.BlockSpec((tl, HY_ZW), lambda i: (i, 0)),
            pl.BlockSpec((HY_ZW, HY_FILTER_W), const),
            pl.BlockSpec((1, HY_FILTER_W), const),
            pl.BlockSpec((HY_FILTER_W, HY_FILTER_W), const),
            pl.BlockSpec((1, HY_FILTER_W), const),
            pl.BlockSpec((HY_FILTER_W, nw), const),
            pl.BlockSpec((1, HY_FILTER_W), const),
            pl.BlockSpec((1, nw), const),
        ],
        out_specs=[pl.BlockSpec((tl, nw), lambda i: (i, 0)), pl.BlockSpec((1, nw), const)],
        out_shape=[jax.ShapeDtypeStruct((SEQ, nw), F32), jax.ShapeDtypeStruct((1, nw), F32)],
        compiler_params=_cparams(("arbitrary",)),
        name="hyena_filters",
    )(z_p, w1p, b1.reshape(1, -1), w2, b2.reshape(1, -1), w3, freq.reshape(1, -1), decay2)


FFT_LANES = 128
FFT_G = 32
FFT_STEPS = FFT_N2 // FFT_G
FFT_ROWS = 2 * FFT_N1
Y_PITCH = FFT_ROWS + 8
T_PITCH = FFT_N2 + 8


def _pad_rows_in(src_ref, dst_scr, scale=None):
    for n1 in range(FFT_HALF):
        v = src_ref[n1 * FFT_N2:(n1 + 1) * FFT_N2, :]
        dst_scr[n1 * T_PITCH:n1 * T_PITCH + FFT_N2, :] = v if scale is None else v * scale


def _fft_stage_a(t, src_scr, ma_ref, y_scr):
    for g in range(FFT_G):
        j = t * FFT_G + g
        xj = src_scr[pl.ds(j, FFT_HALF, stride=T_PITCH), :]
        a = jnp.dot(ma_ref[g], xj.astype(BF16), preferred_element_type=F32)
        y_scr[pl.ds(pl.multiple_of(j * Y_PITCH, 8), FFT_ROWS), :] = a


def _fft_load_pair(tb, g, y_scr):
    cols = []
    for k1 in (tb * FFT_G + g, tb * FFT_G + g + 1):
        re = y_scr[pl.ds(k1, FFT_N2, stride=Y_PITCH), :]
        im = y_scr[pl.ds(FFT_N1 + k1, FFT_N2, stride=Y_PITCH), :]
        cols.append(jnp.concatenate([re, im], axis=0))
    return jnp.concatenate(cols, axis=1).astype(BF16)


def _hyena_spectrum_kernel(h_ref, ss_ref, ma_ref, g2_ref, o_ref, y_scr, h_scr):
    t = pl.program_id(1)

    @pl.when(t == 0)
    def _():
        _pad_rows_in(h_ref, h_scr, lax.rsqrt(ss_ref[...] + NORM_EPS))

    @pl.when(t < FFT_STEPS)
    def _():
        _fft_stage_a(t, h_scr, ma_ref, y_scr)

    @pl.when(t >= FFT_STEPS)
    def _():
        tb = t - FFT_STEPS
        for g in range(0, FFT_G, 2):
            z = jnp.dot(g2_ref[...], _fft_load_pair(tb, g, y_scr), preferred_element_type=F32)
            o_ref[g] = z[:, :FFT_LANES].astype(BF16)
            o_ref[g + 1] = z[:, FFT_LANES:].astype(BF16)


def _hyena_spectrum(filt, sumsq, m_a, g2):
    nb = filt.shape[1] // FFT_LANES
    last = FFT_STEPS - 1
    return pl.pallas_call(
        _hyena_spectrum_kernel,
        grid=(nb, 2 * FFT_STEPS),
        in_specs=[
            pl.BlockSpec((SEQ, FFT_LANES), lambda c, t: (0, c)),
            pl.BlockSpec((1, FFT_LANES), lambda c, t: (0, c)),
            pl.BlockSpec((FFT_G, FFT_ROWS, FFT_HALF), lambda c, t: (jnp.minimum(t, last), 0, 0)),
            pl.BlockSpec((FFT_ROWS, FFT_ROWS), lambda c, t: (0, 0)),
        ],
        out_specs=pl.BlockSpec((None, FFT_G, FFT_ROWS, FFT_LANES),
                               lambda c, t: (c, jnp.maximum(t - FFT_STEPS, 0), 0, 0)),
        out_shape=jax.ShapeDtypeStruct((nb, FFT_N1, FFT_ROWS, FFT_LANES), BF16),
        scratch_shapes=[pltpu.VMEM((FFT_N2 * Y_PITCH, FFT_LANES), F32),
                        pltpu.VMEM((FFT_HALF * T_PITCH, FFT_LANES), F32)],
        compiler_params=_cparams(("arbitrary", "arbitrary")),
        name="hyena_spectrum",
    )(filt, sumsq, m_a, g2)


def _long_conv_kernel(u_ref, x_ref, skip_ref, ma_ref, hf_ref, g2_ref, g2i_ref, mc_ref,
                      o_ref, y_scr, stage_scr, u_scr, x_scr):
    t = pl.program_id(1)

    @pl.when(t == 0)
    def _():
        _pad_rows_in(u_ref, u_scr)
        _pad_rows_in(x_ref, x_scr)

    @pl.when(t < FFT_STEPS)
    def _():
        _fft_stage_a(t, u_scr, ma_ref, y_scr)

    @pl.when((t >= FFT_STEPS) & (t < 2 * FFT_STEPS))
    def _():
        tb = t - FFT_STEPS
        for g in range(0, FFT_G, 2):
            stage_scr[g // 2] = _fft_load_pair(tb, g, y_scr)
        for g in range(0, FFT_G, 2):
            z = jnp.dot(g2_ref[...], stage_scr[g // 2], preferred_element_type=F32)
            hf = jnp.concatenate([hf_ref[g], hf_ref[g + 1]], axis=1).astype(F32)
            zr, zi = z[:FFT_N2], z[FFT_N2:]
            hr, hi = hf[:FFT_N2], hf[FFT_N2:]
            p = jnp.concatenate([zr * hr - zi * hi, zr * hi + zi * hr], axis=0).astype(BF16)
            b = jnp.dot(g2i_ref[...], p, preferred_element_type=F32)
            for half, k1 in enumerate((tb * FFT_G + g, tb * FFT_G + g + 1)):
                lanes = slice(half * FFT_LANES, (half + 1) * FFT_LANES)
                y_scr[pl.ds(k1, FFT_N2, stride=Y_PITCH), :] = b[:FFT_N2, lanes]
                y_scr[pl.ds(FFT_N1 + k1, FFT_N2, stride=Y_PITCH), :] = b[FFT_N2:, lanes]

    @pl.when(t >= 2 * FFT_STEPS)
    def _():
        tc = t - 2 * FFT_STEPS
        for g in range(FFT_G):
            m2 = tc * FFT_G + g
            bm = y_scr[pl.ds(pl.multiple_of(m2 * Y_PITCH, 8), FFT_ROWS), :]
            y = jnp.dot(mc_ref[g], bm.astype(BF16), preferred_element_type=F32)
            rows = pl.ds(m2, FFT_HALF, stride=T_PITCH)
            u_scr[rows, :] = x_scr[rows, :] * (y + skip_ref[...] * u_scr[rows, :])

    @pl.when(t == 3 * FFT_STEPS - 1)
    def _():
        for n1 in range(FFT_HALF):
            o_ref[n1 * FFT_N2:(n1 + 1) * FFT_N2, :] = u_scr[n1 * T_PITCH:n1 * T_PITCH + FFT_N2, :]


def _long_conv_gate(u, ucol, x, xcol, skip, hf, hf0, m_a, g2, g2i, m_c):
    nb = HY_WIDTH // FFT_LANES
    ub, xb = ucol // FFT_LANES, xcol // FFT_LANES
    last = FFT_STEPS - 1
    step = lambda t, phase: jnp.clip(t - phase * FFT_STEPS, 0, last)
    return pl.pallas_call(
        _long_conv_kernel,
        grid=(nb, 3 * FFT_STEPS),
        in_specs=[
            pl.BlockSpec((SEQ, FFT_LANES), lambda c, t: (0, c + ub)),
            pl.BlockSpec((SEQ, FFT_LANES), lambda c, t: (0, c + xb)),
            pl.BlockSpec((1, FFT_LANES), lambda c, t: (0, c)),
            pl.BlockSpec((FFT_G, FFT_ROWS, FFT_HALF), lambda c, t: (step(t, 0), 0, 0)),
            pl.BlockSpec((None, FFT_G, FFT_ROWS, FFT_LANES),
                         lambda c, t: (c + hf0, step(t, 1), 0, 0)),
            pl.BlockSpec((FFT_ROWS, FFT_ROWS), lambda c, t: (0, 0)),
            pl.BlockSpec((FFT_ROWS, FFT_ROWS), lambda c, t: (0, 0)),
            pl.BlockSpec((FFT_G, FFT_HALF, FFT_ROWS), lambda c, t: (step(t, 2), 0, 0)),
        ],
        out_specs=pl.BlockSpec((SEQ, FFT_LANES), lambda c, t: (0, c)),
        out_shape=jax.ShapeDtypeStruct((SEQ, HY_WIDTH), F32),
        scratch_shapes=[pltpu.VMEM((FFT_N2 * Y_PITCH, FFT_LANES), F32),
                        pltpu.VMEM((FFT_G // 2, FFT_ROWS, 2 * FFT_LANES), BF16),
                        pltpu.VMEM((FFT_HALF * T_PITCH, FFT_LANES), F32),
                        pltpu.VMEM((FFT_HALF * T_PITCH, FFT_LANES), F32)],
        compiler_params=_cparams(("arbitrary", "arbitrary")),
        name="long_conv_gate",
    )(u, x, skip.reshape(1, HY_WIDTH), m_a, hf, g2, g2i, m_c)


def _merge_kernel(a_ref, hy_ref, ga_ref, gh_ref, wa_ref, wh_ref, o_ref):
    ya = jnp.dot(a_ref[...], wa_ref[...], preferred_element_type=F32)
    yh = jnp.dot(hy_ref[...].astype(BF16), wh_ref[...], preferred_element_type=F32)
    o_ref[...] = (ga_ref[...].astype(F32) * ya + gh_ref[...].astype(F32) * yh).astype(BF16)


def _merge(attn, hy, hg, w_attn_o, w_hy_o):
    tm = 512
    row = lambda i: (i, 0)
    const = lambda i: (0, 0)
    return pl.pallas_call(
        _merge_kernel,
        grid=(SEQ // tm,),
        in_specs=[
            pl.BlockSpec((tm, MLA_HEADS * V_HEAD), row),
            pl.BlockSpec((tm, HY_WIDTH), row),
            pl.BlockSpec((tm, D_MODEL), lambda i: (i, 0)),
            pl.BlockSpec((tm, D_MODEL), lambda i: (i, 1)),
            pl.BlockSpec((MLA_HEADS * V_HEAD, D_MODEL), const),
            pl.BlockSpec((HY_WIDTH, D_MODEL), const),
        ],
        out_specs=pl.BlockSpec((tm, D_MODEL), row),
        out_shape=jax.ShapeDtypeStruct((SEQ, D_MODEL), BF16),
        compiler_params=_cparams(("arbitrary",)),
        name="merge_branches",
    )(attn, hy, hg, hg, w_attn_o, w_hy_o)


def _out_proj_kernel(y_ref, w_ref, s_ref, mod_ref, o_ref):
    xm = jnp.dot(y_ref[...], w_ref[...], preferred_element_type=F32)
    o_ref[...] = s_ref[...] + mod_ref[0:1, :] * xm


def _out_proj(y, w_out, s, mod_gate):
    tm = 512
    row = lambda i: (i, 0)
    const = lambda i: (0, 0)
    return pl.pallas_call(
        _out_proj_kernel,
        grid=(SEQ // tm,),
        in_specs=[
            pl.BlockSpec((tm, D_MODEL), row),
            pl.BlockSpec((D_MODEL, D_MODEL), const),
            pl.BlockSpec((tm, D_MODEL), row),
            pl.BlockSpec((8, D_MODEL), const),
        ],
        out_specs=pl.BlockSpec((tm, D_MODEL), row),
        out_shape=jax.ShapeDtypeStruct((SEQ, D_MODEL), F32),
        compiler_params=_cparams(("arbitrary",)),
        name="out_proj",
    )(y, w_out, s, mod_gate)


def _rope_tables():
    t = np.arange(SEQ)
    pos = np.stack([t // GRID_W, t % GRID_W], axis=1).astype(np.float64)
    inv_freq = ROPE_THETA ** (-np.arange(0, ROPE_AXIS, 2, dtype=np.float64) / ROPE_AXIS)
    i = np.arange(QK_ROPE)
    ang = pos[:, i // ROPE_AXIS] * inv_freq[i % (ROPE_AXIS // 2)][None, :]
    cos, sin = np.cos(ang), np.sin(ang)
    tab_q = np.concatenate([np.ones((SEQ, QK_NOPE)), cos, sin], axis=1) * (ATTN_SCALE * math.log2(math.e))
    tab_k = np.concatenate([cos, sin], axis=1)
    tab_k_ctx = np.concatenate([np.ones((CTX_LEN, QK_ROPE)), np.zeros((CTX_LEN, QK_ROPE))], axis=1)
    return (jnp.asarray(tab_q, F32), jnp.asarray(tab_k, F32), jnp.asarray(tab_k_ctx, F32))


def _rope_swap():
    i = np.arange(QK_ROPE)
    first_half = (i % ROPE_AXIS) < ROPE_AXIS // 2
    partner = np.where(first_half, i + ROPE_AXIS // 2, i - ROPE_AXIS // 2)
    sign = np.where(first_half, -1.0, 1.0)
    return partner, sign


def _filter_features():
    pos = np.arange(SEQ, dtype=np.float64)[:, None]
    t01 = pos / (SEQ - 1)
    bands = np.linspace(1e-4, HY_BANDS - 1, HY_BANDS)[None, :]
    ang = bands * (2.0 * math.pi / SEQ) * pos
    dist = np.abs(pos - (SEQ // 2)) / (SEQ / 2.0)
    z = np.concatenate([t01, np.cos(ang), -np.sin(ang), dist], axis=1)
    z = np.pad(z, ((0, 0), (0, HY_ZW - z.shape[1])))
    return jnp.asarray(z, F32)


def _dft_tables():
    n1 = np.arange(FFT_HALF)
    k1 = np.arange(FFT_N1)
    n2 = np.arange(FFT_N2)
    n = FFT_N2 * n1[None, None, :] + n2[:, None, None]
    ph = (k1[None, :, None] * n) % FFT_N
    th = 2.0 * math.pi * ph / FFT_N
    m_a = np.concatenate([np.cos(th), -np.sin(th)], axis=1)
    ph2 = (n2[:, None] * n2[None, :]) % FFT_N2
    th2 = 2.0 * math.pi * ph2 / FFT_N2
    c, s = np.cos(th2), np.sin(th2)
    g2 = np.block([[c, s], [-s, c]])
    g2i = np.block([[c, -s], [s, c]])
    n = FFT_N2 * (n1[None, :, None] + FFT_HALF // 2) + n2[:, None, None]
    ph = (k1[None, None, :] * n) % FFT_N
    th = 2.0 * math.pi * ph / FFT_N
    m_c = np.concatenate([np.cos(th), -np.sin(th)], axis=2) / FFT_N
    return tuple(jnp.asarray(t, F32).astype(BF16) for t in (m_a, g2, g2i, m_c))


def _rows8(m, lo, hi):
    return jnp.zeros((8, D_MODEL), F32).at[:hi - lo].set(m[lo:hi])


def kernel(x, c, ctx, c_ctx, w_mod, b_mod, g_ffn1, w13_ffn1, w2_ffn1, g_mix, w_in, g_q, w_uq,
           g_kv, w_ukv, w_attn_o, hy_conv_w, hy_conv_b, hy_w1, hy_b1, hy_w2, hy_b2, hy_w3,
           hy_freq, hy_skip, w_hy_o, w_out, g_ffn2, w13_ffn2, w2_ffn2, g_final):
    xs = x[0]
    cs = ctx[0]
    li = 0

    c8 = jnp.zeros((8, D_MODEL), F32).at[0].set(c[0]).at[1].set(c_ctx)
    mod = _modulation(c8, w_mod[li], b_mod[li])
    mx = mod[0].reshape(N_MOD, D_MODEL)
    mc = mod[1].reshape(N_MOD, D_MODEL)

    w13 = w13_ffn1[li]
    w2 = w2_ffn1[li]
    x1 = _half_ffn(xs, _rows8(mx, 0, 3), g_ffn1[li], w13, w2, g_final, tm=FFN_TM, final_norm=False)
    c1 = _half_ffn(cs, _rows8(mc, 0, 3), g_ffn1[li], w13, w2, g_final, tm=CTX_LEN,
                   final_norm=False)

    hx = _prenorm(x1, _rows8(mx, 3, 5), g_mix[li], tm=512)
    hc = _prenorm(c1, _rows8(mc, 3, 5), g_mix[li], tm=CTX_LEN)

    win = w_in[li]
    partner, sign = _rope_swap()
    w_kr = win[:, KV_END:KR_END]
    w_a = jnp.concatenate([win[:, :KV_END], w_kr, w_kr[:, partner] * sign], axis=1).astype(BF16)
    w_hg = jnp.concatenate([win[:, HY_END:], win[:, KR_END:HY_END]], axis=1).astype(BF16)
    wq = w_uq[li].reshape(Q_LORA, MLA_HEADS, QK_NOPE + QK_ROPE)
    wq_r = wq[:, :, QK_NOPE:]
    wq_p = jnp.concatenate([wq[:, :, :QK_NOPE], wq_r, wq_r[:, :, partner] * sign], axis=2)
    wq_p = wq_p.reshape(Q_LORA, MLA_HEADS * HEAD_PAD).astype(BF16)
    wkv = w_ukv[li].reshape(KV_LORA, MLA_HEADS, QK_NOPE + V_HEAD)
    wkv_p = jnp.concatenate([wkv[:, :, :QK_NOPE].reshape(KV_LORA, -1),
                             wkv[:, :, QK_NOPE:].reshape(KV_LORA, -1)], axis=1).astype(BF16)

    tab_q, tab_k, tab_k_ctx = _rope_tables()
    q, k_lat, v_lat = _proj_attn(hx, w_a, g_q[li], g_kv[li], wq_p, wkv_p, tab_q, tab_k,
                                 tm=512, with_q=True)
    k_ctx, v_ctx = _proj_attn(hc, w_a, g_q[li], g_kv[li], wq_p, wkv_p, tab_q[:CTX_LEN],
                              tab_k_ctx, tm=CTX_LEN, with_q=False)
    attn = _attention(q, k_lat, k_ctx, v_lat, v_ctx)

    hg = _proj_hg(hx, w_hg)
    u3 = _short_conv(hg, hy_conv_w[li], hy_conv_b[li], col0=2 * D_MODEL)
    w1p = jnp.zeros((HY_ZW, HY_FILTER_W), F32).at[:HY_EMB].set(hy_w1[li])
    decay = np.abs(np.linspace(HY_MIN_DECAY, HY_MAX_DECAY, HY_WIDTH))
    decay2 = jnp.asarray(np.tile(decay, HY_ORDER)[None, :], F32)
    filt, sumsq = _hyena_filters(_filter_features(), w1p, hy_b1[li], hy_w2[li], hy_b2[li],
                                 hy_w3[li], hy_freq[li], decay2)
    m_a, g2, g2i, m_c = _dft_tables()
    hf = _hyena_spectrum(filt, sumsq, m_a, g2)
    nb = HY_WIDTH // FFT_LANES
    z = _long_conv_gate(u3, 0, u3, HY_WIDTH, hy_skip[li][0], hf, 0, m_a, g2, g2i, m_c)
    hy = _long_conv_gate(z, 0, u3, 2 * HY_WIDTH, hy_skip[li][1], hf, nb, m_a, g2, g2i, m_c)

    y = _merge(attn, hy, hg, w_attn_o[li].astype(BF16), w_hy_o[li].astype(BF16))
    x2 = _out_proj(y, w_out[li].astype(BF16), x1, _rows8(mx, 5, 6))

    out = _half_ffn(x2, _rows8(mx, 6, 9), g_ffn2[li], w13_ffn2[li], w2_ffn2[li], g_final,
                    tm=FFN_TM, final_norm=True)
    return out[None]
```

```python
import functools
import math

import numpy as np
import jax
import jax.numpy as jnp
from jax import lax
from jax.experimental import pallas as pl
from jax.experimental.pallas import tpu as pltpu

F32 = jnp.float32
BF16 = jnp.bfloat16

D_MODEL = 2048
SEQ = 8192
GRID_W = 64
CTX_LEN = 256
N_MOD = 9
D_FF = 5632
NORM_EPS = 1e-6

MLA_HEADS = 16
Q_LORA = 512
KV_LORA = 512
QK_NOPE = 128
QK_ROPE = 64
V_HEAD = 128
ROPE_AXIS = QK_ROPE // 2
ROPE_THETA = 10000.0
ATTN_SCALE = (QK_NOPE + QK_ROPE) ** -0.5
HEAD_PAD = 256

HY_WIDTH = 1024
HY_ORDER = 2
HY_EMB = 33
HY_BANDS = (HY_EMB - 1) // 2
HY_FILTER_W = 64
HY_TARGET = 1e-2
HY_FAST_DECAY = 0.3
HY_SLOW_DECAY = 1.5
HY_MAX_DECAY = math.log(HY_TARGET) / HY_FAST_DECAY
HY_MIN_DECAY = math.log(HY_TARGET) / HY_SLOW_DECAY
HY_MOD_SHIFT = 0.05

Q_END = Q_LORA
KV_END = Q_END + KV_LORA
KR_END = KV_END + QK_ROPE
HY_END = KR_END + 3 * HY_WIDTH

FFT_N = 2 * SEQ
FFT_N1 = 128
FFT_N2 = 128
FFT_HALF = SEQ // FFT_N2

V7X_VMEM_LIMIT = 60 * 1024 * 1024


def _cparams(sem, vmem=V7X_VMEM_LIMIT):
    return pltpu.CompilerParams(dimension_semantics=sem, vmem_limit_bytes=vmem)


def _sigmoid(x):
    return 1.0 / (1.0 + jnp.exp(-x))


def _rms(x, g):
    var = jnp.mean(x * x, axis=-1, keepdims=True)
    return x * lax.rsqrt(var + NORM_EPS) * g


def _mod_kernel(c_ref, w_ref, b_ref, o_ref):
    c = c_ref[...]
    a = c * _sigmoid(c)
    o_ref[...] = jnp.dot(a, w_ref[...], preferred_element_type=F32) + b_ref[...]


def _modulation(c8, w_mod, b_mod):
    n = w_mod.shape[1]
    tn = 1024
    return pl.pallas_call(
        _mod_kernel,
        grid=(n // tn,),
        in_specs=[
            pl.BlockSpec((8, D_MODEL), lambda j: (0, 0)),
            pl.BlockSpec((D_MODEL, tn), lambda j: (0, j)),
            pl.BlockSpec((1, tn), lambda j: (0, j)),
        ],
        out_specs=pl.BlockSpec((8, tn), lambda j: (0, j)),
        out_shape=jax.ShapeDtypeStruct((8, n), F32),
        compiler_params=_cparams(("arbitrary",)),
        name="modulation",
    )(c8, w_mod, b_mod.reshape(1, n))


def _ffn_kernel(s_ref, mod_ref, g_ref, w1_ref, w3_ref, w2_ref, gf_ref, o_ref, h_scr,
                *, final_norm):
    f = pl.program_id(1)

    @pl.when(f == 0)
    def _():
        h = _rms(s_ref[...], g_ref[...])
        h = h * (1.0 + mod_ref[1:2, :]) + mod_ref[0:1, :]
        h_scr[...] = h.astype(BF16)
        o_ref[...] = jnp.zeros_like(o_ref)

    h = h_scr[...]
    a = jnp.dot(h, w1_ref[...].astype(BF16), preferred_element_type=F32)
    b = jnp.dot(h, w3_ref[...].astype(BF16), preferred_element_type=F32)
    act = (a * _sigmoid(a) * b).astype(BF16)
    o_ref[...] += jnp.dot(act, w2_ref[...].astype(BF16), preferred_element_type=F32)

    @pl.when(f == pl.num_programs(1) - 1)
    def _():
        out = s_ref[...] + 0.5 * mod_ref[2:3, :] * o_ref[...]
        if final_norm:
            out = _rms(out, gf_ref[...])
        o_ref[...] = out


FFN_TM = 1024


def _half_ffn(s, mod3, g, w13, w2, g_final, *, tm, final_norm):
    rows = s.shape[0]
    tf = 256
    nf = D_FF // tf
    return pl.pallas_call(
        functools.partial(_ffn_kernel, final_norm=final_norm),
        grid=(rows // tm, nf),
        in_specs=[
            pl.BlockSpec((tm, D_MODEL), lambda i, f: (i, 0)),
            pl.BlockSpec((8, D_MODEL), lambda i, f: (0, 0)),
            pl.BlockSpec((1, D_MODEL), lambda i, f: (0, 0)),
            pl.BlockSpec((D_MODEL, tf), lambda i, f: (0, f)),
            pl.BlockSpec((D_MODEL, tf), lambda i, f: (0, f + nf)),
            pl.BlockSpec((tf, D_MODEL), lambda i, f: (f, 0)),
            pl.BlockSpec((1, D_MODEL), lambda i, f: (0, 0)),
        ],
        out_specs=pl.BlockSpec((tm, D_MODEL), lambda i, f: (i, 0)),
        out_shape=jax.ShapeDtypeStruct((rows, D_MODEL), F32),
        scratch_shapes=[pltpu.VMEM((tm, D_MODEL), BF16)],
        compiler_params=_cparams(("arbitrary", "arbitrary")),
        name="half_ffn",
    )(s, mod3, g.reshape(1, D_MODEL), w13, w13, w2, g_final.reshape(1, D_MODEL))


def _prenorm_kernel(s_ref, mod_ref, g_ref, o_ref):
    h = _rms(s_ref[...], g_ref[...])
    o_ref[...] = (h * (1.0 + mod_ref[1:2, :]) + mod_ref[0:1, :]).astype(BF16)


def _prenorm(s, mod3, g, *, tm):
    rows = s.shape[0]
    return pl.pallas_call(
        _prenorm_kernel,
        grid=(rows // tm,),
        in_specs=[
            pl.BlockSpec((tm, D_MODEL), lambda i: (i, 0)),
            pl.BlockSpec((8, D_MODEL), lambda i: (0, 0)),
            pl.BlockSpec((1, D_MODEL), lambda i: (0, 0)),
        ],
        out_specs=pl.BlockSpec((tm, D_MODEL), lambda i: (i, 0)),
        out_shape=jax.ShapeDtypeStruct((rows, D_MODEL), BF16),
        compiler_params=_cparams(("arbitrary",)),
        name="prenorm",
    )(s, mod3, g.reshape(1, D_MODEL))


def _proj_hg_kernel(h_ref, w_ref, o_ref, *, n_plain_tiles):
    j = pl.program_id(1)
    acc = jnp.dot(h_ref[...], w_ref[...], preferred_element_type=F32)

    @pl.when(j < n_plain_tiles)
    def _():
        o_ref[...] = acc.astype(BF16)

    @pl.when(j >= n_plain_tiles)
    def _():
        o_ref[...] = _sigmoid(acc).astype(BF16)


def _proj_hg(h, w_hg):
    rows, n = h.shape[0], w_hg.shape[1]
    tm, tn = 1024, 1024
    return pl.pallas_call(
        functools.partial(_proj_hg_kernel, n_plain_tiles=3 * HY_WIDTH // tn),
        grid=(rows // tm, n // tn),
        in_specs=[
            pl.BlockSpec((tm, D_MODEL), lambda i, j: (i, 0)),
            pl.BlockSpec((D_MODEL, tn), lambda i, j: (0, j)),
        ],
        out_specs=pl.BlockSpec((tm, tn), lambda i, j: (i, j)),
        out_shape=jax.ShapeDtypeStruct((rows, n), BF16),
        compiler_params=_cparams(("arbitrary", "arbitrary")),
        name="proj_gate_hyena",
    )(h, w_hg)


def _proj_attn_kernel(h_ref, wa_ref, gq_ref, gkv_ref, wuq_ref, wukv_ref, tq_ref, tk_ref,
                      *out_refs, with_q):
    if with_q:
        q_ref, k_ref, v_ref = out_refs
    else:
        k_ref, v_ref = out_refs
    p = jnp.dot(h_ref[...], wa_ref[...], preferred_element_type=F32)
    kvn = _rms(p[:, Q_LORA:Q_LORA + KV_LORA], gkv_ref[...]).astype(BF16)
    t = p[:, KV_END:KV_END + 2 * QK_ROPE] * tk_ref[...]
    krot = (t + pltpu.roll(t, QK_ROPE, 1)).astype(BF16)
    kn = jnp.dot(kvn, wukv_ref[:, :MLA_HEADS * QK_NOPE], preferred_element_type=F32)
    for hd in range(MLA_HEADS):
        k_ref[:, hd * HEAD_PAD:hd * HEAD_PAD + QK_NOPE] = (
            kn[:, hd * QK_NOPE:(hd + 1) * QK_NOPE].astype(BF16))
        k_ref[:, hd * HEAD_PAD + QK_NOPE:(hd + 1) * HEAD_PAD] = krot
    vv = jnp.dot(kvn, wukv_ref[:, MLA_HEADS * QK_NOPE:], preferred_element_type=F32)
    lane = lax.broadcasted_iota(jnp.int32, (h_ref.shape[0], HEAD_PAD - V_HEAD), 1)
    unit = jnp.where(lane == 0, 1.0, 0.0).astype(BF16)
    for hd in range(MLA_HEADS):
        v_ref[:, hd * HEAD_PAD:hd * HEAD_PAD + V_HEAD] = (
            vv[:, hd * V_HEAD:(hd + 1) * V_HEAD].astype(BF16))
        v_ref[:, hd * HEAD_PAD + V_HEAD:(hd + 1) * HEAD_PAD] = unit
    if with_q:
        qn = _rms(p[:, :Q_LORA], gq_ref[...]).astype(BF16)
        tq = tq_ref[...]
        for hd in range(MLA_HEADS):
            qh = jnp.dot(qn, wuq_ref[:, hd * HEAD_PAD:(hd + 1) * HEAD_PAD],
                         preferred_element_type=F32)
            q_ref[:, hd * HEAD_PAD:(hd + 1) * HEAD_PAD] = (qh * tq).astype(BF16)


def _proj_attn(h, w_a, g_q, g_kv, w_uq, w_ukv, tab_q, tab_k, *, tm, with_q):
    rows = h.shape[0]
    na = w_a.shape[1]
    hp = MLA_HEADS * HEAD_PAD
    hv = MLA_HEADS * HEAD_PAD
    const = lambda i: (0, 0)
    row = lambda i: (i, 0)
    out_shape = [jax.ShapeDtypeStruct((rows, hp), BF16), jax.ShapeDtypeStruct((rows, hv), BF16)]
    out_specs = [pl.BlockSpec((tm, hp), row), pl.BlockSpec((tm, hv), row)]
    if with_q:
        out_shape = [jax.ShapeDtypeStruct((rows, hp), BF16)] + out_shape
        out_specs = [pl.BlockSpec((tm, hp), row)] + out_specs
    return pl.pallas_call(
        functools.partial(_proj_attn_kernel, with_q=with_q),
        grid=(rows // tm,),
        in_specs=[
            pl.BlockSpec((tm, D_MODEL), row),
            pl.BlockSpec((D_MODEL, na), const),
            pl.BlockSpec((1, Q_LORA), const),
            pl.BlockSpec((1, KV_LORA), const),
            pl.BlockSpec((Q_LORA, hp), const),
            pl.BlockSpec((KV_LORA, MLA_HEADS * (QK_NOPE + V_HEAD)), const),
            pl.BlockSpec((tm, HEAD_PAD), row),
            pl.BlockSpec((tm, 2 * QK_ROPE), row),
        ],
        out_specs=out_specs,
        out_shape=out_shape,
        compiler_params=_cparams(("arbitrary",)),
        name="proj_attn",
    )(h, w_a, g_q.reshape(1, Q_LORA), g_kv.reshape(1, KV_LORA), w_uq, w_ukv, tab_q, tab_k)


ATT_TQ = 1024
ATT_SUB = 512
ATT_KC = 1024
_NT = (((1,), (1,)), ((), ()))


def _attn_kernel(q_ref, kl_ref, kc_ref, vl_ref, vc_ref, o_ref):
    n_sub = ATT_TQ // ATT_SUB
    qs, ms, accs = [], [], []
    for r in range(n_sub):
        q = q_ref[r * ATT_SUB:(r + 1) * ATT_SUB, :]
        s = lax.dot_general(q, kc_ref[...], _NT, preferred_element_type=F32)
        m = jnp.max(s, axis=-1, keepdims=True)
        p = jnp.exp2(s - m)
        qs.append(q)
        ms.append(m)
        accs.append(jnp.dot(p.astype(BF16), vc_ref[...], preferred_element_type=F32))
    for c in range(SEQ // ATT_KC):
        for r in range(n_sub):
            s = lax.dot_general(qs[r], kl_ref[c * ATT_KC:(c + 1) * ATT_KC, :], _NT,
                                preferred_element_type=F32)
            m_new = jnp.maximum(ms[r], jnp.max(s, axis=-1, keepdims=True))
            alpha = jnp.exp2(ms[r] - m_new)
            p = jnp.exp2(s - m_new)
            accs[r] = accs[r] * alpha + jnp.dot(
                p.astype(BF16), vl_ref[c * ATT_KC:(c + 1) * ATT_KC, :],
                preferred_element_type=F32)
            ms[r] = m_new
    for r in range(n_sub):
        acc = accs[r]
        o_ref[r * ATT_SUB:(r + 1) * ATT_SUB, :] = (
            acc[:, :V_HEAD] / acc[:, V_HEAD:V_HEAD + 1]).astype(BF16)


def _attention(q, k_lat, k_ctx, v_lat, v_ctx):
    return pl.pallas_call(
        _attn_kernel,
        grid=(MLA_HEADS, SEQ // ATT_TQ),
        in_specs=[
            pl.BlockSpec((ATT_TQ, HEAD_PAD), lambda h, i: (i, h)),
            pl.BlockSpec((SEQ, HEAD_PAD), lambda h, i: (0, h)),
            pl.BlockSpec((CTX_LEN, HEAD_PAD), lambda h, i: (0, h)),
            pl.BlockSpec((SEQ, HEAD_PAD), lambda h, i: (0, h)),
            pl.BlockSpec((CTX_LEN, HEAD_PAD), lambda h, i: (0, h)),
        ],
        out_specs=pl.BlockSpec((ATT_TQ, V_HEAD), lambda h, i: (i, h)),
        out_shape=jax.ShapeDtypeStruct((SEQ, MLA_HEADS * V_HEAD), BF16),
        compiler_params=_cparams(("arbitrary", "arbitrary")),
        name="attention",
    )(q, k_lat, k_ctx, v_lat, v_ctx)


SC_HALO = 16


def _short_conv_kernel(x_ref, prev_ref, next_ref, w_ref, b_ref, o_ref):
    i = pl.program_id(0)
    tm = x_ref.shape[0]
    x = x_ref[...].astype(F32)
    prev_row = prev_ref[SC_HALO - 1:SC_HALO, :].astype(F32) * (i > 0).astype(F32)
    next_row = next_ref[0:1, :].astype(F32) * (i < pl.num_programs(0) - 1).astype(F32)
    rows = lax.broadcasted_iota(jnp.int32, x.shape, 0)
    up = jnp.where(rows == 0, prev_row, pltpu.roll(x, 1, 0))
    dn = jnp.where(rows == tm - 1, next_row, pltpu.roll(x, tm - 1, 0))
    o_ref[...] = up * w_ref[0:1, :] + x * w_ref[1:2, :] + dn * w_ref[2:3, :] + b_ref[...]


def _short_conv(hg, conv_w, conv_b, *, col0):
    tm, tc = 512, 1024
    nb = 3 * HY_WIDTH // tc
    cb0 = col0 // tc
    hb = tm // SC_HALO
    last = SEQ // SC_HALO - 1
    w8 = jnp.zeros((8, 3 * HY_WIDTH), F32).at[:3].set(conv_w)
    return pl.pallas_call(
        _short_conv_kernel,
        grid=(SEQ // tm, nb),
        in_specs=[
            pl.BlockSpec((tm, tc), lambda i, j: (i, j + cb0)),
            pl.BlockSpec((SC_HALO, tc), lambda i, j: (jnp.maximum(i * hb - 1, 0), j + cb0)),
            pl.BlockSpec((SC_HALO, tc), lambda i, j: (jnp.minimum((i + 1) * hb, last), j + cb0)),
            pl.BlockSpec((8, tc), lambda i, j: (0, j)),
            pl.BlockSpec((1, tc), lambda i, j: (0, j)),
        ],
        out_specs=pl.BlockSpec((tm, tc), lambda i, j: (i, j)),
        out_shape=jax.ShapeDtypeStruct((SEQ, 3 * HY_WIDTH), F32),
        compiler_params=_cparams(("arbitrary", "arbitrary")),
        name="short_conv",
    )(hg, hg, hg, w8, conv_b.reshape(1, -1))


HY_ZW = 128


def _dot_split3(a, b):
    a_hi = a.astype(BF16)
    b_hi = b.astype(BF16)
    a_lo = (a - a_hi.astype(F32)).astype(BF16)
    b_lo = (b - b_hi.astype(F32)).astype(BF16)
    dot = functools.partial(jnp.dot, preferred_element_type=F32)
    return dot(a_hi, b_hi) + (dot(a_hi, b_lo) + dot(a_lo, b_hi))


def _filter_kernel(z_ref, w1_ref, b1_ref, w2_ref, b2_ref, w3_ref, fr_ref, dec_ref,
                   h_ref, ss_ref):
    hi = lax.Precision.HIGHEST
    z = z_ref[...]
    fr = fr_ref[...]
    h = jnp.sin(fr * (jnp.dot(z, w1_ref[...], precision=hi, preferred_element_type=F32)
                      + b1_ref[...]))
    h = jnp.sin(fr * (jnp.dot(h, w2_ref[...], precision=hi, preferred_element_type=F32)
                      + b2_ref[...]))
    h = _dot_split3(h, w3_ref[...])
    dist = z[:, HY_EMB:HY_EMB + 1]
    h = h * (jnp.exp(-dist * dec_ref[...]) + HY_MOD_SHIFT)
    h_ref[...] = h

    @pl.when(pl.program_id(0) == 0)
    def _():
        ss_ref[...] = jnp.zeros_like(ss_ref)

    ss_ref[...] += jnp.sum(h * h, axis=0, keepdims=True)


def _hyena_filters(z_p, w1p, b1, w2, b2, w3, freq, decay2):
    tl = 512
    nw = HY_ORDER * HY_WIDTH
    const = lambda i: (0, 0)
    return pl.pallas_call(
        _filter_kernel,
        grid=(SEQ // tl,),
        in_specs=[
            pl.BlockSpec((tl, HY_ZW), lambda i: (i, 0)),
            pl.BlockSpec((HY_ZW, HY_FILTER_W), const),
            pl.BlockSpec((1, HY_FILTER_W), const),
            pl.BlockSpec((HY_FILTER_W, HY_FILTER_W), const),
            pl.BlockSpec((1, HY_FILTER_W), const),
            pl.BlockSpec((HY_FILTER_W, nw), const),
            pl.BlockSpec((1, HY_FILTER_W), const),
            pl.BlockSpec((1, nw), const),
        ],
        out_specs=[pl.BlockSpec((tl, nw), lambda i: (i, 0)), pl.BlockSpec((1, nw), const)],
        out_shape=[jax.ShapeDtypeStruct((SEQ, nw), F32), jax.ShapeDtypeStruct((1, nw), F32)],
        compiler_params=_cparams(("arbitrary",)),
        name="hyena_filters",
    )(z_p, w1p, b1.reshape(1, -1), w2, b2.reshape(1, -1), w3, freq.reshape(1, -1), decay2)


FFT_LANES = 128
FFT_G = 32
FFT_STEPS = FFT_N2 // FFT_G
FFT_ROWS = 2 * FFT_N1
Y_PITCH = FFT_ROWS + 8
T_PITCH = FFT_N2 + 8


def _pad_rows_in(src_ref, dst_scr, scale=None):
    for n1 in range(FFT_HALF):
        v = src_ref[n1 * FFT_N2:(n1 + 1) * FFT_N2, :]
        dst_scr[n1 * T_PITCH:n1 * T_PITCH + FFT_N2, :] = v if scale is None else v * scale


def _fft_stage_a(t, src_scr, ma_ref, y_scr):
    for g in range(FFT_G):
        j = t * FFT_G + g
        xj = src_scr[pl.ds(j, FFT_HALF, stride=T_PITCH), :]
        a = jnp.dot(ma_ref[g], xj.astype(BF16), preferred_element_type=F32)
        y_scr[pl.ds(pl.multiple_of(j * Y_PITCH, 8), FFT_ROWS), :] = a


def _fft_load_pair(tb, g, y_scr):
    cols = []
    for k1 in (tb * FFT_G + g, tb * FFT_G + g + 1):
        re = y_scr[pl.ds(k1, FFT_N2, stride=Y_PITCH), :]
        im = y_scr[pl.ds(FFT_N1 + k1, FFT_N2, stride=Y_PITCH), :]
        cols.append(jnp.concatenate([re, im], axis=0))
    return jnp.concatenate(cols, axis=1).astype(BF16)


def _hyena_spectrum_kernel(h_ref, ss_ref, ma_ref, g2_ref, o_ref, y_scr, h_scr):
    t = pl.program_id(1)

    @pl.when(t == 0)
    def _():
        _pad_rows_in(h_ref, h_scr, lax.rsqrt(ss_ref[...] + NORM_EPS))

    @pl.when(t < FFT_STEPS)
    def _():
        _fft_stage_a(t, h_scr, ma_ref, y_scr)

    @pl.when(t >= FFT_STEPS)
    def _():
        tb = t - FFT_STEPS
        for g in range(0, FFT_G, 2):
            z = jnp.dot(g2_ref[...], _fft_load_pair(tb, g, y_scr), preferred_element_type=F32)
            o_ref[g] = z[:, :FFT_LANES].astype(BF16)
            o_ref[g + 1] = z[:, FFT_LANES:].astype(BF16)


def _hyena_spectrum(filt, sumsq, m_a, g2):
    nb = filt.shape[1] // FFT_LANES
    last = FFT_STEPS - 1
    return pl.pallas_call(
        _hyena_spectrum_kernel,
        grid=(nb, 2 * FFT_STEPS),
        in_specs=[
            pl.BlockSpec((SEQ, FFT_LANES), lambda c, t: (0, c)),
            pl.BlockSpec((1, FFT_LANES), lambda c, t: (0, c)),
            pl.BlockSpec((FFT_G, FFT_ROWS, FFT_HALF), lambda c, t: (jnp.minimum(t, last), 0, 0)),
            pl.BlockSpec((FFT_ROWS, FFT_ROWS), lambda c, t: (0, 0)),
        ],
        out_specs=pl.BlockSpec((None, FFT_G, FFT_ROWS, FFT_LANES),
                               lambda c, t: (c, jnp.maximum(t - FFT_STEPS, 0), 0, 0)),
        out_shape=jax.ShapeDtypeStruct((nb, FFT_N1, FFT_ROWS, FFT_LANES), BF16),
        scratch_shapes=[pltpu.VMEM((FFT_N2 * Y_PITCH, FFT_LANES), F32),
                        pltpu.VMEM((FFT_HALF * T_PITCH, FFT_LANES), F32)],
        compiler_params=_cparams(("arbitrary", "arbitrary")),
        name="hyena_spectrum",
    )(filt, sumsq, m_a, g2)


def _long_conv_kernel(u_ref, x_ref, skip_ref, ma_ref, hf_ref, g2_ref, g2i_ref, mc_ref,
                      o_ref, y_scr, stage_scr, u_scr, x_scr):
    t = pl.program_id(1)

    @pl.when(t == 0)
    def _():
        _pad_rows_in(u_ref, u_scr)
        _pad_rows_in(x_ref, x_scr)

    @pl.when(t < FFT_STEPS)
    def _():
        _fft_stage_a(t, u_scr, ma_ref, y_scr)

    @pl.when((t >= FFT_STEPS) & (t < 2 * FFT_STEPS))
    def _():
        tb = t - FFT_STEPS
        for g in range(0, FFT_G, 2):
            stage_scr[g // 2] = _fft_load_pair(tb, g, y_scr)
        for g in range(0, FFT_G, 2):
            z = jnp.dot(g2_ref[...], stage_scr[g // 2], preferred_element_type=F32)
            hf = jnp.concatenate([hf_ref[g], hf_ref[g + 1]], axis=1).astype(F32)
            zr, zi = z[:FFT_N2], z[FFT_N2:]
            hr, hi = hf[:FFT_N2], hf[FFT_N2:]
            p = jnp.concatenate([zr * hr - zi * hi, zr * hi + zi * hr], axis=0).astype(BF16)
            b = jnp.dot(g2i_ref[...], p, preferred_element_type=F32)
            for half, k1 in enumerate((tb * FFT_G + g, tb * FFT_G + g + 1)):
                lanes = slice(half * FFT_LANES, (half + 1) * FFT_LANES)
                y_scr[pl.ds(k1, FFT_N2, stride=Y_PITCH), :] = b[:FFT_N2, lanes]
                y_scr[pl.ds(FFT_N1 + k1, FFT_N2, stride=Y_PITCH), :] = b[FFT_N2:, lanes]

    @pl.when(t >= 2 * FFT_STEPS)
    def _():
        tc = t - 2 * FFT_STEPS
        for g in range(FFT_G):
            m2 = tc * FFT_G + g
            bm = y_scr[pl.ds(pl.multiple_of(m2 * Y_PITCH, 8), FFT_ROWS), :]
            y = jnp.dot(mc_ref[g], bm.astype(BF16), preferred_element_type=F32)
            rows = pl.ds(m2, FFT_HALF, stride=T_PITCH)
            u_scr[rows, :] = x_scr[rows, :] * (y + skip_ref[...] * u_scr[rows, :])

    @pl.when(t == 3 * FFT_STEPS - 1)
    def _():
        for n1 in range(FFT_HALF):
            o_ref[n1 * FFT_N2:(n1 + 1) * FFT_N2, :] = u_scr[n1 * T_PITCH:n1 * T_PITCH + FFT_N2, :]


def _long_conv_gate(u, ucol, x, xcol, skip, hf, hf0, m_a, g2, g2i, m_c):
    nb = HY_WIDTH // FFT_LANES
    ub, xb = ucol // FFT_LANES, xcol // FFT_LANES
    last = FFT_STEPS - 1
    step = lambda t, phase: jnp.clip(t - phase * FFT_STEPS, 0, last)
    return pl.pallas_call(
        _long_conv_kernel,
        grid=(nb, 3 * FFT_STEPS),
        in_specs=[
            pl.BlockSpec((SEQ, FFT_LANES), lambda c, t: (0, c + ub)),
            pl.BlockSpec((SEQ, FFT_LANES), lambda c, t: (0, c + xb)),
            pl.BlockSpec((1, FFT_LANES), lambda c, t: (0, c)),
            pl.BlockSpec((FFT_G, FFT_ROWS, FFT_HALF), lambda c, t: (step(t, 0), 0, 0)),
            pl.BlockSpec((None, FFT_G, FFT_ROWS, FFT_LANES),
                         lambda c, t: (c + hf0, step(t, 1), 0, 0)),
            pl.BlockSpec((FFT_ROWS, FFT_ROWS), lambda c, t: (0, 0)),
            pl.BlockSpec((FFT_ROWS, FFT_ROWS), lambda c, t: (0, 0)),
            pl.BlockSpec((FFT_G, FFT_HALF, FFT_ROWS), lambda c, t: (step(t, 2), 0, 0)),
        ],
        out_specs=pl.BlockSpec((SEQ, FFT_LANES), lambda c, t: (0, c)),
        out_shape=jax.ShapeDtypeStruct((SEQ, HY_WIDTH), F32),
        scratch_shapes=[pltpu.VMEM((FFT_N2 * Y_PITCH, FFT_LANES), F32),
                        pltpu.VMEM((FFT_G // 2, FFT_ROWS, 2 * FFT_LANES), BF16),
                        pltpu.VMEM((FFT_HALF * T_PITCH, FFT_LANES), F32),
                        pltpu.VMEM((FFT_HALF * T_PITCH, FFT_LANES), F32)],
        compiler_params=_cparams(("arbitrary", "arbitrary")),
        name="long_conv_gate",
    )(u, x, skip.reshape(1, HY_WIDTH), m_a, hf, g2, g2i, m_c)


def _merge_kernel(a_ref, hy_ref, ga_ref, gh_ref, wa_ref, wh_ref, o_ref):
    ya = jnp.dot(a_ref[...], wa_ref[...], preferred_element_type=F32)
    yh = jnp.dot(hy_ref[...].astype(BF16), wh_ref[...], preferred_element_type=F32)
    o_ref[...] = (ga_ref[...].astype(F32) * ya + gh_ref[...].astype(F32) * yh).astype(BF16)


def _merge(attn, hy, hg, gate_col0, w_attn_o, w_hy_o):
    tm, tn = 1024, 1024
    ga0 = gate_col0 // tn
    gh0 = (gate_col0 + D_MODEL) // tn
    row = lambda i, j: (i, 0)
    return pl.pallas_call(
        _merge_kernel,
        grid=(SEQ // tm, D_MODEL // tn),
        in_specs=[
            pl.BlockSpec((tm, MLA_HEADS * V_HEAD), row),
            pl.BlockSpec((tm, HY_WIDTH), row),
            pl.BlockSpec((tm, tn), lambda i, j: (i, j + ga0)),
            pl.BlockSpec((tm, tn), lambda i, j: (i, j + gh0)),
            pl.BlockSpec((MLA_HEADS * V_HEAD, tn), lambda i, j: (0, j)),
            pl.BlockSpec((HY_WIDTH, tn), lambda i, j: (0, j)),
        ],
        out_specs=pl.BlockSpec((tm, tn), lambda i, j: (i, j)),
        out_shape=jax.ShapeDtypeStruct((SEQ, D_MODEL), BF16),
        compiler_params=_cparams(("arbitrary", "arbitrary")),
        name="merge_branches",
    )(attn, hy, hg, hg, w_attn_o, w_hy_o)


def _out_proj_kernel(y_ref, w_ref, s_ref, mod_ref, o_ref):
    xm = jnp.dot(y_ref[...], w_ref[...], preferred_element_type=F32)
    o_ref[...] = s_ref[...] + mod_ref[0:1, :] * xm


def _out_proj(y, w_out, s, mod_gate):
    tm = 512
    row = lambda i: (i, 0)
    const = lambda i: (0, 0)
    return pl.pallas_call(
        _out_proj_kernel,
        grid=(SEQ // tm,),
        in_specs=[
            pl.BlockSpec((tm, D_MODEL), row),
            pl.BlockSpec((D_MODEL, D_MODEL), const),
            pl.BlockSpec((tm, D_MODEL), row),
            pl.BlockSpec((8, D_MODEL), const),
        ],
        out_specs=pl.BlockSpec((tm, D_MODEL), row),
        out_shape=jax.ShapeDtypeStruct((SEQ, D_MODEL), F32),
        compiler_params=_cparams(("arbitrary",)),
        name="out_proj",
    )(y, w_out, s, mod_gate)


def _rope_tables():
    t = np.arange(SEQ)
    pos = np.stack([t // GRID_W, t % GRID_W], axis=1).astype(np.float64)
    inv_freq = ROPE_THETA ** (-np.arange(0, ROPE_AXIS, 2, dtype=np.float64) / ROPE_AXIS)
    i = np.arange(QK_ROPE)
    ang = pos[:, i // ROPE_AXIS] * inv_freq[i % (ROPE_AXIS // 2)][None, :]
    cos, sin = np.cos(ang), np.sin(ang)
    tab_q = np.concatenate([np.ones((SEQ, QK_NOPE)), cos, sin], axis=1) * (ATTN_SCALE * math.log2(math.e))
    tab_k = np.concatenate([cos, sin], axis=1)
    tab_k_ctx = np.concatenate([np.ones((CTX_LEN, QK_ROPE)), np.zeros((CTX_LEN, QK_ROPE))], axis=1)
    return (jnp.asarray(tab_q, F32), jnp.asarray(tab_k, F32), jnp.asarray(tab_k_ctx, F32))


def _rope_swap():
    i = np.arange(QK_ROPE)
    first_half = (i % ROPE_AXIS) < ROPE_AXIS // 2
    partner = np.where(first_half, i + ROPE_AXIS // 2, i - ROPE_AXIS // 2)
    sign = np.where(first_half, -1.0, 1.0)
    return partner, sign


def _filter_features():
    pos = np.arange(SEQ, dtype=np.float64)[:, None]
    t01 = pos / (SEQ - 1)
    bands = np.linspace(1e-4, HY_BANDS - 1, HY_BANDS)[None, :]
    ang = bands * (2.0 * math.pi / SEQ) * pos
    dist = np.abs(pos - (SEQ // 2)) / (SEQ / 2.0)
    z = np.concatenate([t01, np.cos(ang), -np.sin(ang), dist], axis=1)
    z = np.pad(z, ((0, 0), (0, HY_ZW - z.shape[1])))
    return jnp.asarray(z, F32)


def _dft_tables():
    n1 = np.arange(FFT_HALF)
    k1 = np.arange(FFT_N1)
    n2 = np.arange(FFT_N2)
    n = FFT_N2 * n1[None, None, :] + n2[:, None, None]
    ph = (k1[None, :, None] * n) % FFT_N
    th = 2.0 * math.pi * ph / FFT_N
    m_a = np.concatenate([np.cos(th), -np.sin(th)], axis=1)
    ph2 = (n2[:, None] * n2[None, :]) % FFT_N2
    th2 = 2.0 * math.pi * ph2 / FFT_N2
    c, s = np.cos(th2), np.sin(th2)
    g2 = np.block([[c, s], [-s, c]])
    g2i = np.block([[c, -s], [s, c]])
    n = FFT_N2 * (n1[None, :, None] + FFT_HALF // 2) + n2[:, None, None]
    ph = (k1[None, None, :] * n) % FFT_N
    th = 2.0 * math.pi * ph / FFT_N
    m_c = np.concatenate([np.cos(th), -np.sin(th)], axis=2) / FFT_N
    return tuple(jnp.asarray(t, F32).astype(BF16) for t in (m_a, g2, g2i, m_c))


def _rows8(m, lo, hi):
    return jnp.zeros((8, D_MODEL), F32).at[:hi - lo].set(m[lo:hi])


def kernel(x, c, ctx, c_ctx, w_mod, b_mod, g_ffn1, w13_ffn1, w2_ffn1, g_mix, w_in, g_q, w_uq,
           g_kv, w_ukv, w_attn_o, hy_conv_w, hy_conv_b, hy_w1, hy_b1, hy_w2, hy_b2, hy_w3,
           hy_freq, hy_skip, w_hy_o, w_out, g_ffn2, w13_ffn2, w2_ffn2, g_final):
    xs = x[0]
    cs = ctx[0]
    li = 0

    c8 = jnp.zeros((8, D_MODEL), F32).at[0].set(c[0]).at[1].set(c_ctx)
    mod = _modulation(c8, w_mod[li], b_mod[li])
    mx = mod[0].reshape(N_MOD, D_MODEL)
    mc = mod[1].reshape(N_MOD, D_MODEL)

    w13 = w13_ffn1[li]
    w2 = w2_ffn1[li]
    x1 = _half_ffn(xs, _rows8(mx, 0, 3), g_ffn1[li], w13, w2, g_final, tm=FFN_TM, final_norm=False)
    c1 = _half_ffn(cs, _rows8(mc, 0, 3), g_ffn1[li], w13, w2, g_final, tm=CTX_LEN,
                   final_norm=False)

    hx = _prenorm(x1, _rows8(mx, 3, 5), g_mix[li], tm=512)
    hc = _prenorm(c1, _rows8(mc, 3, 5), g_mix[li], tm=CTX_LEN)

    win = w_in[li]
    partner, sign = _rope_swap()
    w_kr = win[:, KV_END:KR_END]
    w_a = jnp.concatenate([win[:, :KV_END], w_kr, w_kr[:, partner] * sign], axis=1).astype(BF16)
    w_hg = win[:, KR_END:].astype(BF16)
    wq = w_uq[li].reshape(Q_LORA, MLA_HEADS, QK_NOPE + QK_ROPE)
    wq_r = wq[:, :, QK_NOPE:]
    wq_p = jnp.concatenate([wq[:, :, :QK_NOPE], wq_r, wq_r[:, :, partner] * sign], axis=2)
    wq_p = wq_p.reshape(Q_LORA, MLA_HEADS * HEAD_PAD).astype(BF16)
    wkv = w_ukv[li].reshape(KV_LORA, MLA_HEADS, QK_NOPE + V_HEAD)
    wkv_p = jnp.concatenate([wkv[:, :, :QK_NOPE].reshape(KV_LORA, -1),
                             wkv[:, :, QK_NOPE:].reshape(KV_LORA, -1)], axis=1).astype(BF16)

    tab_q, tab_k, tab_k_ctx = _rope_tables()
    q, k_lat, v_lat = _proj_attn(hx, w_a, g_q[li], g_kv[li], wq_p, wkv_p, tab_q, tab_k,
                                 tm=512, with_q=True)
    k_ctx, v_ctx = _proj_attn(hc, w_a, g_q[li], g_kv[li], wq_p, wkv_p, tab_q[:CTX_LEN],
                              tab_k_ctx, tm=CTX_LEN, with_q=False)
    attn = _attention(q, k_lat, k_ctx, v_lat, v_ctx)

    hg = _proj_hg(hx, w_hg)
    u3 = _short_conv(hg, hy_conv_w[li], hy_conv_b[li], col0=0)
    w1p = jnp.zeros((HY_ZW, HY_FILTER_W), F32).at[:HY_EMB].set(hy_w1[li])
    decay = np.abs(np.linspace(HY_MIN_DECAY, HY_MAX_DECAY, HY_WIDTH))
    decay2 = jnp.asarray(np.tile(decay, HY_ORDER)[None, :], F32)
    filt, sumsq = _hyena_filters(_filter_features(), w1p, hy_b1[li], hy_w2[li], hy_b2[li],
                                 hy_w3[li], hy_freq[li], decay2)
    m_a, g2, g2i, m_c = _dft_tables()
    hf = _hyena_spectrum(filt, sumsq, m_a, g2)
    nb = HY_WIDTH // FFT_LANES
    z = _long_conv_gate(u3, 0, u3, HY_WIDTH, hy_skip[li][0], hf, 0, m_a, g2, g2i, m_c)
    hy = _long_conv_gate(z, 0, u3, 2 * HY_WIDTH, hy_skip[li][1], hf, nb, m_a, g2, g2i, m_c)

    y = _merge(attn, hy, hg, 3 * HY_WIDTH, w_attn_o[li].astype(BF16), w_hy_o[li].astype(BF16))
    x2 = _out_proj(y, w_out[li].astype(BF16), x1, _rows8(mx, 5, 6))

    out = _half_ffn(x2, _rows8(mx, 6, 9), g_ffn2[li], w13_ffn2[li], w2_ffn2[li], g_final,
                    tm=FFN_TM, final_norm=True)
    return out[None]
```

```python
import functools
import math

import numpy as np
import jax
import jax.numpy as jnp
from jax import lax
from jax.experimental import pallas as pl
from jax.experimental.pallas import tpu as pltpu

F32 = jnp.float32
BF16 = jnp.bfloat16

D_MODEL = 2048
SEQ = 8192
GRID_W = 64
CTX_LEN = 256
N_MOD = 9
D_FF = 5632
NORM_EPS = 1e-6

MLA_HEADS = 16
Q_LORA = 512
KV_LORA = 512
QK_NOPE = 128
QK_ROPE = 64
V_HEAD = 128
ROPE_AXIS = QK_ROPE // 2
ROPE_THETA = 10000.0
ATTN_SCALE = (QK_NOPE + QK_ROPE) ** -0.5
HEAD_PAD = 256

HY_WIDTH = 1024
HY_ORDER = 2
HY_EMB = 33
HY_BANDS = (HY_EMB - 1) // 2
HY_FILTER_W = 64
HY_TARGET = 1e-2
HY_FAST_DECAY = 0.3
HY_SLOW_DECAY = 1.5
HY_MAX_DECAY = math.log(HY_TARGET) / HY_FAST_DECAY
HY_MIN_DECAY = math.log(HY_TARGET) / HY_SLOW_DECAY
HY_MOD_SHIFT = 0.05

Q_END = Q_LORA
KV_END = Q_END + KV_LORA
KR_END = KV_END + QK_ROPE
HY_END = KR_END + 3 * HY_WIDTH

FFT_N = 2 * SEQ
FFT_N1 = 128
FFT_N2 = 128
FFT_HALF = SEQ // FFT_N2

V7X_VMEM_LIMIT = 60 * 1024 * 1024


def _cparams(sem, vmem=V7X_VMEM_LIMIT):
    return pltpu.CompilerParams(dimension_semantics=sem, vmem_limit_bytes=vmem)


def _sigmoid(x):
    return 1.0 / (1.0 + jnp.exp(-x))


def _rms(x, g):
    var = jnp.mean(x * x, axis=-1, keepdims=True)
    return x * lax.rsqrt(var + NORM_EPS) * g


def _mod_kernel(c_ref, w_ref, b_ref, o_ref):
    c = c_ref[...]
    a = c * _sigmoid(c)
    o_ref[...] = jnp.dot(a, w_ref[...], preferred_element_type=F32) + b_ref[...]


def _modulation(c8, w_mod, b_mod):
    n = w_mod.shape[1]
    tn = 1024
    return pl.pallas_call(
        _mod_kernel,
        grid=(n // tn,),
        in_specs=[
            pl.BlockSpec((8, D_MODEL), lambda j: (0, 0)),
            pl.BlockSpec((D_MODEL, tn), lambda j: (0, j)),
            pl.BlockSpec((1, tn), lambda j: (0, j)),
        ],
        out_specs=pl.BlockSpec((8, tn), lambda j: (0, j)),
        out_shape=jax.ShapeDtypeStruct((8, n), F32),
        compiler_params=_cparams(("arbitrary",)),
        name="modulation",
    )(c8, w_mod, b_mod.reshape(1, n))


def _ffn_kernel(s_ref, mod_ref, g_ref, w1_ref, w3_ref, w2_ref, gf_ref, o_ref, h_scr,
                w1_scr, w3_scr, w2_scr, *, final_norm):
    f = pl.program_id(1)
    slot = f % 2

    def round_tile():
        w1_scr[slot] = w1_ref[...].astype(BF16)
        w3_scr[slot] = w3_ref[...].astype(BF16)
        w2_scr[slot] = w2_ref[...].astype(BF16)

    @pl.when(f == 0)
    def _():
        h = _rms(s_ref[...], g_ref[...])
        h = h * (1.0 + mod_ref[1:2, :]) + mod_ref[0:1, :]
        h_scr[...] = h.astype(BF16)
        o_ref[...] = jnp.zeros_like(o_ref)
        round_tile()

    @pl.when(f > 0)
    def _():
        round_tile()
        h = h_scr[...]
        a = jnp.dot(h, w1_scr[1 - slot], preferred_element_type=F32)
        b = jnp.dot(h, w3_scr[1 - slot], preferred_element_type=F32)
        act = (a * _sigmoid(a) * b).astype(BF16)
        o_ref[...] += jnp.dot(act, w2_scr[1 - slot], preferred_element_type=F32)

    @pl.when(f == pl.num_programs(1) - 1)
    def _():
        out = s_ref[...] + 0.5 * mod_ref[2:3, :] * o_ref[...]
        if final_norm:
            out = _rms(out, gf_ref[...])
        o_ref[...] = out


FFN_TM = 1024


def _half_ffn(s, mod3, g, w13, w2, g_final, *, tm, final_norm):
    rows = s.shape[0]
    tf = 256
    nf = D_FF // tf
    tile = lambda f: jnp.minimum(f, nf - 1)
    return pl.pallas_call(
        functools.partial(_ffn_kernel, final_norm=final_norm),
        grid=(rows // tm, nf + 1),
        in_specs=[
            pl.BlockSpec((tm, D_MODEL), lambda i, f: (i, 0), pipeline_mode=pl.Buffered(1)),
            pl.BlockSpec((8, D_MODEL), lambda i, f: (0, 0)),
            pl.BlockSpec((1, D_MODEL), lambda i, f: (0, 0)),
            pl.BlockSpec((D_MODEL, tf), lambda i, f: (0, tile(f))),
            pl.BlockSpec((D_MODEL, tf), lambda i, f: (0, tile(f) + nf)),
            pl.BlockSpec((tf, D_MODEL), lambda i, f: (tile(f), 0)),
            pl.BlockSpec((1, D_MODEL), lambda i, f: (0, 0)),
        ],
        out_specs=pl.BlockSpec((tm, D_MODEL), lambda i, f: (i, 0)),
        out_shape=jax.ShapeDtypeStruct((rows, D_MODEL), F32),
        scratch_shapes=[pltpu.VMEM((tm, D_MODEL), BF16),
                        pltpu.VMEM((2, D_MODEL, tf), BF16),
                        pltpu.VMEM((2, D_MODEL, tf), BF16),
                        pltpu.VMEM((2, tf, D_MODEL), BF16)],
        compiler_params=_cparams(("arbitrary", "arbitrary")),
        name="half_ffn",
    )(s, mod3, g.reshape(1, D_MODEL), w13, w13, w2, g_final.reshape(1, D_MODEL))


def _prenorm_kernel(s_ref, mod_ref, g_ref, o_ref):
    h = _rms(s_ref[...], g_ref[...])
    o_ref[...] = (h * (1.0 + mod_ref[1:2, :]) + mod_ref[0:1, :]).astype(BF16)


def _prenorm(s, mod3, g, *, tm):
    rows = s.shape[0]
    return pl.pallas_call(
        _prenorm_kernel,
        grid=(rows // tm,),
        in_specs=[
            pl.BlockSpec((tm, D_MODEL), lambda i: (i, 0)),
            pl.BlockSpec((8, D_MODEL), lambda i: (0, 0)),
            pl.BlockSpec((1, D_MODEL), lambda i: (0, 0)),
        ],
        out_specs=pl.BlockSpec((tm, D_MODEL), lambda i: (i, 0)),
        out_shape=jax.ShapeDtypeStruct((rows, D_MODEL), BF16),
        compiler_params=_cparams(("arbitrary",)),
        name="prenorm",
    )(s, mod3, g.reshape(1, D_MODEL))


def _proj_hg_kernel(h_ref, w_ref, o_ref, *, n_plain_tiles):
    j = pl.program_id(1)
    acc = jnp.dot(h_ref[...], w_ref[...], preferred_element_type=F32)

    @pl.when(j < n_plain_tiles)
    def _():
        o_ref[...] = acc.astype(BF16)

    @pl.when(j >= n_plain_tiles)
    def _():
        o_ref[...] = _sigmoid(acc).astype(BF16)


def _proj_hg(h, w_hg):
    rows, n = h.shape[0], w_hg.shape[1]
    tm, tn = 1024, 1024
    return pl.pallas_call(
        functools.partial(_proj_hg_kernel, n_plain_tiles=3 * HY_WIDTH // tn),
        grid=(rows // tm, n // tn),
        in_specs=[
            pl.BlockSpec((tm, D_MODEL), lambda i, j: (i, 0)),
            pl.BlockSpec((D_MODEL, tn), lambda i, j: (0, j)),
        ],
        out_specs=pl.BlockSpec((tm, tn), lambda i, j: (i, j)),
        out_shape=jax.ShapeDtypeStruct((rows, n), BF16),
        compiler_params=_cparams(("arbitrary", "arbitrary")),
        name="proj_gate_hyena",
    )(h, w_hg)


def _proj_attn_kernel(h_ref, wa_ref, gq_ref, gkv_ref, wuq_ref, wukv_ref, tq_ref, tk_ref,
                      *out_refs, with_q):
    if with_q:
        q_ref, k_ref, v_ref = out_refs
    else:
        k_ref, v_ref = out_refs
    p = jnp.dot(h_ref[...], wa_ref[...], preferred_element_type=F32)
    kvn = _rms(p[:, Q_LORA:Q_LORA + KV_LORA], gkv_ref[...]).astype(BF16)
    t = p[:, KV_END:KV_END + 2 * QK_ROPE] * tk_ref[...]
    krot = (t + pltpu.roll(t, QK_ROPE, 1)).astype(BF16)
    kn = jnp.dot(kvn, wukv_ref[:, :MLA_HEADS * QK_NOPE], preferred_element_type=F32)
    for hd in range(MLA_HEADS):
        k_ref[:, hd * HEAD_PAD:hd * HEAD_PAD + QK_NOPE] = (
            kn[:, hd * QK_NOPE:(hd + 1) * QK_NOPE].astype(BF16))
        k_ref[:, hd * HEAD_PAD + QK_NOPE:(hd + 1) * HEAD_PAD] = krot
    vv = jnp.dot(kvn, wukv_ref[:, MLA_HEADS * QK_NOPE:], preferred_element_type=F32)
    lane = lax.broadcasted_iota(jnp.int32, (h_ref.shape[0], HEAD_PAD - V_HEAD), 1)
    unit = jnp.where(lane == 0, 1.0, 0.0).astype(BF16)
    for hd in range(MLA_HEADS):
        v_ref[:, hd * HEAD_PAD:hd * HEAD_PAD + V_HEAD] = (
            vv[:, hd * V_HEAD:(hd + 1) * V_HEAD].astype(BF16))
        v_ref[:, hd * HEAD_PAD + V_HEAD:(hd + 1) * HEAD_PAD] = unit
    if with_q:
        qn = _rms(p[:, :Q_LORA], gq_ref[...]).astype(BF16)
        tq = tq_ref[...]
        for hd in range(MLA_HEADS):
            qh = jnp.dot(qn, wuq_ref[:, hd * HEAD_PAD:(hd + 1) * HEAD_PAD],
                         preferred_element_type=F32)
            q_ref[:, hd * HEAD_PAD:(hd + 1) * HEAD_PAD] = (qh * tq).astype(BF16)


def _proj_attn(h, w_a, g_q, g_kv, w_uq, w_ukv, tab_q, tab_k, *, tm, with_q):
    rows = h.shape[0]
    na = w_a.shape[1]
    hp = MLA_HEADS * HEAD_PAD
    hv = MLA_HEADS * HEAD_PAD
    const = lambda i: (0, 0)
    row = lambda i: (i, 0)
    out_shape = [jax.ShapeDtypeStruct((rows, hp), BF16), jax.ShapeDtypeStruct((rows, hv), BF16)]
    out_specs = [pl.BlockSpec((tm, hp), row), pl.BlockSpec((tm, hv), row)]
    if with_q:
        out_shape = [jax.ShapeDtypeStruct((rows, hp), BF16)] + out_shape
        out_specs = [pl.BlockSpec((tm, hp), row)] + out_specs
    return pl.pallas_call(
        functools.partial(_proj_attn_kernel, with_q=with_q),
        grid=(rows // tm,),
        in_specs=[
            pl.BlockSpec((tm, D_MODEL), row),
            pl.BlockSpec((D_MODEL, na), const),
            pl.BlockSpec((1, Q_LORA), const),
            pl.BlockSpec((1, KV_LORA), const),
            pl.BlockSpec((Q_LORA, hp), const),
            pl.BlockSpec((KV_LORA, MLA_HEADS * (QK_NOPE + V_HEAD)), const),
            pl.BlockSpec((tm, HEAD_PAD), row),
            pl.BlockSpec((tm, 2 * QK_ROPE), row),
        ],
        out_specs=out_specs,
        out_shape=out_shape,
        compiler_params=_cparams(("arbitrary",)),
        name="proj_attn",
    )(h, w_a, g_q.reshape(1, Q_LORA), g_kv.reshape(1, KV_LORA), w_uq, w_ukv, tab_q, tab_k)


ATT_TQ = 1024
ATT_SUB = 512
ATT_KC = 1024
_NT = (((1,), (1,)), ((), ()))


def _attn_kernel(q_ref, kl_ref, kc_ref, vl_ref, vc_ref, o_ref):
    n_sub = ATT_TQ // ATT_SUB
    qs, ms, accs = [], [], []
    for r in range(n_sub):
        q = q_ref[r * ATT_SUB:(r + 1) * ATT_SUB, :]
        s = lax.dot_general(q, kc_ref[...], _NT, preferred_element_type=F32)
        m = jnp.max(s, axis=-1, keepdims=True)
        p = jnp.exp2(s - m)
        qs.append(q)
        ms.append(m)
        accs.append(jnp.dot(p.astype(BF16), vc_ref[...], preferred_element_type=F32))
    for c in range(SEQ // ATT_KC):
        for r in range(n_sub):
            s = lax.dot_general(qs[r], kl_ref[c * ATT_KC:(c + 1) * ATT_KC, :], _NT,
                                preferred_element_type=F32)
            m_new = jnp.maximum(ms[r], jnp.max(s, axis=-1, keepdims=True))
            alpha = jnp.exp2(ms[r] - m_new)
            p = jnp.exp2(s - m_new)
            accs[r] = accs[r] * alpha + jnp.dot(
                p.astype(BF16), vl_ref[c * ATT_KC:(c + 1) * ATT_KC, :],
                preferred_element_type=F32)
            ms[r] = m_new
    for r in range(n_sub):
        acc = accs[r]
        o_ref[r * ATT_SUB:(r + 1) * ATT_SUB, :] = (
            acc[:, :V_HEAD] / acc[:, V_HEAD:V_HEAD + 1]).astype(BF16)


def _attention(q, k_lat, k_ctx, v_lat, v_ctx):
    return pl.pallas_call(
        _attn_kernel,
        grid=(MLA_HEADS, SEQ // ATT_TQ),
        in_specs=[
            pl.BlockSpec((ATT_TQ, HEAD_PAD), lambda h, i: (i, h)),
            pl.BlockSpec((SEQ, HEAD_PAD), lambda h, i: (0, h)),
            pl.BlockSpec((CTX_LEN, HEAD_PAD), lambda h, i: (0, h)),
            pl.BlockSpec((SEQ, HEAD_PAD), lambda h, i: (0, h)),
            pl.BlockSpec((CTX_LEN, HEAD_PAD), lambda h, i: (0, h)),
        ],
        out_specs=pl.BlockSpec((ATT_TQ, V_HEAD), lambda h, i: (i, h)),
        out_shape=jax.ShapeDtypeStruct((SEQ, MLA_HEADS * V_HEAD), BF16),
        compiler_params=_cparams(("arbitrary", "arbitrary")),
        name="attention",
    )(q, k_lat, k_ctx, v_lat, v_ctx)


SC_HALO = 16


def _short_conv_kernel(x_ref, prev_ref, next_ref, w_ref, b_ref, o_ref):
    i = pl.program_id(0)
    tm = x_ref.shape[0]
    x = x_ref[...].astype(F32)
    prev_row = prev_ref[SC_HALO - 1:SC_HALO, :].astype(F32) * (i > 0).astype(F32)
    next_row = next_ref[0:1, :].astype(F32) * (i < pl.num_programs(0) - 1).astype(F32)
    rows = lax.broadcasted_iota(jnp.int32, x.shape, 0)
    up = jnp.where(rows == 0, prev_row, pltpu.roll(x, 1, 0))
    dn = jnp.where(rows == tm - 1, next_row, pltpu.roll(x, tm - 1, 0))
    o_ref[...] = up * w_ref[0:1, :] + x * w_ref[1:2, :] + dn * w_ref[2:3, :] + b_ref[...]


def _short_conv(hg, conv_w, conv_b, *, col0):
    tm, tc = 512, 1024
    nb = 3 * HY_WIDTH // tc
    cb0 = col0 // tc
    hb = tm // SC_HALO
    last = SEQ // SC_HALO - 1
    w8 = jnp.zeros((8, 3 * HY_WIDTH), F32).at[:3].set(conv_w)
    return pl.pallas_call(
        _short_conv_kernel,
        grid=(SEQ // tm, nb),
        in_specs=[
            pl.BlockSpec((tm, tc), lambda i, j: (i, j + cb0)),
            pl.BlockSpec((SC_HALO, tc), lambda i, j: (jnp.maximum(i * hb - 1, 0), j + cb0)),
            pl.BlockSpec((SC_HALO, tc), lambda i, j: (jnp.minimum((i + 1) * hb, last), j + cb0)),
            pl.BlockSpec((8, tc), lambda i, j: (0, j)),
            pl.BlockSpec((1, tc), lambda i, j: (0, j)),
        ],
        out_specs=pl.BlockSpec((tm, tc), lambda i, j: (i, j)),
        out_shape=jax.ShapeDtypeStruct((SEQ, 3 * HY_WIDTH), F32),
        compiler_params=_cparams(("arbitrary", "arbitrary")),
        name="short_conv",
    )(hg, hg, hg, w8, conv_b.reshape(1, -1))


HY_ZW = 128


def _dot_split3(a, b):
    a_hi = a.astype(BF16)
    b_hi = b.astype(BF16)
    a_lo = (a - a_hi.astype(F32)).astype(BF16)
    b_lo = (b - b_hi.astype(F32)).astype(BF16)
    dot = functools.partial(jnp.dot, preferred_element_type=F32)
    return dot(a_hi, b_hi) + (dot(a_hi, b_lo) + dot(a_lo, b_hi))


def _filter_kernel(z_ref, w1_ref, b1_ref, w2_ref, b2_ref, w3_ref, fr_ref, dec_ref,
                   h_ref, ss_ref):
    hi = lax.Precision.HIGHEST
    z = z_ref[...]
    fr = fr_ref[...]
    h = jnp.sin(fr * (jnp.dot(z, w1_ref[...], precision=hi, preferred_element_type=F32)
                      + b1_ref[...]))
    h = jnp.sin(fr * (jnp.dot(h, w2_ref[...], precision=hi, preferred_element_type=F32)
                      + b2_ref[...]))
    h = _dot_split3(h, w3_ref[...])
    dist = z[:, HY_EMB:HY_EMB + 1]
    h = h * (jnp.exp(-dist * dec_ref[...]) + HY_MOD_SHIFT)
    h_ref[...] = h

    @pl.when(pl.program_id(0) == 0)
    def _():
        ss_ref[...] = jnp.zeros_like(ss_ref)

    ss_ref[...] += jnp.sum(h * h, axis=0, keepdims=True)


def _hyena_filters(z_p, w1p, b1, w2, b2, w3, freq, decay2):
    tl = 512
    nw = HY_ORDER * HY_WIDTH
    const = lambda i: (0, 0)
    return pl.pallas_call(
        _filter_kernel,
        grid=(SEQ // tl,),
        in_specs=[
            pl.BlockSpec((tl, HY_ZW), lambda i: (i, 0)),
            pl.BlockSpec((HY_ZW, HY_FILTER_W), const),
            pl.BlockSpec((1, HY_FILTER_W), const),
            pl.BlockSpec((HY_FILTER_W, HY_FILTER_W), const),
            pl.BlockSpec((1, HY_FILTER_W), const),
            pl.BlockSpec((HY_FILTER_W, nw), const),
            pl.BlockSpec((1, HY_FILTER_W), const),
            pl.BlockSpec((1, nw), const),
        ],
        out_specs=[pl.BlockSpec((tl, nw), lambda i: (i, 0)), pl.BlockSpec((1, nw), const)],
        out_shape=[jax.ShapeDtypeStruct((SEQ, nw), F32), jax.ShapeDtypeStruct((1, nw), F32)],
        compiler_params=_cparams(("arbitrary",)),
        name="hyena_filters",
    )(z_p, w1p, b1.reshape(1, -1), w2, b2.reshape(1, -1), w3, freq.reshape(1, -1), decay2)


FFT_LANES = 128
FFT_G = 32
FFT_STEPS = FFT_N2 // FFT_G
FFT_ROWS = 2 * FFT_N1
Y_PITCH = FFT_ROWS + 8
T_PITCH = FFT_N2 + 8


def _pad_rows_in(src_ref, dst_scr, scale=None):
    for n1 in range(FFT_HALF):
        v = src_ref[n1 * FFT_N2:(n1 + 1) * FFT_N2, :]
        dst_scr[n1 * T_PITCH:n1 * T_PITCH + FFT_N2, :] = v if scale is None else v * scale


def _fft_stage_a(t, src_scr, ma_ref, y_scr):
    for g in range(FFT_G):
        j = t * FFT_G + g
        xj = src_scr[pl.ds(j, FFT_HALF, stride=T_PITCH), :]
        a = jnp.dot(ma_ref[g], xj.astype(BF16), preferred_element_type=F32)
        y_scr[pl.ds(pl.multiple_of(j * Y_PITCH, 8), FFT_ROWS), :] = a


def _fft_load_pair(tb, g, y_scr):
    cols = []
    for k1 in (tb * FFT_G + g, tb * FFT_G + g + 1):
        re = y_scr[pl.ds(k1, FFT_N2, stride=Y_PITCH), :]
        im = y_scr[pl.ds(FFT_N1 + k1, FFT_N2, stride=Y_PITCH), :]
        cols.append(jnp.concatenate([re, im], axis=0))
    return jnp.concatenate(cols, axis=1).astype(BF16)


def _hyena_spectrum_kernel(h_ref, ss_ref, ma_ref, g2_ref, o_ref, y_scr, h_scr):
    t = pl.program_id(1)

    @pl.when(t == 0)
    def _():
        _pad_rows_in(h_ref, h_scr, lax.rsqrt(ss_ref[...] + NORM_EPS))

    @pl.when(t < FFT_STEPS)
    def _():
        _fft_stage_a(t, h_scr, ma_ref, y_scr)

    @pl.when(t >= FFT_STEPS)
    def _():
        tb = t - FFT_STEPS
        for g in range(0, FFT_G, 2):
            z = jnp.dot(g2_ref[...], _fft_load_pair(tb, g, y_scr), preferred_element_type=F32)
            o_ref[g] = z[:, :FFT_LANES].astype(BF16)
            o_ref[g + 1] = z[:, FFT_LANES:].astype(BF16)


def _hyena_spectrum(filt, sumsq, m_a, g2):
    nb = filt.shape[1] // FFT_LANES
    last = FFT_STEPS - 1
    return pl.pallas_call(
        _hyena_spectrum_kernel,
        grid=(nb, 2 * FFT_STEPS),
        in_specs=[
            pl.BlockSpec((SEQ, FFT_LANES), lambda c, t: (0, c)),
            pl.BlockSpec((1, FFT_LANES), lambda c, t: (0, c)),
            pl.BlockSpec((FFT_G, FFT_ROWS, FFT_HALF), lambda c, t: (jnp.minimum(t, last), 0, 0)),
            pl.BlockSpec((FFT_ROWS, FFT_ROWS), lambda c, t: (0, 0)),
        ],
        out_specs=pl.BlockSpec((None, FFT_G, FFT_ROWS, FFT_LANES),
                               lambda c, t: (c, jnp.maximum(t - FFT_STEPS, 0), 0, 0)),
        out_shape=jax.ShapeDtypeStruct((nb, FFT_N1, FFT_ROWS, FFT_LANES), BF16),
        scratch_shapes=[pltpu.VMEM((FFT_N2 * Y_PITCH, FFT_LANES), F32),
                        pltpu.VMEM((FFT_HALF * T_PITCH, FFT_LANES), F32)],
        compiler_params=_cparams(("arbitrary", "arbitrary")),
        name="hyena_spectrum",
    )(filt, sumsq, m_a, g2)


def _long_conv_kernel(u_ref, x_ref, skip_ref, ma_ref, hf_ref, g2_ref, g2i_ref, mc_ref,
                      o_ref, y_scr, stage_scr, u_scr, x_scr):
    t = pl.program_id(1)

    @pl.when(t == 0)
    def _():
        _pad_rows_in(u_ref, u_scr)
        _pad_rows_in(x_ref, x_scr)

    @pl.when(t < FFT_STEPS)
    def _():
        _fft_stage_a(t, u_scr, ma_ref, y_scr)

    @pl.when((t >= FFT_STEPS) & (t < 2 * FFT_STEPS))
    def _():
        tb = t - FFT_STEPS
        for g in range(0, FFT_G, 2):
            stage_scr[g // 2] = _fft_load_pair(tb, g, y_scr)
        for g in range(0, FFT_G, 2):
            z = jnp.dot(g2_ref[...], stage_scr[g // 2], preferred_element_type=F32)
            hf = jnp.concatenate([hf_ref[g], hf_ref[g + 1]], axis=1).astype(F32)
            zr, zi = z[:FFT_N2], z[FFT_N2:]
            hr, hi = hf[:FFT_N2], hf[FFT_N2:]
            p = jnp.concatenate([zr * hr - zi * hi, zr * hi + zi * hr], axis=0).astype(BF16)
            b = jnp.dot(g2i_ref[...], p, preferred_element_type=F32)
            for half, k1 in enumerate((tb * FFT_G + g, tb * FFT_G + g + 1)):
                lanes = slice(half * FFT_LANES, (half + 1) * FFT_LANES)
                y_scr[pl.ds(k1, FFT_N2, stride=Y_PITCH), :] = b[:FFT_N2, lanes]
                y_scr[pl.ds(FFT_N1 + k1, FFT_N2, stride=Y_PITCH), :] = b[FFT_N2:, lanes]

    @pl.when(t >= 2 * FFT_STEPS)
    def _():
        tc = t - 2 * FFT_STEPS
        for g in range(FFT_G):
            m2 = tc * FFT_G + g
            bm = y_scr[pl.ds(pl.multiple_of(m2 * Y_PITCH, 8), FFT_ROWS), :]
            y = jnp.dot(mc_ref[g], bm.astype(BF16), preferred_element_type=F32)
            rows = pl.ds(m2, FFT_HALF, stride=T_PITCH)
            u_scr[rows, :] = x_scr[rows, :] * (y + skip_ref[...] * u_scr[rows, :])

    @pl.when(t == 3 * FFT_STEPS - 1)
    def _():
        for n1 in range(FFT_HALF):
            o_ref[n1 * FFT_N2:(n1 + 1) * FFT_N2, :] = u_scr[n1 * T_PITCH:n1 * T_PITCH + FFT_N2, :]


def _long_conv_gate(u, ucol, x, xcol, skip, hf, hf0, m_a, g2, g2i, m_c):
    nb = HY_WIDTH // FFT_LANES
    ub, xb = ucol // FFT_LANES, xcol // FFT_LANES
    last = FFT_STEPS - 1
    step = lambda t, phase: jnp.clip(t - phase * FFT_STEPS, 0, last)
    return pl.pallas_call(
        _long_conv_kernel,
        grid=(nb, 3 * FFT_STEPS),
        in_specs=[
            pl.BlockSpec((SEQ, FFT_LANES), lambda c, t: (0, c + ub)),
            pl.BlockSpec((SEQ, FFT_LANES), lambda c, t: (0, c + xb)),
            pl.BlockSpec((1, FFT_LANES), lambda c, t: (0, c)),
            pl.BlockSpec((FFT_G, FFT_ROWS, FFT_HALF), lambda c, t: (step(t, 0), 0, 0)),
            pl.BlockSpec((None, FFT_G, FFT_ROWS, FFT_LANES),
                         lambda c, t: (c + hf0, step(t, 1), 0, 0)),
            pl.BlockSpec((FFT_ROWS, FFT_ROWS), lambda c, t: (0, 0)),
            pl.BlockSpec((FFT_ROWS, FFT_ROWS), lambda c, t: (0, 0)),
            pl.BlockSpec((FFT_G, FFT_HALF, FFT_ROWS), lambda c, t: (step(t, 2), 0, 0)),
        ],
        out_specs=pl.BlockSpec((SEQ, FFT_LANES), lambda c, t: (0, c)),
        out_shape=jax.ShapeDtypeStruct((SEQ, HY_WIDTH), F32),
        scratch_shapes=[pltpu.VMEM((FFT_N2 * Y_PITCH, FFT_LANES), F32),
                        pltpu.VMEM((FFT_G // 2, FFT_ROWS, 2 * FFT_LANES), BF16),
                        pltpu.VMEM((FFT_HALF * T_PITCH, FFT_LANES), F32),
                        pltpu.VMEM((FFT_HALF * T_PITCH, FFT_LANES), F32)],
        compiler_params=_cparams(("arbitrary", "arbitrary")),
        name="long_conv_gate",
    )(u, x, skip.reshape(1, HY_WIDTH), m_a, hf, g2, g2i, m_c)


def _merge_kernel(a_ref, hy_ref, ga_ref, gh_ref, wa_ref, wh_ref, o_ref):
    ya = jnp.dot(a_ref[...], wa_ref[...], preferred_element_type=F32)
    yh = jnp.dot(hy_ref[...].astype(BF16), wh_ref[...], preferred_element_type=F32)
    o_ref[...] = (ga_ref[...].astype(F32) * ya + gh_ref[...].astype(F32) * yh).astype(BF16)


def _merge(attn, hy, hg, gate_col0, w_attn_o, w_hy_o):
    tm, tn = 1024, 1024
    ga0 = gate_col0 // tn
    gh0 = (gate_col0 + D_MODEL) // tn
    row = lambda i, j: (i, 0)
    return pl.pallas_call(
        _merge_kernel,
        grid=(SEQ // tm, D_MODEL // tn),
        in_specs=[
            pl.BlockSpec((tm, MLA_HEADS * V_HEAD), row),
            pl.BlockSpec((tm, HY_WIDTH), row),
            pl.BlockSpec((tm, tn), lambda i, j: (i, j + ga0)),
            pl.BlockSpec((tm, tn), lambda i, j: (i, j + gh0)),
            pl.BlockSpec((MLA_HEADS * V_HEAD, tn), lambda i, j: (0, j)),
            pl.BlockSpec((HY_WIDTH, tn), lambda i, j: (0, j)),
        ],
        out_specs=pl.BlockSpec((tm, tn), lambda i, j: (i, j)),
        out_shape=jax.ShapeDtypeStruct((SEQ, D_MODEL), BF16),
        compiler_params=_cparams(("arbitrary", "arbitrary")),
        name="merge_branches",
    )(attn, hy, hg, hg, w_attn_o, w_hy_o)


def _out_proj_kernel(y_ref, w_ref, s_ref, mod_ref, o_ref):
    xm = jnp.dot(y_ref[...], w_ref[...], preferred_element_type=F32)
    o_ref[...] = s_ref[...] + mod_ref[0:1, :] * xm


def _out_proj(y, w_out, s, mod_gate):
    tm = 512
    row = lambda i: (i, 0)
    const = lambda i: (0, 0)
    return pl.pallas_call(
        _out_proj_kernel,
        grid=(SEQ // tm,),
        in_specs=[
            pl.BlockSpec((tm, D_MODEL), row),
            pl.BlockSpec((D_MODEL, D_MODEL), const),
            pl.BlockSpec((tm, D_MODEL), row),
            pl.BlockSpec((8, D_MODEL), const),
        ],
        out_specs=pl.BlockSpec((tm, D_MODEL), row),
        out_shape=jax.ShapeDtypeStruct((SEQ, D_MODEL), F32),
        compiler_params=_cparams(("arbitrary",)),
        name="out_proj",
    )(y, w_out, s, mod_gate)


def _rope_tables():
    t = np.arange(SEQ)
    pos = np.stack([t // GRID_W, t % GRID_W], axis=1).astype(np.float64)
    inv_freq = ROPE_THETA ** (-np.arange(0, ROPE_AXIS, 2, dtype=np.float64) / ROPE_AXIS)
    i = np.arange(QK_ROPE)
    ang = pos[:, i // ROPE_AXIS] * inv_freq[i % (ROPE_AXIS // 2)][None, :]
    cos, sin = np.cos(ang), np.sin(ang)
    tab_q = np.concatenate([np.ones((SEQ, QK_NOPE)), cos, sin], axis=1) * (ATTN_SCALE * math.log2(math.e))
    tab_k = np.concatenate([cos, sin], axis=1)
    tab_k_ctx = np.concatenate([np.ones((CTX_LEN, QK_ROPE)), np.zeros((CTX_LEN, QK_ROPE))], axis=1)
    return (jnp.asarray(tab_q, F32), jnp.asarray(tab_k, F32), jnp.asarray(tab_k_ctx, F32))


def _rope_swap():
    i = np.arange(QK_ROPE)
    first_half = (i % ROPE_AXIS) < ROPE_AXIS // 2
    partner = np.where(first_half, i + ROPE_AXIS // 2, i - ROPE_AXIS // 2)
    sign = np.where(first_half, -1.0, 1.0)
    return partner, sign


def _filter_features():
    pos = np.arange(SEQ, dtype=np.float64)[:, None]
    t01 = pos / (SEQ - 1)
    bands = np.linspace(1e-4, HY_BANDS - 1, HY_BANDS)[None, :]
    ang = bands * (2.0 * math.pi / SEQ) * pos
    dist = np.abs(pos - (SEQ // 2)) / (SEQ / 2.0)
    z = np.concatenate([t01, np.cos(ang), -np.sin(ang), dist], axis=1)
    z = np.pad(z, ((0, 0), (0, HY_ZW - z.shape[1])))
    return jnp.asarray(z, F32)


def _dft_tables():
    n1 = np.arange(FFT_HALF)
    k1 = np.arange(FFT_N1)
    n2 = np.arange(FFT_N2)
    n = FFT_N2 * n1[None, None, :] + n2[:, None, None]
    ph = (k1[None, :, None] * n) % FFT_N
    th = 2.0 * math.pi * ph / FFT_N
    m_a = np.concatenate([np.cos(th), -np.sin(th)], axis=1)
    ph2 = (n2[:, None] * n2[None, :]) % FFT_N2
    th2 = 2.0 * math.pi * ph2 / FFT_N2
    c, s = np.cos(th2), np.sin(th2)
    g2 = np.block([[c, s], [-s, c]])
    g2i = np.block([[c, -s], [s, c]])
    n = FFT_N2 * (n1[None, :, None] + FFT_HALF // 2) + n2[:, None, None]
    ph = (k1[None, None, :] * n) % FFT_N
    th = 2.0 * math.pi * ph / FFT_N
    m_c = np.concatenate([np.cos(th), -np.sin(th)], axis=2) / FFT_N
    return tuple(jnp.asarray(t, F32).astype(BF16) for t in (m_a, g2, g2i, m_c))


def _rows8(m, lo, hi):
    return jnp.zeros((8, D_MODEL), F32).at[:hi - lo].set(m[lo:hi])


def kernel(x, c, ctx, c_ctx, w_mod, b_mod, g_ffn1, w13_ffn1, w2_ffn1, g_mix, w_in, g_q, w_uq,
           g_kv, w_ukv, w_attn_o, hy_conv_w, hy_conv_b, hy_w1, hy_b1, hy_w2, hy_b2, hy_w3,
           hy_freq, hy_skip, w_hy_o, w_out, g_ffn2, w13_ffn2, w2_ffn2, g_final):
    xs = x[0]
    cs = ctx[0]
    li = 0

    c8 = jnp.zeros((8, D_MODEL), F32).at[0].set(c[0]).at[1].set(c_ctx)
    mod = _modulation(c8, w_mod[li], b_mod[li])
    mx = mod[0].reshape(N_MOD, D_MODEL)
    mc = mod[1].reshape(N_MOD, D_MODEL)

    w13 = w13_ffn1[li]
    w2 = w2_ffn1[li]
    x1 = _half_ffn(xs, _rows8(mx, 0, 3), g_ffn1[li], w13, w2, g_final, tm=FFN_TM, final_norm=False)
    c1 = _half_ffn(cs, _rows8(mc, 0, 3), g_ffn1[li], w13, w2, g_final, tm=CTX_LEN,
                   final_norm=False)

    hx = _prenorm(x1, _rows8(mx, 3, 5), g_mix[li], tm=512)
    hc = _prenorm(c1, _rows8(mc, 3, 5), g_mix[li], tm=CTX_LEN)

    win = w_in[li]
    partner, sign = _rope_swap()
    w_kr = win[:, KV_END:KR_END]
    w_a = jnp.concatenate([win[:, :KV_END], w_kr, w_kr[:, partner] * sign], axis=1).astype(BF16)
    w_hg = win[:, KR_END:].astype(BF16)
    wq = w_uq[li].reshape(Q_LORA, MLA_HEADS, QK_NOPE + QK_ROPE)
    wq_r = wq[:, :, QK_NOPE:]
    wq_p = jnp.concatenate([wq[:, :, :QK_NOPE], wq_r, wq_r[:, :, partner] * sign], axis=2)
    wq_p = wq_p.reshape(Q_LORA, MLA_HEADS * HEAD_PAD).astype(BF16)
    wkv = w_ukv[li].reshape(KV_LORA, MLA_HEADS, QK_NOPE + V_HEAD)
    wkv_p = jnp.concatenate([wkv[:, :, :QK_NOPE].reshape(KV_LORA, -1),
                             wkv[:, :, QK_NOPE:].reshape(KV_LORA, -1)], axis=1).astype(BF16)

    tab_q, tab_k, tab_k_ctx = _rope_tables()
    q, k_lat, v_lat = _proj_attn(hx, w_a, g_q[li], g_kv[li], wq_p, wkv_p, tab_q, tab_k,
                                 tm=512, with_q=True)
    k_ctx, v_ctx = _proj_attn(hc, w_a, g_q[li], g_kv[li], wq_p, wkv_p, tab_q[:CTX_LEN],
                              tab_k_ctx, tm=CTX_LEN, with_q=False)
    attn = _attention(q, k_lat, k_ctx, v_lat, v_ctx)

    hg = _proj_hg(hx, w_hg)
    u3 = _short_conv(hg, hy_conv_w[li], hy_conv_b[li], col0=0)
    w1p = jnp.zeros((HY_ZW, HY_FILTER_W), F32).at[:HY_EMB].set(hy_w1[li])
    decay = np.abs(np.linspace(HY_MIN_DECAY, HY_MAX_DECAY, HY_WIDTH))
    decay2 = jnp.asarray(np.tile(decay, HY_ORDER)[None, :], F32)
    filt, sumsq = _hyena_filters(_filter_features(), w1p, hy_b1[li], hy_w2[li], hy_b2[li],
                                 hy_w3[li], hy_freq[li], decay2)
    m_a, g2, g2i, m_c = _dft_tables()
    hf = _hyena_spectrum(filt, sumsq, m_a, g2)
    nb = HY_WIDTH // FFT_LANES
    z = _long_conv_gate(u3, 0, u3, HY_WIDTH, hy_skip[li][0], hf, 0, m_a, g2, g2i, m_c)
    hy = _long_conv_gate(z, 0, u3, 2 * HY_WIDTH, hy_skip[li][1], hf, nb, m_a, g2, g2i, m_c)

    y = _merge(attn, hy, hg, 3 * HY_WIDTH, w_attn_o[li].astype(BF16), w_hy_o[li].astype(BF16))
    x2 = _out_proj(y, w_out[li].astype(BF16), x1, _rows8(mx, 5, 6))

    out = _half_ffn(x2, _rows8(mx, 6, 9), g_ffn2[li], w13_ffn2[li], w2_ffn2[li], g_final,
                    tm=FFN_TM, final_norm=True)
    return out[None]
```

```python
import functools
import math

import numpy as np
import jax
import jax.numpy as jnp
from jax import lax
from jax.experimental import pallas as pl
from jax.experimental.pallas import tpu as pltpu

F32 = jnp.float32
BF16 = jnp.bfloat16

D_MODEL = 2048
SEQ = 8192
GRID_W = 64
CTX_LEN = 256
N_MOD = 9
D_FF = 5632
NORM_EPS = 1e-6

MLA_HEADS = 16
Q_LORA = 512
KV_LORA = 512
QK_NOPE = 128
QK_ROPE = 64
V_HEAD = 128
ROPE_AXIS = QK_ROPE // 2
ROPE_THETA = 10000.0
ATTN_SCALE = (QK_NOPE + QK_ROPE) ** -0.5
HEAD_PAD = 256

HY_WIDTH = 1024
HY_ORDER = 2
HY_EMB = 33
HY_BANDS = (HY_EMB - 1) // 2
HY_FILTER_W = 64
HY_TARGET = 1e-2
HY_FAST_DECAY = 0.3
HY_SLOW_DECAY = 1.5
HY_MAX_DECAY = math.log(HY_TARGET) / HY_FAST_DECAY
HY_MIN_DECAY = math.log(HY_TARGET) / HY_SLOW_DECAY
HY_MOD_SHIFT = 0.05

Q_END = Q_LORA
KV_END = Q_END + KV_LORA
KR_END = KV_END + QK_ROPE
HY_END = KR_END + 3 * HY_WIDTH

FFT_N = 2 * SEQ
FFT_N1 = 128
FFT_N2 = 128
FFT_HALF = SEQ // FFT_N2

V7X_VMEM_LIMIT = 60 * 1024 * 1024
LANES = 128


def _cparams(sem, vmem=V7X_VMEM_LIMIT):
    return pltpu.CompilerParams(dimension_semantics=sem, vmem_limit_bytes=vmem)


def _sigmoid(x):
    return 1.0 / (1.0 + jnp.exp(-x))


def _rms(x, g):
    var = jnp.mean(x * x, axis=-1, keepdims=True)
    return x * lax.rsqrt(var + NORM_EPS) * g


def _mod_kernel(c_ref, w_ref, b_ref, o_ref):
    c = c_ref[...]
    a = c * _sigmoid(c)
    o_ref[...] = jnp.dot(a, w_ref[...], preferred_element_type=F32) + b_ref[...]


def _modulation(c8, w_mod, b_mod):
    n = w_mod.shape[1]
    tn = 1024
    return pl.pallas_call(
        _mod_kernel,
        grid=(n // tn,),
        in_specs=[
            pl.BlockSpec((8, D_MODEL), lambda j: (0, 0)),
            pl.BlockSpec((D_MODEL, tn), lambda j: (0, j)),
            pl.BlockSpec((1, tn), lambda j: (0, j)),
        ],
        out_specs=pl.BlockSpec((8, tn), lambda j: (0, j)),
        out_shape=jax.ShapeDtypeStruct((8, n), F32),
        compiler_params=_cparams(("arbitrary",)),
        name="modulation",
    )(c8, w_mod, b_mod.reshape(1, n))


def _ffn_kernel(s_ref, mod_ref, g_ref, w1_ref, w3_ref, w2_ref, gf_ref, o_ref, h_scr,
                *, final_norm):
    f = pl.program_id(1)

    @pl.when(f == 0)
    def _():
        h = _rms(s_ref[...], g_ref[...])
        h = h * (1.0 + mod_ref[1:2, :]) + mod_ref[0:1, :]
        h_scr[...] = h.astype(BF16)
        o_ref[...] = jnp.zeros_like(o_ref)

    h = h_scr[...]
    a = jnp.dot(h, w1_ref[...].astype(BF16), preferred_element_type=F32)
    b = jnp.dot(h, w3_ref[...].astype(BF16), preferred_element_type=F32)
    act = (a * _sigmoid(a) * b).astype(BF16)
    o_ref[...] += jnp.dot(act, w2_ref[...].astype(BF16), preferred_element_type=F32)

    @pl.when(f == pl.num_programs(1) - 1)
    def _():
        out = s_ref[...] + 0.5 * mod_ref[2:3, :] * o_ref[...]
        if final_norm:
            out = _rms(out, gf_ref[...])
        o_ref[...] = out


FFN_TM = 1024


def _half_ffn(s, mod3, g, w13, w2, g_final, *, tm, final_norm):
    rows = s.shape[0]
    tf = 256
    nf = D_FF // tf
    return pl.pallas_call(
        functools.partial(_ffn_kernel, final_norm=final_norm),
        grid=(rows // tm, nf),
        in_specs=[
            pl.BlockSpec((tm, D_MODEL), lambda i, f: (i, 0)),
            pl.BlockSpec((8, D_MODEL), lambda i, f: (0, 0)),
            pl.BlockSpec((1, D_MODEL), lambda i, f: (0, 0)),
            pl.BlockSpec((D_MODEL, tf), lambda i, f: (0, f)),
            pl.BlockSpec((D_MODEL, tf), lambda i, f: (0, f + nf)),
            pl.BlockSpec((tf, D_MODEL), lambda i, f: (f, 0)),
            pl.BlockSpec((1, D_MODEL), lambda i, f: (0, 0)),
        ],
        out_specs=pl.BlockSpec((tm, D_MODEL), lambda i, f: (i, 0)),
        out_shape=jax.ShapeDtypeStruct((rows, D_MODEL), F32),
        scratch_shapes=[pltpu.VMEM((tm, D_MODEL), BF16)],
        compiler_params=_cparams(("arbitrary", "arbitrary")),
        name="half_ffn",
    )(s, mod3, g.reshape(1, D_MODEL), w13, w13, w2, g_final.reshape(1, D_MODEL))


def _prenorm_kernel(s_ref, mod_ref, g_ref, o_ref):
    h = _rms(s_ref[...], g_ref[...])
    o_ref[...] = (h * (1.0 + mod_ref[1:2, :]) + mod_ref[0:1, :]).astype(BF16)


def _prenorm(s, mod3, g, *, tm):
    rows = s.shape[0]
    return pl.pallas_call(
        _prenorm_kernel,
        grid=(rows // tm,),
        in_specs=[
            pl.BlockSpec((tm, D_MODEL), lambda i: (i, 0)),
            pl.BlockSpec((8, D_MODEL), lambda i: (0, 0)),
            pl.BlockSpec((1, D_MODEL), lambda i: (0, 0)),
        ],
        out_specs=pl.BlockSpec((tm, D_MODEL), lambda i: (i, 0)),
        out_shape=jax.ShapeDtypeStruct((rows, D_MODEL), BF16),
        compiler_params=_cparams(("arbitrary",)),
        name="prenorm",
    )(s, mod3, g.reshape(1, D_MODEL))


def _proj_hg_kernel(h_ref, w_ref, o_ref, *, n_plain_tiles):
    j = pl.program_id(1)
    acc = jnp.dot(h_ref[...], w_ref[...], preferred_element_type=F32)

    @pl.when(j < n_plain_tiles)
    def _():
        o_ref[...] = acc.astype(BF16)

    @pl.when(j >= n_plain_tiles)
    def _():
        o_ref[...] = _sigmoid(acc).astype(BF16)


def _proj_hg(h, w_hg):
    rows, n = h.shape[0], w_hg.shape[1]
    tm, tn = 1024, 1024
    return pl.pallas_call(
        functools.partial(_proj_hg_kernel, n_plain_tiles=3 * HY_WIDTH // tn),
        grid=(rows // tm, n // tn),
        in_specs=[
            pl.BlockSpec((tm, D_MODEL), lambda i, j: (i, 0)),
            pl.BlockSpec((D_MODEL, tn), lambda i, j: (0, j)),
        ],
        out_specs=pl.BlockSpec((tm, tn), lambda i, j: (i, j)),
        out_shape=jax.ShapeDtypeStruct((rows, n), BF16),
        compiler_params=_cparams(("arbitrary", "arbitrary")),
        name="proj_gate_hyena",
    )(h, w_hg)


def _proj_attn_kernel(h_ref, wa_ref, gq_ref, gkv_ref, wuq_ref, wukv_ref, tq_ref, tk_ref,
                      *out_refs, with_q):
    if with_q:
        q_ref, k_ref, v_ref = out_refs
    else:
        k_ref, v_ref = out_refs
    p = jnp.dot(h_ref[...], wa_ref[...], preferred_element_type=F32)
    kvn = _rms(p[:, Q_LORA:Q_LORA + KV_LORA], gkv_ref[...]).astype(BF16)
    t = p[:, KV_END:KV_END + 2 * QK_ROPE] * tk_ref[...]
    krot = (t + pltpu.roll(t, QK_ROPE, 1)).astype(BF16)
    kn = jnp.dot(kvn, wukv_ref[:, :MLA_HEADS * QK_NOPE], preferred_element_type=F32)
    for hd in range(MLA_HEADS):
        k_ref[:, hd * HEAD_PAD:hd * HEAD_PAD + QK_NOPE] = (
            kn[:, hd * QK_NOPE:(hd + 1) * QK_NOPE].astype(BF16))
        k_ref[:, hd * HEAD_PAD + QK_NOPE:(hd + 1) * HEAD_PAD] = krot
    vv = jnp.dot(kvn, wukv_ref[:, MLA_HEADS * QK_NOPE:], preferred_element_type=F32)
    lane = lax.broadcasted_iota(jnp.int32, (h_ref.shape[0], HEAD_PAD - V_HEAD), 1)
    unit = jnp.where(lane == 0, 1.0, 0.0).astype(BF16)
    for hd in range(MLA_HEADS):
        v_ref[:, hd * HEAD_PAD:hd * HEAD_PAD + V_HEAD] = (
            vv[:, hd * V_HEAD:(hd + 1) * V_HEAD].astype(BF16))
        v_ref[:, hd * HEAD_PAD + V_HEAD:(hd + 1) * HEAD_PAD] = unit
    if with_q:
        qn = _rms(p[:, :Q_LORA], gq_ref[...]).astype(BF16)
        tq = tq_ref[...]
        for hd in range(MLA_HEADS):
            qh = jnp.dot(qn, wuq_ref[:, hd * HEAD_PAD:(hd + 1) * HEAD_PAD],
                         preferred_element_type=F32)
            q_ref[:, hd * HEAD_PAD:(hd + 1) * HEAD_PAD] = (qh * tq).astype(BF16)


def _proj_attn(h, w_a, g_q, g_kv, w_uq, w_ukv, tab_q, tab_k, *, tm, with_q):
    rows = h.shape[0]
    na = w_a.shape[1]
    hp = MLA_HEADS * HEAD_PAD
    hv = MLA_HEADS * HEAD_PAD
    const = lambda i: (0, 0)
    row = lambda i: (i, 0)
    out_shape = [jax.ShapeDtypeStruct((rows, hp), BF16), jax.ShapeDtypeStruct((rows, hv), BF16)]
    out_specs = [pl.BlockSpec((tm, hp), row), pl.BlockSpec((tm, hv), row)]
    if with_q:
        out_shape = [jax.ShapeDtypeStruct((rows, hp), BF16)] + out_shape
        out_specs = [pl.BlockSpec((tm, hp), row)] + out_specs
    return pl.pallas_call(
        functools.partial(_proj_attn_kernel, with_q=with_q),
        grid=(rows // tm,),
        in_specs=[
            pl.BlockSpec((tm, D_MODEL), row),
            pl.BlockSpec((D_MODEL, na), const),
            pl.BlockSpec((1, Q_LORA), const),
            pl.BlockSpec((1, KV_LORA), const),
            pl.BlockSpec((Q_LORA, hp), const),
            pl.BlockSpec((KV_LORA, MLA_HEADS * (QK_NOPE + V_HEAD)), const),
            pl.BlockSpec((tm, HEAD_PAD), row),
            pl.BlockSpec((tm, 2 * QK_ROPE), row),
        ],
        out_specs=out_specs,
        out_shape=out_shape,
        compiler_params=_cparams(("arbitrary",)),
        name="proj_attn",
    )(h, w_a, g_q.reshape(1, Q_LORA), g_kv.reshape(1, KV_LORA), w_uq, w_ukv, tab_q, tab_k)


ATT_TQ = 1024
ATT_SUB = 512
ATT_KC = 1024
_NT = (((1,), (1,)), ((), ()))


def _attn_kernel(q_ref, kl_ref, kc_ref, vl_ref, vc_ref, o_ref):
    n_sub = ATT_TQ // ATT_SUB
    qs, ms, accs = [], [], []
    for r in range(n_sub):
        q = q_ref[r * ATT_SUB:(r + 1) * ATT_SUB, :]
        s = lax.dot_general(q, kc_ref[...], _NT, preferred_element_type=F32)
        m = jnp.max(s, axis=-1, keepdims=True)
        p = jnp.exp2(s - m)
        qs.append(q)
        ms.append(m)
        accs.append(jnp.dot(p.astype(BF16), vc_ref[...], preferred_element_type=F32))
    for c in range(SEQ // ATT_KC):
        for r in range(n_sub):
            s = lax.dot_general(qs[r], kl_ref[c * ATT_KC:(c + 1) * ATT_KC, :], _NT,
                                preferred_element_type=F32)
            m_new = jnp.maximum(ms[r], jnp.max(s, axis=-1, keepdims=True))
            alpha = jnp.exp2(ms[r] - m_new)
            p = jnp.exp2(s - m_new)
            accs[r] = accs[r] * alpha + jnp.dot(
                p.astype(BF16), vl_ref[c * ATT_KC:(c + 1) * ATT_KC, :],
                preferred_element_type=F32)
            ms[r] = m_new
    for r in range(n_sub):
        acc = accs[r]
        o_ref[r * ATT_SUB:(r + 1) * ATT_SUB, :] = (
            acc[:, :V_HEAD] / acc[:, V_HEAD:V_HEAD + 1]).astype(BF16)


def _attention(q, k_lat, k_ctx, v_lat, v_ctx):
    return pl.pallas_call(
        _attn_kernel,
        grid=(MLA_HEADS, SEQ // ATT_TQ),
        in_specs=[
            pl.BlockSpec((ATT_TQ, HEAD_PAD), lambda h, i: (i, h)),
            pl.BlockSpec((SEQ, HEAD_PAD), lambda h, i: (0, h)),
            pl.BlockSpec((CTX_LEN, HEAD_PAD), lambda h, i: (0, h)),
            pl.BlockSpec((SEQ, HEAD_PAD), lambda h, i: (0, h)),
            pl.BlockSpec((CTX_LEN, HEAD_PAD), lambda h, i: (0, h)),
        ],
        out_specs=pl.BlockSpec((ATT_TQ, V_HEAD), lambda h, i: (i, h)),
        out_shape=jax.ShapeDtypeStruct((SEQ, MLA_HEADS * V_HEAD), BF16),
        compiler_params=_cparams(("arbitrary", "arbitrary")),
        name="attention",
    )(q, k_lat, k_ctx, v_lat, v_ctx)


SC_HALO = 16


def _short_conv_kernel(x_ref, prev_ref, next_ref, w_ref, b_ref, o_ref):
    i = pl.program_id(0)
    tm = x_ref.shape[0]
    x = x_ref[...].astype(F32)
    prev_row = prev_ref[SC_HALO - 1:SC_HALO, :].astype(F32) * (i > 0).astype(F32)
    next_row = next_ref[0:1, :].astype(F32) * (i < pl.num_programs(0) - 1).astype(F32)
    rows = lax.broadcasted_iota(jnp.int32, x.shape, 0)
    up = jnp.where(rows == 0, prev_row, pltpu.roll(x, 1, 0))
    dn = jnp.where(rows == tm - 1, next_row, pltpu.roll(x, tm - 1, 0))
    y = up * w_ref[0:1, :] + x * w_ref[1:2, :] + dn * w_ref[2:3, :] + b_ref[...]
    for blk in range(o_ref.shape[0]):
        o_ref[blk] = y[:, blk * LANES:(blk + 1) * LANES]


def _short_conv(hg, conv_w, conv_b, *, col0):
    tm, tc = 512, 1024
    nb = 3 * HY_WIDTH // tc
    cb0 = col0 // tc
    hb = tm // SC_HALO
    last = SEQ // SC_HALO - 1
    w8 = jnp.zeros((8, 3 * HY_WIDTH), F32).at[:3].set(conv_w)
    return pl.pallas_call(
        _short_conv_kernel,
        grid=(SEQ // tm, nb),
        in_specs=[
            pl.BlockSpec((tm, tc), lambda i, j: (i, j + cb0)),
            pl.BlockSpec((SC_HALO, tc), lambda i, j: (jnp.maximum(i * hb - 1, 0), j + cb0)),
            pl.BlockSpec((SC_HALO, tc), lambda i, j: (jnp.minimum((i + 1) * hb, last), j + cb0)),
            pl.BlockSpec((8, tc), lambda i, j: (0, j)),
            pl.BlockSpec((1, tc), lambda i, j: (0, j)),
        ],
        out_specs=pl.BlockSpec((tc // LANES, tm, LANES), lambda i, j: (j, i, 0)),
        out_shape=jax.ShapeDtypeStruct((3 * HY_WIDTH // LANES, SEQ, LANES), F32),
        compiler_params=_cparams(("arbitrary", "arbitrary")),
        name="short_conv",
    )(hg, hg, hg, w8, conv_b.reshape(1, -1))


HY_ZW = 128


def _dot_split3(a, b):
    a_hi = a.astype(BF16)
    b_hi = b.astype(BF16)
    a_lo = (a - a_hi.astype(F32)).astype(BF16)
    b_lo = (b - b_hi.astype(F32)).astype(BF16)
    dot = functools.partial(jnp.dot, preferred_element_type=F32)
    return dot(a_hi, b_hi) + (dot(a_hi, b_lo) + dot(a_lo, b_hi))


def _filter_kernel(z_ref, w1_ref, b1_ref, w2_ref, b2_ref, w3_ref, fr_ref, dec_ref,
                   h_ref, ss_ref):
    hi = lax.Precision.HIGHEST
    z = z_ref[...]
    fr = fr_ref[...]
    h = jnp.sin(fr * (jnp.dot(z, w1_ref[...], precision=hi, preferred_element_type=F32)
                      + b1_ref[...]))
    h = jnp.sin(fr * (jnp.dot(h, w2_ref[...], precision=hi, preferred_element_type=F32)
                      + b2_ref[...]))
    h = _dot_split3(h, w3_ref[...])
    dist = z[:, HY_EMB:HY_EMB + 1]
    h = h * (jnp.exp(-dist * dec_ref[...]) + HY_MOD_SHIFT)
    for blk in range(h_ref.shape[0]):
        h_ref[blk] = h[:, blk * LANES:(blk + 1) * LANES]

    @pl.when(pl.program_id(0) == 0)
    def _():
        ss_ref[...] = jnp.zeros_like(ss_ref)

    ss_ref[...] += jnp.sum(h * h, axis=0, keepdims=True)


def _hyena_filters(z_p, w1p, b1, w2, b2, w3, freq, decay2):
    tl = 512
    nw = HY_ORDER * HY_WIDTH
    const = lambda i: (0, 0)
    return pl.pallas_call(
        _filter_kernel,
        grid=(SEQ // tl,),
        in_specs=[
            pl.BlockSpec((tl, HY_ZW), lambda i: (i, 0)),
            pl.BlockSpec((HY_ZW, HY_FILTER_W), const),
            pl.BlockSpec((1, HY_FILTER_W), const),
            pl.BlockSpec((HY_FILTER_W, HY_FILTER_W), const),
            pl.BlockSpec((1, HY_FILTER_W), const),
            pl.BlockSpec((HY_FILTER_W, nw), const),
            pl.BlockSpec((1, HY_FILTER_W), const),
            pl.BlockSpec((1, nw), const),
        ],
        out_specs=[pl.BlockSpec((nw // LANES, tl, LANES), lambda i: (0, i, 0)),
                   pl.BlockSpec((1, nw), const)],
        out_shape=[jax.ShapeDtypeStruct((nw // LANES, SEQ, LANES), F32),
                   jax.ShapeDtypeStruct((1, nw), F32)],
        compiler_params=_cparams(("arbitrary",)),
        name="hyena_filters",
    )(z_p, w1p, b1.reshape(1, -1), w2, b2.reshape(1, -1), w3, freq.reshape(1, -1), decay2)


FFT_LANES = 128
FFT_G = 32
FFT_STEPS = FFT_N2 // FFT_G
FFT_ROWS = 2 * FFT_N1
Y_PITCH = FFT_ROWS + 8
T_PITCH = FFT_N2 + 8


def _pad_rows_in(src_ref, dst_scr, scale=None):
    for n1 in range(FFT_HALF):
        v = src_ref[n1 * FFT_N2:(n1 + 1) * FFT_N2, :]
        dst_scr[n1 * T_PITCH:n1 * T_PITCH + FFT_N2, :] = v if scale is None else v * scale


def _fft_stage_a(t, src_scr, ma_ref, y_scr):
    for g in range(FFT_G):
        j = t * FFT_G + g
        xj = src_scr[pl.ds(j, FFT_HALF, stride=T_PITCH), :]
        a = jnp.dot(ma_ref[g], xj.astype(BF16), preferred_element_type=F32)
        y_scr[pl.ds(pl.multiple_of(j * Y_PITCH, 8), FFT_ROWS), :] = a


def _fft_load_pair(tb, g, y_scr):
    cols = []
    for k1 in (tb * FFT_G + g, tb * FFT_G + g + 1):
        re = y_scr[pl.ds(k1, FFT_N2, stride=Y_PITCH), :]
        im = y_scr[pl.ds(FFT_N1 + k1, FFT_N2, stride=Y_PITCH), :]
        cols.append(jnp.concatenate([re, im], axis=0))
    return jnp.concatenate(cols, axis=1).astype(BF16)


def _hyena_spectrum_kernel(h_ref, ss_ref, ma_ref, g2_ref, o_ref, y_scr, h_scr):
    t = pl.program_id(1)

    @pl.when(t == 0)
    def _():
        _pad_rows_in(h_ref, h_scr, lax.rsqrt(ss_ref[...] + NORM_EPS))

    @pl.when(t < FFT_STEPS)
    def _():
        _fft_stage_a(t, h_scr, ma_ref, y_scr)

    @pl.when(t >= FFT_STEPS)
    def _():
        tb = t - FFT_STEPS
        for g in range(0, FFT_G, 2):
            z = jnp.dot(g2_ref[...], _fft_load_pair(tb, g, y_scr), preferred_element_type=F32)
            o_ref[g] = z[:, :FFT_LANES].astype(BF16)
            o_ref[g + 1] = z[:, FFT_LANES:].astype(BF16)


def _hyena_spectrum(filt, sumsq, m_a, g2):
    nb = filt.shape[0]
    last = FFT_STEPS - 1
    return pl.pallas_call(
        _hyena_spectrum_kernel,
        grid=(nb, 2 * FFT_STEPS),
        in_specs=[
            pl.BlockSpec((None, SEQ, FFT_LANES), lambda c, t: (c, 0, 0)),
            pl.BlockSpec((1, FFT_LANES), lambda c, t: (0, c)),
            pl.BlockSpec((FFT_G, FFT_ROWS, FFT_HALF), lambda c, t: (jnp.minimum(t, last), 0, 0)),
            pl.BlockSpec((FFT_ROWS, FFT_ROWS), lambda c, t: (0, 0)),
        ],
        out_specs=pl.BlockSpec((None, FFT_G, FFT_ROWS, FFT_LANES),
                               lambda c, t: (c, jnp.maximum(t - FFT_STEPS, 0), 0, 0)),
        out_shape=jax.ShapeDtypeStruct((nb, FFT_N1, FFT_ROWS, FFT_LANES), BF16),
        scratch_shapes=[pltpu.VMEM((FFT_N2 * Y_PITCH, FFT_LANES), F32),
                        pltpu.VMEM((FFT_HALF * T_PITCH, FFT_LANES), F32)],
        compiler_params=_cparams(("arbitrary", "arbitrary")),
        name="hyena_spectrum",
    )(filt, sumsq, m_a, g2)


def _long_conv_kernel(u_ref, x_ref, skip_ref, ma_ref, hf_ref, g2_ref, g2i_ref, mc_ref,
                      o_ref, y_scr, stage_scr, u_scr, x_scr):
    t = pl.program_id(1)

    @pl.when(t == 0)
    def _():
        _pad_rows_in(u_ref, u_scr)
        _pad_rows_in(x_ref, x_scr)

    @pl.when(t < FFT_STEPS)
    def _():
        _fft_stage_a(t, u_scr, ma_ref, y_scr)

    @pl.when((t >= FFT_STEPS) & (t < 2 * FFT_STEPS))
    def _():
        tb = t - FFT_STEPS
        for g in range(0, FFT_G, 2):
            stage_scr[g // 2] = _fft_load_pair(tb, g, y_scr)
        for g in range(0, FFT_G, 2):
            z = jnp.dot(g2_ref[...], stage_scr[g // 2], preferred_element_type=F32)
            hf = jnp.concatenate([hf_ref[g], hf_ref[g + 1]], axis=1).astype(F32)
            zr, zi = z[:FFT_N2], z[FFT_N2:]
            hr, hi = hf[:FFT_N2], hf[FFT_N2:]
            p = jnp.concatenate([zr * hr - zi * hi, zr * hi + zi * hr], axis=0).astype(BF16)
            b = jnp.dot(g2i_ref[...], p, preferred_element_type=F32)
            for half, k1 in enumerate((tb * FFT_G + g, tb * FFT_G + g + 1)):
                lanes = slice(half * FFT_LANES, (half + 1) * FFT_LANES)
                y_scr[pl.ds(k1, FFT_N2, stride=Y_PITCH), :] = b[:FFT_N2, lanes]
                y_scr[pl.ds(FFT_N1 + k1, FFT_N2, stride=Y_PITCH), :] = b[FFT_N2:, lanes]

    @pl.when(t >= 2 * FFT_STEPS)
    def _():
        tc = t - 2 * FFT_STEPS
        for g in range(FFT_G):
            m2 = tc * FFT_G + g
            bm = y_scr[pl.ds(pl.multiple_of(m2 * Y_PITCH, 8), FFT_ROWS), :]
            y = jnp.dot(mc_ref[g], bm.astype(BF16), preferred_element_type=F32)
            rows = pl.ds(m2, FFT_HALF, stride=T_PITCH)
            u_scr[rows, :] = x_scr[rows, :] * (y + skip_ref[...] * u_scr[rows, :])

    @pl.when(t == 3 * FFT_STEPS - 1)
    def _():
        for n1 in range(FFT_HALF):
            o_ref[n1 * FFT_N2:(n1 + 1) * FFT_N2, :] = u_scr[n1 * T_PITCH:n1 * T_PITCH + FFT_N2, :]


def _long_conv_gate(u, ub, x, xb, skip, hf, hf0, m_a, g2, g2i, m_c):
    nb = HY_WIDTH // FFT_LANES
    last = FFT_STEPS - 1
    step = lambda t, phase: jnp.clip(t - phase * FFT_STEPS, 0, last)
    return pl.pallas_call(
        _long_conv_kernel,
        grid=(nb, 3 * FFT_STEPS),
        in_specs=[
            pl.BlockSpec((None, SEQ, FFT_LANES), lambda c, t: (c + ub, 0, 0)),
            pl.BlockSpec((None, SEQ, FFT_LANES), lambda c, t: (c + xb, 0, 0)),
            pl.BlockSpec((1, FFT_LANES), lambda c, t: (0, c)),
            pl.BlockSpec((FFT_G, FFT_ROWS, FFT_HALF), lambda c, t: (step(t, 0), 0, 0)),
            pl.BlockSpec((None, FFT_G, FFT_ROWS, FFT_LANES),
                         lambda c, t: (c + hf0, step(t, 1), 0, 0)),
            pl.BlockSpec((FFT_ROWS, FFT_ROWS), lambda c, t: (0, 0)),
            pl.BlockSpec((FFT_ROWS, FFT_ROWS), lambda c, t: (0, 0)),
            pl.BlockSpec((FFT_G, FFT_HALF, FFT_ROWS), lambda c, t: (step(t, 2), 0, 0)),
        ],
        out_specs=pl.BlockSpec((None, SEQ, FFT_LANES), lambda c, t: (c, 0, 0)),
        out_shape=jax.ShapeDtypeStruct((nb, SEQ, FFT_LANES), F32),
        scratch_shapes=[pltpu.VMEM((FFT_N2 * Y_PITCH, FFT_LANES), F32),
                        pltpu.VMEM((FFT_G // 2, FFT_ROWS, 2 * FFT_LANES), BF16),
                        pltpu.VMEM((FFT_HALF * T_PITCH, FFT_LANES), F32),
                        pltpu.VMEM((FFT_HALF * T_PITCH, FFT_LANES), F32)],
        compiler_params=_cparams(("arbitrary", "arbitrary")),
        name="long_conv_gate",
    )(u, x, skip.reshape(1, HY_WIDTH), m_a, hf, g2, g2i, m_c)


def _merge_kernel(a_ref, hy_ref, ga_ref, gh_ref, wa_ref, wh_ref, o_ref):
    ya = jnp.dot(a_ref[...], wa_ref[...], preferred_element_type=F32)
    hy = jnp.concatenate([hy_ref[blk].astype(BF16) for blk in range(hy_ref.shape[0])], axis=1)
    yh = jnp.dot(hy, wh_ref[...], preferred_element_type=F32)
    o_ref[...] = (ga_ref[...].astype(F32) * ya + gh_ref[...].astype(F32) * yh).astype(BF16)


def _merge(attn, hy, hg, gate_col0, w_attn_o, w_hy_o):
    tm, tn = 1024, 1024
    ga0 = gate_col0 // tn
    gh0 = (gate_col0 + D_MODEL) // tn
    row = lambda i, j: (i, 0)
    return pl.pallas_call(
        _merge_kernel,
        grid=(SEQ // tm, D_MODEL // tn),
        in_specs=[
            pl.BlockSpec((tm, MLA_HEADS * V_HEAD), row),
            pl.BlockSpec((HY_WIDTH // LANES, tm, LANES), lambda i, j: (0, i, 0)),
            pl.BlockSpec((tm, tn), lambda i, j: (i, j + ga0)),
            pl.BlockSpec((tm, tn), lambda i, j: (i, j + gh0)),
            pl.BlockSpec((MLA_HEADS * V_HEAD, tn), lambda i, j: (0, j)),
            pl.BlockSpec((HY_WIDTH, tn), lambda i, j: (0, j)),
        ],
        out_specs=pl.BlockSpec((tm, tn), lambda i, j: (i, j)),
        out_shape=jax.ShapeDtypeStruct((SEQ, D_MODEL), BF16),
        compiler_params=_cparams(("arbitrary", "arbitrary")),
        name="merge_branches",
    )(attn, hy, hg, hg, w_attn_o, w_hy_o)


def _out_proj_kernel(y_ref, w_ref, s_ref, mod_ref, o_ref):
    xm = jnp.dot(y_ref[...], w_ref[...], preferred_element_type=F32)
    o_ref[...] = s_ref[...] + mod_ref[0:1, :] * xm


def _out_proj(y, w_out, s, mod_gate):
    tm = 512
    row = lambda i: (i, 0)
    const = lambda i: (0, 0)
    return pl.pallas_call(
        _out_proj_kernel,
        grid=(SEQ // tm,),
        in_specs=[
            pl.BlockSpec((tm, D_MODEL), row),
            pl.BlockSpec((D_MODEL, D_MODEL), const),
            pl.BlockSpec((tm, D_MODEL), row),
            pl.BlockSpec((8, D_MODEL), const),
        ],
        out_specs=pl.BlockSpec((tm, D_MODEL), row),
        out_shape=jax.ShapeDtypeStruct((SEQ, D_MODEL), F32),
        compiler_params=_cparams(("arbitrary",)),
        name="out_proj",
    )(y, w_out, s, mod_gate)


def _rope_tables():
    t = np.arange(SEQ)
    pos = np.stack([t // GRID_W, t % GRID_W], axis=1).astype(np.float64)
    inv_freq = ROPE_THETA ** (-np.arange(0, ROPE_AXIS, 2, dtype=np.float64) / ROPE_AXIS)
    i = np.arange(QK_ROPE)
    ang = pos[:, i // ROPE_AXIS] * inv_freq[i % (ROPE_AXIS // 2)][None, :]
    cos, sin = np.cos(ang), np.sin(ang)
    tab_q = np.concatenate([np.ones((SEQ, QK_NOPE)), cos, sin], axis=1) * (ATTN_SCALE * math.log2(math.e))
    tab_k = np.concatenate([cos, sin], axis=1)
    tab_k_ctx = np.concatenate([np.ones((CTX_LEN, QK_ROPE)), np.zeros((CTX_LEN, QK_ROPE))], axis=1)
    return (jnp.asarray(tab_q, F32), jnp.asarray(tab_k, F32), jnp.asarray(tab_k_ctx, F32))


def _rope_swap():
    i = np.arange(QK_ROPE)
    first_half = (i % ROPE_AXIS) < ROPE_AXIS // 2
    partner = np.where(first_half, i + ROPE_AXIS // 2, i - ROPE_AXIS // 2)
    sign = np.where(first_half, -1.0, 1.0)
    return partner, sign


def _filter_features():
    pos = np.arange(SEQ, dtype=np.float64)[:, None]
    t01 = pos / (SEQ - 1)
    bands = np.linspace(1e-4, HY_BANDS - 1, HY_BANDS)[None, :]
    ang = bands * (2.0 * math.pi / SEQ) * pos
    dist = np.abs(pos - (SEQ // 2)) / (SEQ / 2.0)
    z = np.concatenate([t01, np.cos(ang), -np.sin(ang), dist], axis=1)
    z = np.pad(z, ((0, 0), (0, HY_ZW - z.shape[1])))
    return jnp.asarray(z, F32)


def _dft_tables():
    n1 = np.arange(FFT_HALF)
    k1 = np.arange(FFT_N1)
    n2 = np.arange(FFT_N2)
    n = FFT_N2 * n1[None, None, :] + n2[:, None, None]
    ph = (k1[None, :, None] * n) % FFT_N
    th = 2.0 * math.pi * ph / FFT_N
    m_a = np.concatenate([np.cos(th), -np.sin(th)], axis=1)
    ph2 = (n2[:, None] * n2[None, :]) % FFT_N2
    th2 = 2.0 * math.pi * ph2 / FFT_N2
    c, s = np.cos(th2), np.sin(th2)
    g2 = np.block([[c, s], [-s, c]])
    g2i = np.block([[c, -s], [s, c]])
    n = FFT_N2 * (n1[None, :, None] + FFT_HALF // 2) + n2[:, None, None]
    ph = (k1[None, None, :] * n) % FFT_N
    th = 2.0 * math.pi * ph / FFT_N
    m_c = np.concatenate([np.cos(th), -np.sin(th)], axis=2) / FFT_N
    return tuple(jnp.asarray(t, F32).astype(BF16) for t in (m_a, g2, g2i, m_c))


def _rows8(m, lo, hi):
    return jnp.zeros((8, D_MODEL), F32).at[:hi - lo].set(m[lo:hi])


def kernel(x, c, ctx, c_ctx, w_mod, b_mod, g_ffn1, w13_ffn1, w2_ffn1, g_mix, w_in, g_q, w_uq,
           g_kv, w_ukv, w_attn_o, hy_conv_w, hy_conv_b, hy_w1, hy_b1, hy_w2, hy_b2, hy_w3,
           hy_freq, hy_skip, w_hy_o, w_out, g_ffn2, w13_ffn2, w2_ffn2, g_final):
    xs = x[0]
    cs = ctx[0]
    li = 0

    c8 = jnp.zeros((8, D_MODEL), F32).at[0].set(c[0]).at[1].set(c_ctx)
    mod = _modulation(c8, w_mod[li], b_mod[li])
    mx = mod[0].reshape(N_MOD, D_MODEL)
    mc = mod[1].reshape(N_MOD, D_MODEL)

    w13 = w13_ffn1[li]
    w2 = w2_ffn1[li]
    x1 = _half_ffn(xs, _rows8(mx, 0, 3), g_ffn1[li], w13, w2, g_final, tm=FFN_TM, final_norm=False)
    c1 = _half_ffn(cs, _rows8(mc, 0, 3), g_ffn1[li], w13, w2, g_final, tm=CTX_LEN,
                   final_norm=False)

    hx = _prenorm(x1, _rows8(mx, 3, 5), g_mix[li], tm=512)
    hc = _prenorm(c1, _rows8(mc, 3, 5), g_mix[li], tm=CTX_LEN)

    win = w_in[li]
    partner, sign = _rope_swap()
    w_kr = win[:, KV_END:KR_END]
    w_a = jnp.concatenate([win[:, :KV_END], w_kr, w_kr[:, partner] * sign], axis=1).astype(BF16)
    w_hg = win[:, KR_END:].astype(BF16)
    wq = w_uq[li].reshape(Q_LORA, MLA_HEADS, QK_NOPE + QK_ROPE)
    wq_r = wq[:, :, QK_NOPE:]
    wq_p = jnp.concatenate([wq[:, :, :QK_NOPE], wq_r, wq_r[:, :, partner] * sign], axis=2)
    wq_p = wq_p.reshape(Q_LORA, MLA_HEADS * HEAD_PAD).astype(BF16)
    wkv = w_ukv[li].reshape(KV_LORA, MLA_HEADS, QK_NOPE + V_HEAD)
    wkv_p = jnp.concatenate([wkv[:, :, :QK_NOPE].reshape(KV_LORA, -1),
                             wkv[:, :, QK_NOPE:].reshape(KV_LORA, -1)], axis=1).astype(BF16)

    tab_q, tab_k, tab_k_ctx = _rope_tables()
    q, k_lat, v_lat = _proj_attn(hx, w_a, g_q[li], g_kv[li], wq_p, wkv_p, tab_q, tab_k,
                                 tm=512, with_q=True)
    k_ctx, v_ctx = _proj_attn(hc, w_a, g_q[li], g_kv[li], wq_p, wkv_p, tab_q[:CTX_LEN],
                              tab_k_ctx, tm=CTX_LEN, with_q=False)
    attn = _attention(q, k_lat, k_ctx, v_lat, v_ctx)

    hg = _proj_hg(hx, w_hg)
    u3 = _short_conv(hg, hy_conv_w[li], hy_conv_b[li], col0=0)
    w1p = jnp.zeros((HY_ZW, HY_FILTER_W), F32).at[:HY_EMB].set(hy_w1[li])
    decay = np.abs(np.linspace(HY_MIN_DECAY, HY_MAX_DECAY, HY_WIDTH))
    decay2 = jnp.asarray(np.tile(decay, HY_ORDER)[None, :], F32)
    filt, sumsq = _hyena_filters(_filter_features(), w1p, hy_b1[li], hy_w2[li], hy_b2[li],
                                 hy_w3[li], hy_freq[li], decay2)
    m_a, g2, g2i, m_c = _dft_tables()
    hf = _hyena_spectrum(filt, sumsq, m_a, g2)
    nb = HY_WIDTH // FFT_LANES
    z = _long_conv_gate(u3, 0, u3, nb, hy_skip[li][0], hf, 0, m_a, g2, g2i, m_c)
    hy = _long_conv_gate(z, 0, u3, 2 * nb, hy_skip[li][1], hf, nb, m_a, g2, g2i, m_c)

    y = _merge(attn, hy, hg, 3 * HY_WIDTH, w_attn_o[li].astype(BF16), w_hy_o[li].astype(BF16))
    x2 = _out_proj(y, w_out[li].astype(BF16), x1, _rows8(mx, 5, 6))

    out = _half_ffn(x2, _rows8(mx, 6, 9), g_ffn2[li], w13_ffn2[li], w2_ffn2[li], g_final,
                    tm=FFN_TM, final_norm=True)
    return out[None]
```

```python
import functools
import math

import numpy as np
import jax
import jax.numpy as jnp
from jax import lax
from jax.experimental import pallas as pl
from jax.experimental.pallas import tpu as pltpu

F32 = jnp.float32
BF16 = jnp.bfloat16

D_MODEL = 2048
SEQ = 8192
GRID_W = 64
CTX_LEN = 256
N_MOD = 9
D_FF = 5632
NORM_EPS = 1e-6

MLA_HEADS = 16
Q_LORA = 512
KV_LORA = 512
QK_NOPE = 128
QK_ROPE = 64
V_HEAD = 128
ROPE_AXIS = QK_ROPE // 2
ROPE_THETA = 10000.0
ATTN_SCALE = (QK_NOPE + QK_ROPE) ** -0.5
HEAD_PAD = 256

HY_WIDTH = 1024
HY_ORDER = 2
HY_EMB = 33
HY_BANDS = (HY_EMB - 1) // 2
HY_FILTER_W = 64
HY_TARGET = 1e-2
HY_FAST_DECAY = 0.3
HY_SLOW_DECAY = 1.5
HY_MAX_DECAY = math.log(HY_TARGET) / HY_FAST_DECAY
HY_MIN_DECAY = math.log(HY_TARGET) / HY_SLOW_DECAY
HY_MOD_SHIFT = 0.05

Q_END = Q_LORA
KV_END = Q_END + KV_LORA
KR_END = KV_END + QK_ROPE
HY_END = KR_END + 3 * HY_WIDTH

FFT_N = 2 * SEQ
FFT_N1 = 128
FFT_N2 = 128
FFT_HALF = SEQ // FFT_N2

V7X_VMEM_LIMIT = 60 * 1024 * 1024
LANES = 128


def _cparams(sem, vmem=V7X_VMEM_LIMIT):
    return pltpu.CompilerParams(dimension_semantics=sem, vmem_limit_bytes=vmem)


def _sigmoid(x):
    return 1.0 / (1.0 + jnp.exp(-x))


def _rms(x, g):
    var = jnp.mean(x * x, axis=-1, keepdims=True)
    return x * lax.rsqrt(var + NORM_EPS) * g


def _mod_kernel(c_ref, w_ref, b_ref, o_ref):
    c = c_ref[...]
    a = c * _sigmoid(c)
    o_ref[...] = jnp.dot(a, w_ref[...], preferred_element_type=F32) + b_ref[...]


def _modulation(c8, w_mod, b_mod):
    n = w_mod.shape[1]
    tn = 1024
    return pl.pallas_call(
        _mod_kernel,
        grid=(n // tn,),
        in_specs=[
            pl.BlockSpec((8, D_MODEL), lambda j: (0, 0)),
            pl.BlockSpec((D_MODEL, tn), lambda j: (0, j)),
            pl.BlockSpec((1, tn), lambda j: (0, j)),
        ],
        out_specs=pl.BlockSpec((8, tn), lambda j: (0, j)),
        out_shape=jax.ShapeDtypeStruct((8, n), F32),
        compiler_params=_cparams(("arbitrary",)),
        name="modulation",
    )(c8, w_mod, b_mod.reshape(1, n))


def _ffn_kernel(s_ref, mod_ref, g_ref, w1_ref, w3_ref, w2_ref, gf_ref, o_ref, h_scr,
                *, final_norm):
    f = pl.program_id(1)

    @pl.when(f == 0)
    def _():
        h = _rms(s_ref[...], g_ref[...])
        h = h * (1.0 + mod_ref[1:2, :]) + mod_ref[0:1, :]
        h_scr[...] = h.astype(BF16)
        o_ref[...] = jnp.zeros_like(o_ref)

    h = h_scr[...]
    a = jnp.dot(h, w1_ref[...].astype(BF16), preferred_element_type=F32)
    b = jnp.dot(h, w3_ref[...].astype(BF16), preferred_element_type=F32)
    act = (a * _sigmoid(a) * b).astype(BF16)
    o_ref[...] += jnp.dot(act, w2_ref[...].astype(BF16), preferred_element_type=F32)

    @pl.when(f == pl.num_programs(1) - 1)
    def _():
        out = s_ref[...] + 0.5 * mod_ref[2:3, :] * o_ref[...]
        if final_norm:
            out = _rms(out, gf_ref[...])
        o_ref[...] = out


FFN_TM = 1024


FFN_TF = 256


def _half_ffn(s, mod3, g, w13, w2, g_final, *, tm, final_norm):
    rows = s.shape[0]
    tf = FFN_TF
    nf = D_FF // tf
    if w13.ndim == 3:
        w13_specs = [pl.BlockSpec((None, D_MODEL, tf), lambda i, f: (f, 0, 0)),
                     pl.BlockSpec((None, D_MODEL, tf), lambda i, f: (f + nf, 0, 0))]
    else:
        w13_specs = [pl.BlockSpec((D_MODEL, tf), lambda i, f: (0, f)),
                     pl.BlockSpec((D_MODEL, tf), lambda i, f: (0, f + nf))]
    return pl.pallas_call(
        functools.partial(_ffn_kernel, final_norm=final_norm),
        grid=(rows // tm, nf),
        in_specs=[
            pl.BlockSpec((tm, D_MODEL), lambda i, f: (i, 0)),
            pl.BlockSpec((8, D_MODEL), lambda i, f: (0, 0)),
            pl.BlockSpec((1, D_MODEL), lambda i, f: (0, 0)),
            *w13_specs,
            pl.BlockSpec((tf, D_MODEL), lambda i, f: (f, 0)),
            pl.BlockSpec((1, D_MODEL), lambda i, f: (0, 0)),
        ],
        out_specs=pl.BlockSpec((tm, D_MODEL), lambda i, f: (i, 0)),
        out_shape=jax.ShapeDtypeStruct((rows, D_MODEL), F32),
        scratch_shapes=[pltpu.VMEM((tm, D_MODEL), BF16)],
        compiler_params=_cparams(("arbitrary", "arbitrary")),
        name="half_ffn",
    )(s, mod3, g.reshape(1, D_MODEL), w13, w13, w2, g_final.reshape(1, D_MODEL))


def _prenorm_kernel(s_ref, mod_ref, g_ref, o_ref):
    h = _rms(s_ref[...], g_ref[...])
    o_ref[...] = (h * (1.0 + mod_ref[1:2, :]) + mod_ref[0:1, :]).astype(BF16)


def _prenorm(s, mod3, g, *, tm):
    rows = s.shape[0]
    return pl.pallas_call(
        _prenorm_kernel,
        grid=(rows // tm,),
        in_specs=[
            pl.BlockSpec((tm, D_MODEL), lambda i: (i, 0)),
            pl.BlockSpec((8, D_MODEL), lambda i: (0, 0)),
            pl.BlockSpec((1, D_MODEL), lambda i: (0, 0)),
        ],
        out_specs=pl.BlockSpec((tm, D_MODEL), lambda i: (i, 0)),
        out_shape=jax.ShapeDtypeStruct((rows, D_MODEL), BF16),
        compiler_params=_cparams(("arbitrary",)),
        name="prenorm",
    )(s, mod3, g.reshape(1, D_MODEL))


def _proj_hg_kernel(h_ref, w_ref, o_ref, *, n_plain_tiles):
    j = pl.program_id(1)
    acc = jnp.dot(h_ref[...], w_ref[...], preferred_element_type=F32)

    @pl.when(j < n_plain_tiles)
    def _():
        o_ref[...] = acc.astype(BF16)

    @pl.when(j >= n_plain_tiles)
    def _():
        o_ref[...] = _sigmoid(acc).astype(BF16)


def _proj_hg(h, w_hg):
    rows, n = h.shape[0], w_hg.shape[1]
    tm, tn = 1024, 1024
    return pl.pallas_call(
        functools.partial(_proj_hg_kernel, n_plain_tiles=3 * HY_WIDTH // tn),
        grid=(rows // tm, n // tn),
        in_specs=[
            pl.BlockSpec((tm, D_MODEL), lambda i, j: (i, 0)),
            pl.BlockSpec((D_MODEL, tn), lambda i, j: (0, j)),
        ],
        out_specs=pl.BlockSpec((tm, tn), lambda i, j: (i, j)),
        out_shape=jax.ShapeDtypeStruct((rows, n), BF16),
        compiler_params=_cparams(("arbitrary", "arbitrary")),
        name="proj_gate_hyena",
    )(h, w_hg)


def _proj_attn_kernel(h_ref, wa_ref, gq_ref, gkv_ref, wuq_ref, wukv_ref, tq_ref, tk_ref,
                      *out_refs, with_q):
    if with_q:
        q_ref, k_ref, v_ref = out_refs
    else:
        k_ref, v_ref = out_refs
    p = jnp.dot(h_ref[...], wa_ref[...], preferred_element_type=F32)
    kvn = _rms(p[:, Q_LORA:Q_LORA + KV_LORA], gkv_ref[...]).astype(BF16)
    t = p[:, KV_END:KV_END + 2 * QK_ROPE] * tk_ref[...]
    krot = (t + pltpu.roll(t, QK_ROPE, 1)).astype(BF16)
    kn = jnp.dot(kvn, wukv_ref[:, :MLA_HEADS * QK_NOPE], preferred_element_type=F32)
    for hd in range(MLA_HEADS):
        k_ref[:, hd * HEAD_PAD:hd * HEAD_PAD + QK_NOPE] = (
            kn[:, hd * QK_NOPE:(hd + 1) * QK_NOPE].astype(BF16))
        k_ref[:, hd * HEAD_PAD + QK_NOPE:(hd + 1) * HEAD_PAD] = krot
    vv = jnp.dot(kvn, wukv_ref[:, MLA_HEADS * QK_NOPE:], preferred_element_type=F32)
    lane = lax.broadcasted_iota(jnp.int32, (h_ref.shape[0], HEAD_PAD - V_HEAD), 1)
    unit = jnp.where(lane == 0, 1.0, 0.0).astype(BF16)
    for hd in range(MLA_HEADS):
        v_ref[:, hd * HEAD_PAD:hd * HEAD_PAD + V_HEAD] = (
            vv[:, hd * V_HEAD:(hd + 1) * V_HEAD].astype(BF16))
        v_ref[:, hd * HEAD_PAD + V_HEAD:(hd + 1) * HEAD_PAD] = unit
    if with_q:
        qn = _rms(p[:, :Q_LORA], gq_ref[...]).astype(BF16)
        tq = tq_ref[...]
        for hd in range(MLA_HEADS):
            qh = jnp.dot(qn, wuq_ref[:, hd * HEAD_PAD:(hd + 1) * HEAD_PAD],
                         preferred_element_type=F32)
            q_ref[:, hd * HEAD_PAD:(hd + 1) * HEAD_PAD] = (qh * tq).astype(BF16)


def _proj_attn(h, w_a, g_q, g_kv, w_uq, w_ukv, tab_q, tab_k, *, tm, with_q):
    rows = h.shape[0]
    na = w_a.shape[1]
    hp = MLA_HEADS * HEAD_PAD
    hv = MLA_HEADS * HEAD_PAD
    const = lambda i: (0, 0)
    row = lambda i: (i, 0)
    out_shape = [jax.ShapeDtypeStruct((rows, hp), BF16), jax.ShapeDtypeStruct((rows, hv), BF16)]
    out_specs = [pl.BlockSpec((tm, hp), row), pl.BlockSpec((tm, hv), row)]
    if with_q:
        out_shape = [jax.ShapeDtypeStruct((rows, hp), BF16)] + out_shape
        out_specs = [pl.BlockSpec((tm, hp), row)] + out_specs
    return pl.pallas_call(
        functools.partial(_proj_attn_kernel, with_q=with_q),
        grid=(rows // tm,),
        in_specs=[
            pl.BlockSpec((tm, D_MODEL), row),
            pl.BlockSpec((D_MODEL, na), const),
            pl.BlockSpec((1, Q_LORA), const),
            pl.BlockSpec((1, KV_LORA), const),
            pl.BlockSpec((Q_LORA, hp), const),
            pl.BlockSpec((KV_LORA, MLA_HEADS * (QK_NOPE + V_HEAD)), const),
            pl.BlockSpec((tm, HEAD_PAD), row),
            pl.BlockSpec((tm, 2 * QK_ROPE), row),
        ],
        out_specs=out_specs,
        out_shape=out_shape,
        compiler_params=_cparams(("arbitrary",)),
        name="proj_attn",
    )(h, w_a, g_q.reshape(1, Q_LORA), g_kv.reshape(1, KV_LORA), w_uq, w_ukv, tab_q, tab_k)


ATT_TQ = 1024
ATT_SUB = 512
ATT_KC = 1024
_NT = (((1,), (1,)), ((), ()))


def _attn_kernel(q_ref, kl_ref, kc_ref, vl_ref, vc_ref, o_ref):
    n_sub = ATT_TQ // ATT_SUB
    qs, ms, accs = [], [], []
    for r in range(n_sub):
        q = q_ref[r * ATT_SUB:(r + 1) * ATT_SUB, :]
        s = lax.dot_general(q, kc_ref[...], _NT, preferred_element_type=F32)
        m = jnp.max(s, axis=-1, keepdims=True)
        p = jnp.exp2(s - m)
        qs.append(q)
        ms.append(m)
        accs.append(jnp.dot(p.astype(BF16), vc_ref[...], preferred_element_type=F32))
    for c in range(SEQ // ATT_KC):
        for r in range(n_sub):
            s = lax.dot_general(qs[r], kl_ref[c * ATT_KC:(c + 1) * ATT_KC, :], _NT,
                                preferred_element_type=F32)
            m_new = jnp.maximum(ms[r], jnp.max(s, axis=-1, keepdims=True))
            alpha = jnp.exp2(ms[r] - m_new)
            p = jnp.exp2(s - m_new)
            accs[r] = accs[r] * alpha + jnp.dot(
                p.astype(BF16), vl_ref[c * ATT_KC:(c + 1) * ATT_KC, :],
                preferred_element_type=F32)
            ms[r] = m_new
    for r in range(n_sub):
        acc = accs[r]
        o_ref[r * ATT_SUB:(r + 1) * ATT_SUB, :] = (
            acc[:, :V_HEAD] / acc[:, V_HEAD:V_HEAD + 1]).astype(BF16)


def _attention(q, k_lat, k_ctx, v_lat, v_ctx):
    return pl.pallas_call(
        _attn_kernel,
        grid=(MLA_HEADS, SEQ // ATT_TQ),
        in_specs=[
            pl.BlockSpec((ATT_TQ, HEAD_PAD), lambda h, i: (i, h)),
            pl.BlockSpec((SEQ, HEAD_PAD), lambda h, i: (0, h)),
            pl.BlockSpec((CTX_LEN, HEAD_PAD), lambda h, i: (0, h)),
            pl.BlockSpec((SEQ, HEAD_PAD), lambda h, i: (0, h)),
            pl.BlockSpec((CTX_LEN, HEAD_PAD), lambda h, i: (0, h)),
        ],
        out_specs=pl.BlockSpec((ATT_TQ, V_HEAD), lambda h, i: (i, h)),
        out_shape=jax.ShapeDtypeStruct((SEQ, MLA_HEADS * V_HEAD), BF16),
        compiler_params=_cparams(("arbitrary", "arbitrary")),
        name="attention",
    )(q, k_lat, k_ctx, v_lat, v_ctx)


SC_HALO = 16


def _short_conv_kernel(x_ref, prev_ref, next_ref, w_ref, b_ref, o_ref):
    i = pl.program_id(0)
    tm = x_ref.shape[0]
    x = x_ref[...].astype(F32)
    prev_row = prev_ref[SC_HALO - 1:SC_HALO, :].astype(F32) * (i > 0).astype(F32)
    next_row = next_ref[0:1, :].astype(F32) * (i < pl.num_programs(0) - 1).astype(F32)
    rows = lax.broadcasted_iota(jnp.int32, x.shape, 0)
    up = jnp.where(rows == 0, prev_row, pltpu.roll(x, 1, 0))
    dn = jnp.where(rows == tm - 1, next_row, pltpu.roll(x, tm - 1, 0))
    y = up * w_ref[0:1, :] + x * w_ref[1:2, :] + dn * w_ref[2:3, :] + b_ref[...]
    for blk in range(o_ref.shape[0]):
        o_ref[blk] = y[:, blk * LANES:(blk + 1) * LANES]


def _short_conv(hg, conv_w, conv_b, *, col0):
    tm, tc = 512, 1024
    nb = 3 * HY_WIDTH // tc
    cb0 = col0 // tc
    hb = tm // SC_HALO
    last = SEQ // SC_HALO - 1
    w8 = jnp.zeros((8, 3 * HY_WIDTH), F32).at[:3].set(conv_w)
    return pl.pallas_call(
        _short_conv_kernel,
        grid=(SEQ // tm, nb),
        in_specs=[
            pl.BlockSpec((tm, tc), lambda i, j: (i, j + cb0)),
            pl.BlockSpec((SC_HALO, tc), lambda i, j: (jnp.maximum(i * hb - 1, 0), j + cb0)),
            pl.BlockSpec((SC_HALO, tc), lambda i, j: (jnp.minimum((i + 1) * hb, last), j + cb0)),
            pl.BlockSpec((8, tc), lambda i, j: (0, j)),
            pl.BlockSpec((1, tc), lambda i, j: (0, j)),
        ],
        out_specs=pl.BlockSpec((tc // LANES, tm, LANES), lambda i, j: (j, i, 0)),
        out_shape=jax.ShapeDtypeStruct((3 * HY_WIDTH // LANES, SEQ, LANES), F32),
        compiler_params=_cparams(("arbitrary", "arbitrary")),
        name="short_conv",
    )(hg, hg, hg, w8, conv_b.reshape(1, -1))


HY_ZW = 128


def _dot_split3(a, b):
    a_hi = a.astype(BF16)
    b_hi = b.astype(BF16)
    a_lo = (a - a_hi.astype(F32)).astype(BF16)
    b_lo = (b - b_hi.astype(F32)).astype(BF16)
    dot = functools.partial(jnp.dot, preferred_element_type=F32)
    return dot(a_hi, b_hi) + (dot(a_hi, b_lo) + dot(a_lo, b_hi))


def _filter_kernel(z_ref, w1_ref, b1_ref, w2_ref, b2_ref, w3_ref, fr_ref, dec_ref,
                   h_ref, ss_ref):
    hi = lax.Precision.HIGHEST
    z = z_ref[...]
    fr = fr_ref[...]
    h = jnp.sin(fr * (jnp.dot(z, w1_ref[...], precision=hi, preferred_element_type=F32)
                      + b1_ref[...]))
    h = jnp.sin(fr * (jnp.dot(h, w2_ref[...], precision=hi, preferred_element_type=F32)
                      + b2_ref[...]))
    h = _dot_split3(h, w3_ref[...])
    dist = z[:, HY_EMB:HY_EMB + 1]
    h = h * (jnp.exp(-dist * dec_ref[...]) + HY_MOD_SHIFT)
    for blk in range(h_ref.shape[0]):
        h_ref[blk] = h[:, blk * LANES:(blk + 1) * LANES]

    @pl.when(pl.program_id(0) == 0)
    def _():
        ss_ref[...] = jnp.zeros_like(ss_ref)

    ss_ref[...] += jnp.sum(h * h, axis=0, keepdims=True)


def _hyena_filters(z_p, w1p, b1, w2, b2, w3, freq, decay2):
    tl = 512
    nw = HY_ORDER * HY_WIDTH
    const = lambda i: (0, 0)
    return pl.pallas_call(
        _filter_kernel,
        grid=(SEQ // tl,),
        in_specs=[
            pl.BlockSpec((tl, HY_ZW), lambda i: (i, 0)),
            pl.BlockSpec((HY_ZW, HY_FILTER_W), const),
            pl.BlockSpec((1, HY_FILTER_W), const),
            pl.BlockSpec((HY_FILTER_W, HY_FILTER_W), const),
            pl.BlockSpec((1, HY_FILTER_W), const),
            pl.BlockSpec((HY_FILTER_W, nw), const),
            pl.BlockSpec((1, HY_FILTER_W), const),
            pl.BlockSpec((1, nw), const),
        ],
        out_specs=[pl.BlockSpec((nw // LANES, tl, LANES), lambda i: (0, i, 0)),
                   pl.BlockSpec((1, nw), const)],
        out_shape=[jax.ShapeDtypeStruct((nw // LANES, SEQ, LANES), F32),
                   jax.ShapeDtypeStruct((1, nw), F32)],
        compiler_params=_cparams(("arbitrary",)),
        name="hyena_filters",
    )(z_p, w1p, b1.reshape(1, -1), w2, b2.reshape(1, -1), w3, freq.reshape(1, -1), decay2)


FFT_LANES = 128
FFT_G = 32
FFT_STEPS = FFT_N2 // FFT_G
FFT_ROWS = 2 * FFT_N1
Y_PITCH = FFT_ROWS + 8
T_PITCH = FFT_N2 + 8


def _pad_rows_in(src_ref, dst_scr, scale=None):
    for n1 in range(FFT_HALF):
        v = src_ref[n1 * FFT_N2:(n1 + 1) * FFT_N2, :]
        dst_scr[n1 * T_PITCH:n1 * T_PITCH + FFT_N2, :] = v if scale is None else v * scale


def _fft_stage_a(t, src_scr, ma_ref, y_scr):
    for g in range(FFT_G):
        j = t * FFT_G + g
        xj = src_scr[pl.ds(j, FFT_HALF, stride=T_PITCH), :].astype(BF16)
        xj = jnp.concatenate([xj, jnp.zeros_like(xj)], axis=0)
        a = jnp.dot(ma_ref[g], xj, preferred_element_type=F32)
        y_scr[pl.ds(pl.multiple_of(j * Y_PITCH, 8), FFT_ROWS), :] = a


def _fft_load_pair(tb, g, y_scr):
    cols = []
    for k1 in (tb * FFT_G + g, tb * FFT_G + g + 1):
        re = y_scr[pl.ds(k1, FFT_N2, stride=Y_PITCH), :]
        im = y_scr[pl.ds(FFT_N1 + k1, FFT_N2, stride=Y_PITCH), :]
        cols.append(jnp.concatenate([re, im], axis=0))
    return jnp.concatenate(cols, axis=1).astype(BF16)


def _hyena_spectrum_kernel(h_ref, ss_ref, ma_ref, g2_ref, o_ref, y_scr, h_scr):
    t = pl.program_id(1)

    @pl.when(t == 0)
    def _():
        _pad_rows_in(h_ref, h_scr, lax.rsqrt(ss_ref[...] + NORM_EPS))

    @pl.when(t < FFT_STEPS)
    def _():
        _fft_stage_a(t, h_scr, ma_ref, y_scr)

    @pl.when(t >= FFT_STEPS)
    def _():
        tb = t - FFT_STEPS
        for g in range(0, FFT_G, 2):
            z = jnp.dot(g2_ref[...], _fft_load_pair(tb, g, y_scr), preferred_element_type=F32)
            o_ref[g] = z[:, :FFT_LANES].astype(BF16)
            o_ref[g + 1] = z[:, FFT_LANES:].astype(BF16)


def _hyena_spectrum(filt, sumsq, m_a, g2):
    nb = filt.shape[0]
    last = FFT_STEPS - 1
    return pl.pallas_call(
        _hyena_spectrum_kernel,
        grid=(nb, 2 * FFT_STEPS),
        in_specs=[
            pl.BlockSpec((None, SEQ, FFT_LANES), lambda c, t: (c, 0, 0)),
            pl.BlockSpec((1, FFT_LANES), lambda c, t: (0, c)),
            pl.BlockSpec((FFT_G, FFT_ROWS, LANES), lambda c, t: (jnp.minimum(t, last), 0, 0)),
            pl.BlockSpec((FFT_ROWS, FFT_ROWS), lambda c, t: (0, 0)),
        ],
        out_specs=pl.BlockSpec((None, FFT_G, FFT_ROWS, FFT_LANES),
                               lambda c, t: (c, jnp.maximum(t - FFT_STEPS, 0), 0, 0)),
        out_shape=jax.ShapeDtypeStruct((nb, FFT_N1, FFT_ROWS, FFT_LANES), BF16),
        scratch_shapes=[pltpu.VMEM((FFT_N2 * Y_PITCH, FFT_LANES), F32),
                        pltpu.VMEM((FFT_HALF * T_PITCH, FFT_LANES), F32)],
        compiler_params=_cparams(("arbitrary", "arbitrary")),
        name="hyena_spectrum",
    )(filt, sumsq, m_a, g2)


def _long_conv_kernel(u_ref, x_ref, skip_ref, ma_ref, hf_ref, g2_ref, g2i_ref, mc_ref,
                      o_ref, y_scr, stage_scr, u_scr, x_scr):
    t = pl.program_id(1)

    @pl.when(t == 0)
    def _():
        _pad_rows_in(u_ref, u_scr)
        _pad_rows_in(x_ref, x_scr)

    @pl.when(t < FFT_STEPS)
    def _():
        _fft_stage_a(t, u_scr, ma_ref, y_scr)

    @pl.when((t >= FFT_STEPS) & (t < 2 * FFT_STEPS))
    def _():
        tb = t - FFT_STEPS
        for g in range(0, FFT_G, 2):
            stage_scr[g // 2] = _fft_load_pair(tb, g, y_scr)
        for g in range(0, FFT_G, 2):
            z = jnp.dot(g2_ref[...], stage_scr[g // 2], preferred_element_type=F32)
            hf = jnp.concatenate([hf_ref[g], hf_ref[g + 1]], axis=1).astype(F32)
            zr, zi = z[:FFT_N2], z[FFT_N2:]
            hr, hi = hf[:FFT_N2], hf[FFT_N2:]
            p = jnp.concatenate([zr * hr - zi * hi, zr * hi + zi * hr], axis=0).astype(BF16)
            b = jnp.dot(g2i_ref[...], p, preferred_element_type=F32)
            for half, k1 in enumerate((tb * FFT_G + g, tb * FFT_G + g + 1)):
                lanes = slice(half * FFT_LANES, (half + 1) * FFT_LANES)
                y_scr[pl.ds(k1, FFT_N2, stride=Y_PITCH), :] = b[:FFT_N2, lanes]
                y_scr[pl.ds(FFT_N1 + k1, FFT_N2, stride=Y_PITCH), :] = b[FFT_N2:, lanes]

    @pl.when(t >= 2 * FFT_STEPS)
    def _():
        tc = t - 2 * FFT_STEPS
        for g in range(FFT_G):
            m2 = tc * FFT_G + g
            bm = y_scr[pl.ds(pl.multiple_of(m2 * Y_PITCH, 8), FFT_ROWS), :]
            y = jnp.dot(mc_ref[g], bm.astype(BF16), preferred_element_type=F32)
            rows = pl.ds(m2, FFT_HALF, stride=T_PITCH)
            u_scr[rows, :] = x_scr[rows, :] * (y + skip_ref[...] * u_scr[rows, :])

    @pl.when(t == 3 * FFT_STEPS - 1)
    def _():
        for n1 in range(FFT_HALF):
            o_ref[n1 * FFT_N2:(n1 + 1) * FFT_N2, :] = u_scr[n1 * T_PITCH:n1 * T_PITCH + FFT_N2, :]


def _long_conv_gate(u, ub, x, xb, skip, hf, hf0, m_a, g2, g2i, m_c):
    nb = HY_WIDTH // FFT_LANES
    last = FFT_STEPS - 1
    step = lambda t, phase: jnp.clip(t - phase * FFT_STEPS, 0, last)
    return pl.pallas_call(
        _long_conv_kernel,
        grid=(nb, 3 * FFT_STEPS),
        in_specs=[
            pl.BlockSpec((None, SEQ, FFT_LANES), lambda c, t: (c + ub, 0, 0)),
            pl.BlockSpec((None, SEQ, FFT_LANES), lambda c, t: (c + xb, 0, 0)),
            pl.BlockSpec((1, FFT_LANES), lambda c, t: (0, c)),
            pl.BlockSpec((FFT_G, FFT_ROWS, LANES), lambda c, t: (step(t, 0), 0, 0)),
            pl.BlockSpec((None, FFT_G, FFT_ROWS, FFT_LANES),
                         lambda c, t: (c + hf0, step(t, 1), 0, 0)),
            pl.BlockSpec((FFT_ROWS, FFT_ROWS), lambda c, t: (0, 0)),
            pl.BlockSpec((FFT_ROWS, FFT_ROWS), lambda c, t: (0, 0)),
            pl.BlockSpec((FFT_G, FFT_HALF, FFT_ROWS), lambda c, t: (step(t, 2), 0, 0)),
        ],
        out_specs=pl.BlockSpec((None, SEQ, FFT_LANES), lambda c, t: (c, 0, 0)),
        out_shape=jax.ShapeDtypeStruct((nb, SEQ, FFT_LANES), F32),
        scratch_shapes=[pltpu.VMEM((FFT_N2 * Y_PITCH, FFT_LANES), F32),
                        pltpu.VMEM((FFT_G // 2, FFT_ROWS, 2 * FFT_LANES), BF16),
                        pltpu.VMEM((FFT_HALF * T_PITCH, FFT_LANES), F32),
                        pltpu.VMEM((FFT_HALF * T_PITCH, FFT_LANES), F32)],
        compiler_params=_cparams(("arbitrary", "arbitrary")),
        name="long_conv_gate",
    )(u, x, skip.reshape(1, HY_WIDTH), m_a, hf, g2, g2i, m_c)


def _merge_kernel(a_ref, hy_ref, ga_ref, gh_ref, wa_ref, wh_ref, o_ref):
    ya = jnp.dot(a_ref[...], wa_ref[...], preferred_element_type=F32)
    hy = jnp.concatenate([hy_ref[blk].astype(BF16) for blk in range(hy_ref.shape[0])], axis=1)
    yh = jnp.dot(hy, wh_ref[...], preferred_element_type=F32)
    o_ref[...] = (ga_ref[...].astype(F32) * ya + gh_ref[...].astype(F32) * yh).astype(BF16)


def _merge(attn, hy, hg, gate_col0, w_attn_o, w_hy_o):
    tm, tn = 1024, 1024
    ga0 = gate_col0 // tn
    gh0 = (gate_col0 + D_MODEL) // tn
    row = lambda i, j: (i, 0)
    return pl.pallas_call(
        _merge_kernel,
        grid=(SEQ // tm, D_MODEL // tn),
        in_specs=[
            pl.BlockSpec((tm, MLA_HEADS * V_HEAD), row),
            pl.BlockSpec((HY_WIDTH // LANES, tm, LANES), lambda i, j: (0, i, 0)),
            pl.BlockSpec((tm, tn), lambda i, j: (i, j + ga0)),
            pl.BlockSpec((tm, tn), lambda i, j: (i, j + gh0)),
            pl.BlockSpec((MLA_HEADS * V_HEAD, tn), lambda i, j: (0, j)),
            pl.BlockSpec((HY_WIDTH, tn), lambda i, j: (0, j)),
        ],
        out_specs=pl.BlockSpec((tm, tn), lambda i, j: (i, j)),
        out_shape=jax.ShapeDtypeStruct((SEQ, D_MODEL), BF16),
        compiler_params=_cparams(("arbitrary", "arbitrary")),
        name="merge_branches",
    )(attn, hy, hg, hg, w_attn_o, w_hy_o)


def _out_proj_kernel(y_ref, w_ref, s_ref, mod_ref, o_ref):
    xm = jnp.dot(y_ref[...], w_ref[...], preferred_element_type=F32)
    o_ref[...] = s_ref[...] + mod_ref[0:1, :] * xm


def _out_proj(y, w_out, s, mod_gate):
    tm = 512
    row = lambda i: (i, 0)
    const = lambda i: (0, 0)
    return pl.pallas_call(
        _out_proj_kernel,
        grid=(SEQ // tm,),
        in_specs=[
            pl.BlockSpec((tm, D_MODEL), row),
            pl.BlockSpec((D_MODEL, D_MODEL), const),
            pl.BlockSpec((tm, D_MODEL), row),
            pl.BlockSpec((8, D_MODEL), const),
        ],
        out_specs=pl.BlockSpec((tm, D_MODEL), row),
        out_shape=jax.ShapeDtypeStruct((SEQ, D_MODEL), F32),
        compiler_params=_cparams(("arbitrary",)),
        name="out_proj",
    )(y, w_out, s, mod_gate)


def _rope_tables():
    t = np.arange(SEQ)
    pos = np.stack([t // GRID_W, t % GRID_W], axis=1).astype(np.float64)
    inv_freq = ROPE_THETA ** (-np.arange(0, ROPE_AXIS, 2, dtype=np.float64) / ROPE_AXIS)
    i = np.arange(QK_ROPE)
    ang = pos[:, i // ROPE_AXIS] * inv_freq[i % (ROPE_AXIS // 2)][None, :]
    cos, sin = np.cos(ang), np.sin(ang)
    tab_q = np.concatenate([np.ones((SEQ, QK_NOPE)), cos, sin], axis=1) * (ATTN_SCALE * math.log2(math.e))
    tab_k = np.concatenate([cos, sin], axis=1)
    tab_k_ctx = np.concatenate([np.ones((CTX_LEN, QK_ROPE)), np.zeros((CTX_LEN, QK_ROPE))], axis=1)
    return (jnp.asarray(tab_q, F32), jnp.asarray(tab_k, F32), jnp.asarray(tab_k_ctx, F32))


def _rope_swap():
    i = np.arange(QK_ROPE)
    first_half = (i % ROPE_AXIS) < ROPE_AXIS // 2
    partner = np.where(first_half, i + ROPE_AXIS // 2, i - ROPE_AXIS // 2)
    sign = np.where(first_half, -1.0, 1.0)
    return partner, sign


def _filter_features():
    pos = np.arange(SEQ, dtype=np.float64)[:, None]
    t01 = pos / (SEQ - 1)
    bands = np.linspace(1e-4, HY_BANDS - 1, HY_BANDS)[None, :]
    ang = bands * (2.0 * math.pi / SEQ) * pos
    dist = np.abs(pos - (SEQ // 2)) / (SEQ / 2.0)
    z = np.concatenate([t01, np.cos(ang), -np.sin(ang), dist], axis=1)
    z = np.pad(z, ((0, 0), (0, HY_ZW - z.shape[1])))
    return jnp.asarray(z, F32)


def _dft_tables():
    n1 = np.arange(FFT_HALF)
    k1 = np.arange(FFT_N1)
    n2 = np.arange(FFT_N2)
    n = FFT_N2 * n1[None, None, :] + n2[:, None, None]
    ph = (k1[None, :, None] * n) % FFT_N
    th = 2.0 * math.pi * ph / FFT_N
    m_a = np.concatenate([np.cos(th), -np.sin(th)], axis=1)
    m_a = np.pad(m_a, ((0, 0), (0, 0), (0, LANES - FFT_HALF)))
    ph2 = (n2[:, None] * n2[None, :]) % FFT_N2
    th2 = 2.0 * math.pi * ph2 / FFT_N2
    c, s = np.cos(th2), np.sin(th2)
    g2 = np.block([[c, s], [-s, c]])
    g2i = np.block([[c, -s], [s, c]])
    n = FFT_N2 * (n1[None, :, None] + FFT_HALF // 2) + n2[:, None, None]
    ph = (k1[None, None, :] * n) % FFT_N
    th = 2.0 * math.pi * ph / FFT_N
    m_c = np.concatenate([np.cos(th), -np.sin(th)], axis=2) / FFT_N
    return tuple(jnp.asarray(t, F32).astype(BF16) for t in (m_a, g2, g2i, m_c))


def _rows8(m, lo, hi):
    return jnp.zeros((8, D_MODEL), F32).at[:hi - lo].set(m[lo:hi])


def kernel(x, c, ctx, c_ctx, w_mod, b_mod, g_ffn1, w13_ffn1, w2_ffn1, g_mix, w_in, g_q, w_uq,
           g_kv, w_ukv, w_attn_o, hy_conv_w, hy_conv_b, hy_w1, hy_b1, hy_w2, hy_b2, hy_w3,
           hy_freq, hy_skip, w_hy_o, w_out, g_ffn2, w13_ffn2, w2_ffn2, g_final):
    xs = x[0]
    cs = ctx[0]
    li = 0

    c8 = jnp.zeros((8, D_MODEL), F32).at[0].set(c[0]).at[1].set(c_ctx)
    mod = _modulation(c8, w_mod[li], b_mod[li])
    mx = mod[0].reshape(N_MOD, D_MODEL)
    mc = mod[1].reshape(N_MOD, D_MODEL)

    w13 = w13_ffn1[li]
    w2 = w2_ffn1[li]
    x1 = _half_ffn(xs, _rows8(mx, 0, 3), g_ffn1[li], w13, w2, g_final, tm=FFN_TM, final_norm=False)
    c1 = _half_ffn(cs, _rows8(mc, 0, 3), g_ffn1[li], w13, w2, g_final, tm=CTX_LEN,
                   final_norm=False)

    hx = _prenorm(x1, _rows8(mx, 3, 5), g_mix[li], tm=512)
    hc = _prenorm(c1, _rows8(mc, 3, 5), g_mix[li], tm=CTX_LEN)

    win = w_in[li]
    partner, sign = _rope_swap()
    w_kr = win[:, KV_END:KR_END]
    w_a = jnp.concatenate([win[:, :KV_END], w_kr, w_kr[:, partner] * sign], axis=1).astype(BF16)
    w_hg = win[:, KR_END:].astype(BF16)
    wq = w_uq[li].reshape(Q_LORA, MLA_HEADS, QK_NOPE + QK_ROPE)
    wq_r = wq[:, :, QK_NOPE:]
    wq_p = jnp.concatenate([wq[:, :, :QK_NOPE], wq_r, wq_r[:, :, partner] * sign], axis=2)
    wq_p = wq_p.reshape(Q_LORA, MLA_HEADS * HEAD_PAD).astype(BF16)
    wkv = w_ukv[li].reshape(KV_LORA, MLA_HEADS, QK_NOPE + V_HEAD)
    wkv_p = jnp.concatenate([wkv[:, :, :QK_NOPE].reshape(KV_LORA, -1),
                             wkv[:, :, QK_NOPE:].reshape(KV_LORA, -1)], axis=1).astype(BF16)

    tab_q, tab_k, tab_k_ctx = _rope_tables()
    q, k_lat, v_lat = _proj_attn(hx, w_a, g_q[li], g_kv[li], wq_p, wkv_p, tab_q, tab_k,
                                 tm=512, with_q=True)
    k_ctx, v_ctx = _proj_attn(hc, w_a, g_q[li], g_kv[li], wq_p, wkv_p, tab_q[:CTX_LEN],
                              tab_k_ctx, tm=CTX_LEN, with_q=False)
    attn = _attention(q, k_lat, k_ctx, v_lat, v_ctx)

    hg = _proj_hg(hx, w_hg)
    u3 = _short_conv(hg, hy_conv_w[li], hy_conv_b[li], col0=0)
    w1p = jnp.zeros((HY_ZW, HY_FILTER_W), F32).at[:HY_EMB].set(hy_w1[li])
    decay = np.abs(np.linspace(HY_MIN_DECAY, HY_MAX_DECAY, HY_WIDTH))
    decay2 = jnp.asarray(np.tile(decay, HY_ORDER)[None, :], F32)
    filt, sumsq = _hyena_filters(_filter_features(), w1p, hy_b1[li], hy_w2[li], hy_b2[li],
                                 hy_w3[li], hy_freq[li], decay2)
    m_a, g2, g2i, m_c = _dft_tables()
    hf = _hyena_spectrum(filt, sumsq, m_a, g2)
    nb = HY_WIDTH // FFT_LANES
    z = _long_conv_gate(u3, 0, u3, nb, hy_skip[li][0], hf, 0, m_a, g2, g2i, m_c)
    hy = _long_conv_gate(z, 0, u3, 2 * nb, hy_skip[li][1], hf, nb, m_a, g2, g2i, m_c)

    y = _merge(attn, hy, hg, 3 * HY_WIDTH, w_attn_o[li].astype(BF16), w_hy_o[li].astype(BF16))
    x2 = _out_proj(y, w_out[li].astype(BF16), x1, _rows8(mx, 5, 6))

    w13t = w13_ffn2[li].reshape(D_MODEL, 2 * D_FF // FFN_TF, FFN_TF).transpose(1, 0, 2)
    out = _half_ffn(x2, _rows8(mx, 6, 9), g_ffn2[li], w13t, w2_ffn2[li], g_final,
                    tm=FFN_TM, final_norm=True)
    return out[None]
```

```python
import functools
import math

import numpy as np
import jax
import jax.numpy as jnp
from jax import lax
from jax.experimental import pallas as pl
from jax.experimental.pallas import tpu as pltpu

F32 = jnp.float32
BF16 = jnp.bfloat16

D_MODEL = 2048
SEQ = 8192
GRID_W = 64
CTX_LEN = 256
N_MOD = 9
D_FF = 5632
NORM_EPS = 1e-6

MLA_HEADS = 16
Q_LORA = 512
KV_LORA = 512
QK_NOPE = 128
QK_ROPE = 64
V_HEAD = 128
ROPE_AXIS = QK_ROPE // 2
ROPE_THETA = 10000.0
ATTN_SCALE = (QK_NOPE + QK_ROPE) ** -0.5
HEAD_PAD = 256

HY_WIDTH = 1024
HY_ORDER = 2
HY_EMB = 33
HY_BANDS = (HY_EMB - 1) // 2
HY_FILTER_W = 64
HY_TARGET = 1e-2
HY_FAST_DECAY = 0.3
HY_SLOW_DECAY = 1.5
HY_MAX_DECAY = math.log(HY_TARGET) / HY_FAST_DECAY
HY_MIN_DECAY = math.log(HY_TARGET) / HY_SLOW_DECAY
HY_MOD_SHIFT = 0.05

Q_END = Q_LORA
KV_END = Q_END + KV_LORA
KR_END = KV_END + QK_ROPE
HY_END = KR_END + 3 * HY_WIDTH

FFT_N = 2 * SEQ
FFT_N1 = 128
FFT_N2 = 128
FFT_HALF = SEQ // FFT_N2

V7X_VMEM_LIMIT = 60 * 1024 * 1024
LANES = 128


def _cparams(sem, vmem=V7X_VMEM_LIMIT):
    return pltpu.CompilerParams(dimension_semantics=sem, vmem_limit_bytes=vmem)


def _sigmoid(x):
    return 1.0 / (1.0 + jnp.exp(-x))


def _rms(x, g):
    var = jnp.mean(x * x, axis=-1, keepdims=True)
    return x * lax.rsqrt(var + NORM_EPS) * g


def _mod_kernel(c_ref, w_ref, b_ref, o_ref):
    c = c_ref[...]
    a = c * _sigmoid(c)
    o_ref[...] = jnp.dot(a, w_ref[...], preferred_element_type=F32) + b_ref[...]


def _modulation(c8, w_mod, b_mod):
    n = w_mod.shape[1]
    tn = 1024
    return pl.pallas_call(
        _mod_kernel,
        grid=(n // tn,),
        in_specs=[
            pl.BlockSpec((8, D_MODEL), lambda j: (0, 0)),
            pl.BlockSpec((D_MODEL, tn), lambda j: (0, j)),
            pl.BlockSpec((1, tn), lambda j: (0, j)),
        ],
        out_specs=pl.BlockSpec((8, tn), lambda j: (0, j)),
        out_shape=jax.ShapeDtypeStruct((8, n), F32),
        compiler_params=_cparams(("arbitrary",)),
        name="modulation",
    )(c8, w_mod, b_mod.reshape(1, n))


def _ffn_kernel(s_ref, mod_ref, g_ref, w1_ref, w3_ref, w2_ref, gf_ref, o_ref, h_scr,
                *, final_norm):
    f = pl.program_id(1)

    @pl.when(f == 0)
    def _():
        h = _rms(s_ref[...], g_ref[...])
        h = h * (1.0 + mod_ref[1:2, :]) + mod_ref[0:1, :]
        h_scr[...] = h.astype(BF16)
        o_ref[...] = jnp.zeros_like(o_ref)

    h = h_scr[...]
    a = jnp.dot(h, w1_ref[...].astype(BF16), preferred_element_type=F32)
    b = jnp.dot(h, w3_ref[...].astype(BF16), preferred_element_type=F32)
    act = (a * _sigmoid(a) * b).astype(BF16)
    o_ref[...] += jnp.dot(act, w2_ref[...].astype(BF16), preferred_element_type=F32)

    @pl.when(f == pl.num_programs(1) - 1)
    def _():
        out = s_ref[...] + 0.5 * mod_ref[2:3, :] * o_ref[...]
        if final_norm:
            out = _rms(out, gf_ref[...])
        o_ref[...] = out


FFN_TM = 1024


FFN_TF = 256


def _half_ffn(s, mod3, g, w13, w2, g_final, *, tm, final_norm):
    rows = s.shape[0]
    tf = FFN_TF
    nf = D_FF // tf
    return pl.pallas_call(
        functools.partial(_ffn_kernel, final_norm=final_norm),
        grid=(rows // tm, nf),
        in_specs=[
            pl.BlockSpec((tm, D_MODEL), lambda i, f: (i, 0)),
            pl.BlockSpec((8, D_MODEL), lambda i, f: (0, 0)),
            pl.BlockSpec((1, D_MODEL), lambda i, f: (0, 0)),
            pl.BlockSpec((D_MODEL, tf), lambda i, f: (0, f)),
            pl.BlockSpec((D_MODEL, tf), lambda i, f: (0, f + nf)),
            pl.BlockSpec((tf, D_MODEL), lambda i, f: (f, 0)),
            pl.BlockSpec((1, D_MODEL), lambda i, f: (0, 0)),
        ],
        out_specs=pl.BlockSpec((tm, D_MODEL), lambda i, f: (i, 0)),
        out_shape=jax.ShapeDtypeStruct((rows, D_MODEL), F32),
        scratch_shapes=[pltpu.VMEM((tm, D_MODEL), BF16)],
        compiler_params=_cparams(("arbitrary", "arbitrary")),
        name="half_ffn",
    )(s, mod3, g.reshape(1, D_MODEL), w13, w13, w2, g_final.reshape(1, D_MODEL))


def _prenorm_kernel(s_ref, mod_ref, g_ref, o_ref):
    h = _rms(s_ref[...], g_ref[...])
    o_ref[...] = (h * (1.0 + mod_ref[1:2, :]) + mod_ref[0:1, :]).astype(BF16)


def _prenorm(s, mod3, g, *, tm):
    rows = s.shape[0]
    return pl.pallas_call(
        _prenorm_kernel,
        grid=(rows // tm,),
        in_specs=[
            pl.BlockSpec((tm, D_MODEL), lambda i: (i, 0)),
            pl.BlockSpec((8, D_MODEL), lambda i: (0, 0)),
            pl.BlockSpec((1, D_MODEL), lambda i: (0, 0)),
        ],
        out_specs=pl.BlockSpec((tm, D_MODEL), lambda i: (i, 0)),
        out_shape=jax.ShapeDtypeStruct((rows, D_MODEL), BF16),
        compiler_params=_cparams(("arbitrary",)),
        name="prenorm",
    )(s, mod3, g.reshape(1, D_MODEL))


def _proj_hg_kernel(h_ref, wa_ref, wb_ref, o_ref, *, n_plain_tiles, lane_off):
    j = pl.program_id(1)
    w = jnp.concatenate([wa_ref[:, lane_off:], wb_ref[:, :lane_off]], axis=1).astype(BF16)
    acc = jnp.dot(h_ref[...], w, preferred_element_type=F32)
    o_ref[...] = jnp.where(j >= n_plain_tiles, _sigmoid(acc), acc).astype(BF16)


def _proj_hg(h, w_in, col0):
    rows = h.shape[0]
    n = 3 * HY_WIDTH + 2 * D_MODEL
    tm, tn = 1024, 1024
    blk0, lane_off = divmod(col0, tn)
    assert 0 < lane_off < LANES and col0 + n <= w_in.shape[1]
    return pl.pallas_call(
        functools.partial(_proj_hg_kernel, n_plain_tiles=3 * HY_WIDTH // tn, lane_off=lane_off),
        grid=(rows // tm, n // tn),
        in_specs=[
            pl.BlockSpec((tm, D_MODEL), lambda i, j: (i, 0)),
            pl.BlockSpec((D_MODEL, tn), lambda i, j: (0, j + blk0)),
            pl.BlockSpec((D_MODEL, LANES), lambda i, j: (0, (j + blk0 + 1) * (tn // LANES))),
        ],
        out_specs=pl.BlockSpec((tm, tn), lambda i, j: (i, j)),
        out_shape=jax.ShapeDtypeStruct((rows, n), BF16),
        compiler_params=_cparams(("arbitrary", "arbitrary")),
        name="proj_gate_hyena",
    )(h, w_in, w_in)


def _proj_attn_kernel(h_ref, wa_ref, gq_ref, gkv_ref, wuq_ref, wukv_ref, tq_ref, tk_ref,
                      *out_refs, with_q):
    if with_q:
        q_ref, k_ref, v_ref = out_refs
    else:
        k_ref, v_ref = out_refs
    p = jnp.dot(h_ref[...], wa_ref[...], preferred_element_type=F32)
    kvn = _rms(p[:, Q_LORA:Q_LORA + KV_LORA], gkv_ref[...]).astype(BF16)
    t = p[:, KV_END:KV_END + 2 * QK_ROPE] * tk_ref[...]
    krot = (t + pltpu.roll(t, QK_ROPE, 1)).astype(BF16)
    kn = jnp.dot(kvn, wukv_ref[:, :MLA_HEADS * QK_NOPE], preferred_element_type=F32)
    for hd in range(MLA_HEADS):
        k_ref[:, hd * HEAD_PAD:hd * HEAD_PAD + QK_NOPE] = (
            kn[:, hd * QK_NOPE:(hd + 1) * QK_NOPE].astype(BF16))
        k_ref[:, hd * HEAD_PAD + QK_NOPE:(hd + 1) * HEAD_PAD] = krot
    vv = jnp.dot(kvn, wukv_ref[:, MLA_HEADS * QK_NOPE:], preferred_element_type=F32)
    lane = lax.broadcasted_iota(jnp.int32, (h_ref.shape[0], HEAD_PAD - V_HEAD), 1)
    unit = jnp.where(lane == 0, 1.0, 0.0).astype(BF16)
    for hd in range(MLA_HEADS):
        v_ref[:, hd * HEAD_PAD:hd * HEAD_PAD + V_HEAD] = (
            vv[:, hd * V_HEAD:(hd + 1) * V_HEAD].astype(BF16))
        v_ref[:, hd * HEAD_PAD + V_HEAD:(hd + 1) * HEAD_PAD] = unit
    if with_q:
        qn = _rms(p[:, :Q_LORA], gq_ref[...]).astype(BF16)
        tq = tq_ref[...]
        for hd in range(MLA_HEADS):
            qh = jnp.dot(qn, wuq_ref[:, hd * HEAD_PAD:(hd + 1) * HEAD_PAD],
                         preferred_element_type=F32)
            q_ref[:, hd * HEAD_PAD:(hd + 1) * HEAD_PAD] = (qh * tq).astype(BF16)


def _proj_attn(h, w_a, g_q, g_kv, w_uq, w_ukv, tab_q, tab_k, *, tm, with_q):
    rows = h.shape[0]
    na = w_a.shape[1]
    hp = MLA_HEADS * HEAD_PAD
    hv = MLA_HEADS * HEAD_PAD
    const = lambda i: (0, 0)
    row = lambda i: (i, 0)
    out_shape = [jax.ShapeDtypeStruct((rows, hp), BF16), jax.ShapeDtypeStruct((rows, hv), BF16)]
    out_specs = [pl.BlockSpec((tm, hp), row), pl.BlockSpec((tm, hv), row)]
    if with_q:
        out_shape = [jax.ShapeDtypeStruct((rows, hp), BF16)] + out_shape
        out_specs = [pl.BlockSpec((tm, hp), row)] + out_specs
    return pl.pallas_call(
        functools.partial(_proj_attn_kernel, with_q=with_q),
        grid=(rows // tm,),
        in_specs=[
            pl.BlockSpec((tm, D_MODEL), row),
            pl.BlockSpec((D_MODEL, na), const),
            pl.BlockSpec((1, Q_LORA), const),
            pl.BlockSpec((1, KV_LORA), const),
            pl.BlockSpec((Q_LORA, hp), const),
            pl.BlockSpec((KV_LORA, MLA_HEADS * (QK_NOPE + V_HEAD)), const),
            pl.BlockSpec((tm, HEAD_PAD), row),
            pl.BlockSpec((tm, 2 * QK_ROPE), row),
        ],
        out_specs=out_specs,
        out_shape=out_shape,
        compiler_params=_cparams(("arbitrary",)),
        name="proj_attn",
    )(h, w_a, g_q.reshape(1, Q_LORA), g_kv.reshape(1, KV_LORA), w_uq, w_ukv, tab_q, tab_k)


ATT_TQ = 2048
ATT_SUB = 512
ATT_KC = 1024
_NT = (((1,), (1,)), ((), ()))


def _attn_kernel(q_ref, kl_ref, kc_ref, vl_ref, vc_ref, o_ref):
    n_sub = ATT_TQ // ATT_SUB
    qs, ms, accs = [], [], []
    for r in range(n_sub):
        q = q_ref[r * ATT_SUB:(r + 1) * ATT_SUB, :]
        s = lax.dot_general(q, kc_ref[...], _NT, preferred_element_type=F32)
        m = jnp.max(s, axis=-1, keepdims=True)
        p = jnp.exp2(s - m)
        qs.append(q)
        ms.append(m)
        accs.append(jnp.dot(p.astype(BF16), vc_ref[...], preferred_element_type=F32))
    for c in range(SEQ // ATT_KC):
        for r in range(n_sub):
            s = lax.dot_general(qs[r], kl_ref[c * ATT_KC:(c + 1) * ATT_KC, :], _NT,
                                preferred_element_type=F32)
            m_new = jnp.maximum(ms[r], jnp.max(s, axis=-1, keepdims=True))
            alpha = jnp.exp2(ms[r] - m_new)
            p = jnp.exp2(s - m_new)
            accs[r] = accs[r] * alpha + jnp.dot(
                p.astype(BF16), vl_ref[c * ATT_KC:(c + 1) * ATT_KC, :],
                preferred_element_type=F32)
            ms[r] = m_new
    for r in range(n_sub):
        acc = accs[r]
        o_ref[r * ATT_SUB:(r + 1) * ATT_SUB, :] = (
            acc[:, :V_HEAD] / acc[:, V_HEAD:V_HEAD + 1]).astype(BF16)


def _attention(q, k_lat, k_ctx, v_lat, v_ctx):
    return pl.pallas_call(
        _attn_kernel,
        grid=(MLA_HEADS, SEQ // ATT_TQ),
        in_specs=[
            pl.BlockSpec((ATT_TQ, HEAD_PAD), lambda h, i: (i, h)),
            pl.BlockSpec((SEQ, HEAD_PAD), lambda h, i: (0, h)),
            pl.BlockSpec((CTX_LEN, HEAD_PAD), lambda h, i: (0, h)),
            pl.BlockSpec((SEQ, HEAD_PAD), lambda h, i: (0, h)),
            pl.BlockSpec((CTX_LEN, HEAD_PAD), lambda h, i: (0, h)),
        ],
        out_specs=pl.BlockSpec((ATT_TQ, V_HEAD), lambda h, i: (i, h)),
        out_shape=jax.ShapeDtypeStruct((SEQ, MLA_HEADS * V_HEAD), BF16),
        compiler_params=_cparams(("arbitrary", "arbitrary")),
        name="attention",
    )(q, k_lat, k_ctx, v_lat, v_ctx)


SC_HALO = 16


def _short_conv_kernel(x_ref, prev_ref, next_ref, w_ref, b_ref, o_ref):
    i = pl.program_id(0)
    tm = x_ref.shape[0]
    x = x_ref[...].astype(F32)
    prev_row = prev_ref[SC_HALO - 1:SC_HALO, :].astype(F32) * (i > 0).astype(F32)
    next_row = next_ref[0:1, :].astype(F32) * (i < pl.num_programs(0) - 1).astype(F32)
    rows = lax.broadcasted_iota(jnp.int32, x.shape, 0)
    up = jnp.where(rows == 0, prev_row, pltpu.roll(x, 1, 0))
    dn = jnp.where(rows == tm - 1, next_row, pltpu.roll(x, tm - 1, 0))
    y = up * w_ref[0:1, :] + x * w_ref[1:2, :] + dn * w_ref[2:3, :] + b_ref[...]
    for blk in range(o_ref.shape[0]):
        o_ref[blk] = y[:, blk * LANES:(blk + 1) * LANES]


def _short_conv(hg, conv_w, conv_b, *, col0):
    tm, tc = 512, 1024
    nb = 3 * HY_WIDTH // tc
    cb0 = col0 // tc
    hb = tm // SC_HALO
    last = SEQ // SC_HALO - 1
    w8 = jnp.zeros((8, 3 * HY_WIDTH), F32).at[:3].set(conv_w)
    return pl.pallas_call(
        _short_conv_kernel,
        grid=(SEQ // tm, nb),
        in_specs=[
            pl.BlockSpec((tm, tc), lambda i, j: (i, j + cb0)),
            pl.BlockSpec((SC_HALO, tc), lambda i, j: (jnp.maximum(i * hb - 1, 0), j + cb0)),
            pl.BlockSpec((SC_HALO, tc), lambda i, j: (jnp.minimum((i + 1) * hb, last), j + cb0)),
            pl.BlockSpec((8, tc), lambda i, j: (0, j)),
            pl.BlockSpec((1, tc), lambda i, j: (0, j)),
        ],
        out_specs=pl.BlockSpec((tc // LANES, tm, LANES), lambda i, j: (j, i, 0)),
        out_shape=jax.ShapeDtypeStruct((3 * HY_WIDTH // LANES, SEQ, LANES), F32),
        compiler_params=_cparams(("arbitrary", "arbitrary")),
        name="short_conv",
    )(hg, hg, hg, w8, conv_b.reshape(1, -1))


HY_ZW = 128


def _dot_split3(a, b):
    a_hi = a.astype(BF16)
    b_hi = b.astype(BF16)
    a_lo = (a - a_hi.astype(F32)).astype(BF16)
    b_lo = (b - b_hi.astype(F32)).astype(BF16)
    dot = functools.partial(jnp.dot, preferred_element_type=F32)
    return dot(a_hi, b_hi) + (dot(a_hi, b_lo) + dot(a_lo, b_hi))


def _filter_kernel(z_ref, w1_ref, b1_ref, w2_ref, b2_ref, w3_ref, fr_ref, dec_ref,
                   h_ref, ss_ref):
    hi = lax.Precision.HIGHEST
    z = z_ref[...]
    fr = fr_ref[...]
    h = jnp.sin(fr * (jnp.dot(z, w1_ref[...], precision=hi, preferred_element_type=F32)
                      + b1_ref[...]))
    h = jnp.sin(fr * (jnp.dot(h, w2_ref[...], precision=hi, preferred_element_type=F32)
                      + b2_ref[...]))
    h = _dot_split3(h, w3_ref[...])
    dist = z[:, HY_EMB:HY_EMB + 1]
    h = h * (jnp.exp(-dist * dec_ref[...]) + HY_MOD_SHIFT)
    for blk in range(h_ref.shape[0]):
        h_ref[blk] = h[:, blk * LANES:(blk + 1) * LANES]

    @pl.when(pl.program_id(0) == 0)
    def _():
        ss_ref[...] = jnp.zeros_like(ss_ref)

    ss_ref[...] += jnp.sum(h * h, axis=0, keepdims=True)


def _hyena_filters(z_p, w1p, b1, w2, b2, w3, freq, decay2):
    tl = 512
    nw = HY_ORDER * HY_WIDTH
    const = lambda i: (0, 0)
    return pl.pallas_call(
        _filter_kernel,
        grid=(SEQ // tl,),
        in_specs=[
            pl.BlockSpec((tl, HY_ZW), lambda i: (i, 0)),
            pl.BlockSpec((HY_ZW, HY_FILTER_W), const),
            pl.BlockSpec((1, HY_FILTER_W), const),
            pl.BlockSpec((HY_FILTER_W, HY_FILTER_W), const),
            pl.BlockSpec((1, HY_FILTER_W), const),
            pl.BlockSpec((HY_FILTER_W, nw), const),
            pl.BlockSpec((1, HY_FILTER_W), const),
            pl.BlockSpec((1, nw), const),
        ],
        out_specs=[pl.BlockSpec((nw // LANES, tl, LANES), lambda i: (0, i, 0)),
                   pl.BlockSpec((1, nw), const)],
        out_shape=[jax.ShapeDtypeStruct((nw // LANES, SEQ, LANES), F32),
                   jax.ShapeDtypeStruct((1, nw), F32)],
        compiler_params=_cparams(("arbitrary",)),
        name="hyena_filters",
    )(z_p, w1p, b1.reshape(1, -1), w2, b2.reshape(1, -1), w3, freq.reshape(1, -1), decay2)


FFT_LANES = 128
FFT_G = 32
FFT_STEPS = FFT_N2 // FFT_G
FFT_ROWS = 2 * FFT_N1
Y_PITCH = FFT_ROWS + 8
T_PITCH = FFT_N2 + 8


def _pad_rows_in(src_ref, dst_scr, scale=None):
    for n1 in range(FFT_HALF):
        v = src_ref[n1 * FFT_N2:(n1 + 1) * FFT_N2, :]
        dst_scr[n1 * T_PITCH:n1 * T_PITCH + FFT_N2, :] = v if scale is None else v * scale


def _fft_stage_a(t, src_scr, ma_ref, y_scr):
    for g in range(FFT_G):
        j = t * FFT_G + g
        xj = src_scr[pl.ds(j, FFT_HALF, stride=T_PITCH), :].astype(BF16)
        xj = jnp.concatenate([xj, jnp.zeros_like(xj)], axis=0)
        a = jnp.dot(ma_ref[g], xj, preferred_element_type=F32)
        y_scr[pl.ds(pl.multiple_of(j * Y_PITCH, 8), FFT_ROWS), :] = a


def _fft_load_pair(tb, g, y_scr):
    cols = []
    for k1 in (tb * FFT_G + g, tb * FFT_G + g + 1):
        re = y_scr[pl.ds(k1, FFT_N2, stride=Y_PITCH), :]
        im = y_scr[pl.ds(FFT_N1 + k1, FFT_N2, stride=Y_PITCH), :]
        cols.append(jnp.concatenate([re, im], axis=0))
    return jnp.concatenate(cols, axis=1).astype(BF16)


def _hyena_spectrum_kernel(h_ref, ss_ref, ma_ref, g2_ref, o_ref, y_scr, h_scr):
    t = pl.program_id(1)

    @pl.when(t == 0)
    def _():
        _pad_rows_in(h_ref, h_scr, lax.rsqrt(ss_ref[...] + NORM_EPS))

    @pl.when(t < FFT_STEPS)
    def _():
        _fft_stage_a(t, h_scr, ma_ref, y_scr)

    @pl.when(t >= FFT_STEPS)
    def _():
        tb = t - FFT_STEPS
        for g in range(0, FFT_G, 2):
            z = jnp.dot(g2_ref[...], _fft_load_pair(tb, g, y_scr), preferred_element_type=F32)
            o_ref[g] = z[:, :FFT_LANES].astype(BF16)
            o_ref[g + 1] = z[:, FFT_LANES:].astype(BF16)


def _hyena_spectrum(filt, sumsq, m_a, g2):
    nb = filt.shape[0]
    last = FFT_STEPS - 1
    return pl.pallas_call(
        _hyena_spectrum_kernel,
        grid=(nb, 2 * FFT_STEPS),
        in_specs=[
            pl.BlockSpec((None, SEQ, FFT_LANES), lambda c, t: (c, 0, 0)),
            pl.BlockSpec((1, FFT_LANES), lambda c, t: (0, c)),
            pl.BlockSpec((FFT_G, FFT_ROWS, LANES), lambda c, t: (jnp.minimum(t, last), 0, 0)),
            pl.BlockSpec((FFT_ROWS, FFT_ROWS), lambda c, t: (0, 0)),
        ],
        out_specs=pl.BlockSpec((None, FFT_G, FFT_ROWS, FFT_LANES),
                               lambda c, t: (c, jnp.maximum(t - FFT_STEPS, 0), 0, 0)),
        out_shape=jax.ShapeDtypeStruct((nb, FFT_N1, FFT_ROWS, FFT_LANES), BF16),
        scratch_shapes=[pltpu.VMEM((FFT_N2 * Y_PITCH, FFT_LANES), F32),
                        pltpu.VMEM((FFT_HALF * T_PITCH, FFT_LANES), F32)],
        compiler_params=_cparams(("arbitrary", "arbitrary")),
        name="hyena_spectrum",
    )(filt, sumsq, m_a, g2)


def _long_conv_kernel(u_ref, x_ref, skip_ref, ma_ref, hf_ref, g2_ref, g2i_ref, mc_ref,
                      o_ref, y_scr, stage_scr, u_scr, x_scr):
    t = pl.program_id(1)

    @pl.when(t == 0)
    def _():
        _pad_rows_in(u_ref, u_scr)
        _pad_rows_in(x_ref, x_scr)

    @pl.when(t < FFT_STEPS)
    def _():
        _fft_stage_a(t, u_scr, ma_ref, y_scr)

    @pl.when((t >= FFT_STEPS) & (t < 2 * FFT_STEPS))
    def _():
        tb = t - FFT_STEPS
        for g in range(0, FFT_G, 2):
            stage_scr[g // 2] = _fft_load_pair(tb, g, y_scr)
        for g in range(0, FFT_G, 2):
            z = jnp.dot(g2_ref[...], stage_scr[g // 2], preferred_element_type=F32)
            hf = jnp.concatenate([hf_ref[g], hf_ref[g + 1]], axis=1).astype(F32)
            zr, zi = z[:FFT_N2], z[FFT_N2:]
            hr, hi = hf[:FFT_N2], hf[FFT_N2:]
            p = jnp.concatenate([zr * hr - zi * hi, zr * hi + zi * hr], axis=0).astype(BF16)
            b = jnp.dot(g2i_ref[...], p, preferred_element_type=F32)
            for half, k1 in enumerate((tb * FFT_G + g, tb * FFT_G + g + 1)):
                lanes = slice(half * FFT_LANES, (half + 1) * FFT_LANES)
                y_scr[pl.ds(k1, FFT_N2, stride=Y_PITCH), :] = b[:FFT_N2, lanes]
                y_scr[pl.ds(FFT_N1 + k1, FFT_N2, stride=Y_PITCH), :] = b[FFT_N2:, lanes]

    @pl.when(t >= 2 * FFT_STEPS)
    def _():
        tc = t - 2 * FFT_STEPS
        for g in range(FFT_G):
            m2 = tc * FFT_G + g
            bm = y_scr[pl.ds(pl.multiple_of(m2 * Y_PITCH, 8), FFT_ROWS), :]
            y = jnp.dot(mc_ref[g], bm.astype(BF16), preferred_element_type=F32)
            rows = pl.ds(m2, FFT_HALF, stride=T_PITCH)
            u_scr[rows, :] = x_scr[rows, :] * (y + skip_ref[...] * u_scr[rows, :])

    @pl.when(t == 3 * FFT_STEPS - 1)
    def _():
        for n1 in range(FFT_HALF):
            o_ref[n1 * FFT_N2:(n1 + 1) * FFT_N2, :] = u_scr[n1 * T_PITCH:n1 * T_PITCH + FFT_N2, :]


def _long_conv_gate(u, ub, x, xb, skip, hf, hf0, m_a, g2, g2i, m_c):
    nb = HY_WIDTH // FFT_LANES
    last = FFT_STEPS - 1
    step = lambda t, phase: jnp.clip(t - phase * FFT_STEPS, 0, last)
    return pl.pallas_call(
        _long_conv_kernel,
        grid=(nb, 3 * FFT_STEPS),
        in_specs=[
            pl.BlockSpec((None, SEQ, FFT_LANES), lambda c, t: (c + ub, 0, 0)),
            pl.BlockSpec((None, SEQ, FFT_LANES), lambda c, t: (c + xb, 0, 0)),
            pl.BlockSpec((1, FFT_LANES), lambda c, t: (0, c)),
            pl.BlockSpec((FFT_G, FFT_ROWS, LANES), lambda c, t: (step(t, 0), 0, 0)),
            pl.BlockSpec((None, FFT_G, FFT_ROWS, FFT_LANES),
                         lambda c, t: (c + hf0, step(t, 1), 0, 0)),
            pl.BlockSpec((FFT_ROWS, FFT_ROWS), lambda c, t: (0, 0)),
            pl.BlockSpec((FFT_ROWS, FFT_ROWS), lambda c, t: (0, 0)),
            pl.BlockSpec((FFT_G, FFT_HALF, FFT_ROWS), lambda c, t: (step(t, 2), 0, 0)),
        ],
        out_specs=pl.BlockSpec((None, SEQ, FFT_LANES), lambda c, t: (c, 0, 0)),
        out_shape=jax.ShapeDtypeStruct((nb, SEQ, FFT_LANES), F32),
        scratch_shapes=[pltpu.VMEM((FFT_N2 * Y_PITCH, FFT_LANES), F32),
                        pltpu.VMEM((FFT_G // 2, FFT_ROWS, 2 * FFT_LANES), BF16),
                        pltpu.VMEM((FFT_HALF * T_PITCH, FFT_LANES), F32),
                        pltpu.VMEM((FFT_HALF * T_PITCH, FFT_LANES), F32)],
        compiler_params=_cparams(("arbitrary", "arbitrary")),
        name="long_conv_gate",
    )(u, x, skip.reshape(1, HY_WIDTH), m_a, hf, g2, g2i, m_c)


def _merge_kernel(a_ref, hy_ref, ga_ref, gh_ref, wa_ref, wh_ref, o_ref):
    ya = jnp.dot(a_ref[...], wa_ref[...], preferred_element_type=F32)
    hy = jnp.concatenate([hy_ref[blk].astype(BF16) for blk in range(hy_ref.shape[0])], axis=1)
    yh = jnp.dot(hy, wh_ref[...], preferred_element_type=F32)
    o_ref[...] = (ga_ref[...].astype(F32) * ya + gh_ref[...].astype(F32) * yh).astype(BF16)


def _merge(attn, hy, hg, gate_col0, w_attn_o, w_hy_o):
    tm, tn = 1024, 1024
    ga0 = gate_col0 // tn
    gh0 = (gate_col0 + D_MODEL) // tn
    row = lambda i, j: (i, 0)
    return pl.pallas_call(
        _merge_kernel,
        grid=(SEQ // tm, D_MODEL // tn),
        in_specs=[
            pl.BlockSpec((tm, MLA_HEADS * V_HEAD), row),
            pl.BlockSpec((HY_WIDTH // LANES, tm, LANES), lambda i, j: (0, i, 0)),
            pl.BlockSpec((tm, tn), lambda i, j: (i, j + ga0)),
            pl.BlockSpec((tm, tn), lambda i, j: (i, j + gh0)),
            pl.BlockSpec((MLA_HEADS * V_HEAD, tn), lambda i, j: (0, j)),
            pl.BlockSpec((HY_WIDTH, tn), lambda i, j: (0, j)),
        ],
        out_specs=pl.BlockSpec((tm, tn), lambda i, j: (i, j)),
        out_shape=jax.ShapeDtypeStruct((SEQ, D_MODEL), BF16),
        compiler_params=_cparams(("arbitrary", "arbitrary")),
        name="merge_branches",
    )(attn, hy, hg, hg, w_attn_o, w_hy_o)


def _out_proj_kernel(y_ref, w_ref, s_ref, mod_ref, o_ref):
    xm = jnp.dot(y_ref[...], w_ref[...], preferred_element_type=F32)
    o_ref[...] = s_ref[...] + mod_ref[0:1, :] * xm


def _out_proj(y, w_out, s, mod_gate):
    tm = 512
    row = lambda i: (i, 0)
    const = lambda i: (0, 0)
    return pl.pallas_call(
        _out_proj_kernel,
        grid=(SEQ // tm,),
        in_specs=[
            pl.BlockSpec((tm, D_MODEL), row),
            pl.BlockSpec((D_MODEL, D_MODEL), const),
            pl.BlockSpec((tm, D_MODEL), row),
            pl.BlockSpec((8, D_MODEL), const),
        ],
        out_specs=pl.BlockSpec((tm, D_MODEL), row),
        out_shape=jax.ShapeDtypeStruct((SEQ, D_MODEL), F32),
        compiler_params=_cparams(("arbitrary",)),
        name="out_proj",
    )(y, w_out, s, mod_gate)


def _rope_tables():
    t = np.arange(SEQ)
    pos = np.stack([t // GRID_W, t % GRID_W], axis=1).astype(np.float64)
    inv_freq = ROPE_THETA ** (-np.arange(0, ROPE_AXIS, 2, dtype=np.float64) / ROPE_AXIS)
    i = np.arange(QK_ROPE)
    ang = pos[:, i // ROPE_AXIS] * inv_freq[i % (ROPE_AXIS // 2)][None, :]
    cos, sin = np.cos(ang), np.sin(ang)
    tab_q = np.concatenate([np.ones((SEQ, QK_NOPE)), cos, sin], axis=1) * (ATTN_SCALE * math.log2(math.e))
    tab_k = np.concatenate([cos, sin], axis=1)
    tab_k_ctx = np.concatenate([np.ones((CTX_LEN, QK_ROPE)), np.zeros((CTX_LEN, QK_ROPE))], axis=1)
    return (jnp.asarray(tab_q, F32), jnp.asarray(tab_k, F32), jnp.asarray(tab_k_ctx, F32))


def _rope_swap():
    i = np.arange(QK_ROPE)
    first_half = (i % ROPE_AXIS) < ROPE_AXIS // 2
    partner = np.where(first_half, i + ROPE_AXIS // 2, i - ROPE_AXIS // 2)
    sign = np.where(first_half, -1.0, 1.0)
    return partner, sign


def _filter_features():
    pos = np.arange(SEQ, dtype=np.float64)[:, None]
    t01 = pos / (SEQ - 1)
    bands = np.linspace(1e-4, HY_BANDS - 1, HY_BANDS)[None, :]
    ang = bands * (2.0 * math.pi / SEQ) * pos
    dist = np.abs(pos - (SEQ // 2)) / (SEQ / 2.0)
    z = np.concatenate([t01, np.cos(ang), -np.sin(ang), dist], axis=1)
    z = np.pad(z, ((0, 0), (0, HY_ZW - z.shape[1])))
    return jnp.asarray(z, F32)


def _dft_tables():
    n1 = np.arange(FFT_HALF)
    k1 = np.arange(FFT_N1)
    n2 = np.arange(FFT_N2)
    n = FFT_N2 * n1[None, None, :] + n2[:, None, None]
    ph = (k1[None, :, None] * n) % FFT_N
    th = 2.0 * math.pi * ph / FFT_N
    m_a = np.concatenate([np.cos(th), -np.sin(th)], axis=1)
    m_a = np.pad(m_a, ((0, 0), (0, 0), (0, LANES - FFT_HALF)))
    ph2 = (n2[:, None] * n2[None, :]) % FFT_N2
    th2 = 2.0 * math.pi * ph2 / FFT_N2
    c, s = np.cos(th2), np.sin(th2)
    g2 = np.block([[c, s], [-s, c]])
    g2i = np.block([[c, -s], [s, c]])
    n = FFT_N2 * (n1[None, :, None] + FFT_HALF // 2) + n2[:, None, None]
    ph = (k1[None, None, :] * n) % FFT_N
    th = 2.0 * math.pi * ph / FFT_N
    m_c = np.concatenate([np.cos(th), -np.sin(th)], axis=2) / FFT_N
    return tuple(jnp.asarray(t, F32).astype(BF16) for t in (m_a, g2, g2i, m_c))


def _rows8(m, lo, hi):
    return jnp.zeros((8, D_MODEL), F32).at[:hi - lo].set(m[lo:hi])


def kernel(x, c, ctx, c_ctx, w_mod, b_mod, g_ffn1, w13_ffn1, w2_ffn1, g_mix, w_in, g_q, w_uq,
           g_kv, w_ukv, w_attn_o, hy_conv_w, hy_conv_b, hy_w1, hy_b1, hy_w2, hy_b2, hy_w3,
           hy_freq, hy_skip, w_hy_o, w_out, g_ffn2, w13_ffn2, w2_ffn2, g_final):
    xs = x[0]
    cs = ctx[0]
    li = 0

    c8 = jnp.zeros((8, D_MODEL), F32).at[0].set(c[0]).at[1].set(c_ctx)
    mod = _modulation(c8, w_mod[li], b_mod[li])
    mx = mod[0].reshape(N_MOD, D_MODEL)
    mc = mod[1].reshape(N_MOD, D_MODEL)

    w13 = w13_ffn1[li]
    w2 = w2_ffn1[li]
    x1 = _half_ffn(xs, _rows8(mx, 0, 3), g_ffn1[li], w13, w2, g_final, tm=FFN_TM, final_norm=False)
    c1 = _half_ffn(cs, _rows8(mc, 0, 3), g_ffn1[li], w13, w2, g_final, tm=CTX_LEN,
                   final_norm=False)

    hx = _prenorm(x1, _rows8(mx, 3, 5), g_mix[li], tm=512)
    hc = _prenorm(c1, _rows8(mc, 3, 5), g_mix[li], tm=CTX_LEN)

    win = w_in[li]
    partner, sign = _rope_swap()
    w_kr = win[:, KV_END:KR_END]
    w_a = jnp.concatenate([win[:, :KV_END], w_kr, w_kr[:, partner] * sign], axis=1).astype(BF16)
    wq = w_uq[li].reshape(Q_LORA, MLA_HEADS, QK_NOPE + QK_ROPE)
    wq_r = wq[:, :, QK_NOPE:]
    wq_p = jnp.concatenate([wq[:, :, :QK_NOPE], wq_r, wq_r[:, :, partner] * sign], axis=2)
    wq_p = wq_p.reshape(Q_LORA, MLA_HEADS * HEAD_PAD).astype(BF16)
    wkv = w_ukv[li].reshape(KV_LORA, MLA_HEADS, QK_NOPE + V_HEAD)
    wkv_p = jnp.concatenate([wkv[:, :, :QK_NOPE].reshape(KV_LORA, -1),
                             wkv[:, :, QK_NOPE:].reshape(KV_LORA, -1)], axis=1).astype(BF16)

    tab_q, tab_k, tab_k_ctx = _rope_tables()
    q, k_lat, v_lat = _proj_attn(hx, w_a, g_q[li], g_kv[li], wq_p, wkv_p, tab_q, tab_k,
                                 tm=512, with_q=True)
    k_ctx, v_ctx = _proj_attn(hc, w_a, g_q[li], g_kv[li], wq_p, wkv_p, tab_q[:CTX_LEN],
                              tab_k_ctx, tm=CTX_LEN, with_q=False)
    attn = _attention(q, k_lat, k_ctx, v_lat, v_ctx)

    hg = _proj_hg(hx, win, KR_END)
    u3 = _short_conv(hg, hy_conv_w[li], hy_conv_b[li], col0=0)
    w1p = jnp.zeros((HY_ZW, HY_FILTER_W), F32).at[:HY_EMB].set(hy_w1[li])
    decay = np.abs(np.linspace(HY_MIN_DECAY, HY_MAX_DECAY, HY_WIDTH))
    decay2 = jnp.asarray(np.tile(decay, HY_ORDER)[None, :], F32)
    filt, sumsq = _hyena_filters(_filter_features(), w1p, hy_b1[li], hy_w2[li], hy_b2[li],
                                 hy_w3[li], hy_freq[li], decay2)
    m_a, g2, g2i, m_c = _dft_tables()
    hf = _hyena_spectrum(filt, sumsq, m_a, g2)
    nb = HY_WIDTH // FFT_LANES
    z = _long_conv_gate(u3, 0, u3, nb, hy_skip[li][0], hf, 0, m_a, g2, g2i, m_c)
    hy = _long_conv_gate(z, 0, u3, 2 * nb, hy_skip[li][1], hf, nb, m_a, g2, g2i, m_c)

    y = _merge(attn, hy, hg, 3 * HY_WIDTH, w_attn_o[li].astype(BF16), w_hy_o[li].astype(BF16))
    x2 = _out_proj(y, w_out[li].astype(BF16), x1, _rows8(mx, 5, 6))

    out = _half_ffn(x2, _rows8(mx, 6, 9), g_ffn2[li], w13_ffn2[li], w2_ffn2[li], g_final,
                    tm=FFN_TM, final_norm=True)
    return out[None]
```

```python
import functools
import math

import numpy as np
import jax
import jax.numpy as jnp
from jax import lax
from jax.experimental import pallas as pl
from jax.experimental.pallas import tpu as pltpu

F32 = jnp.float32
BF16 = jnp.bfloat16

D_MODEL = 2048
SEQ = 8192
GRID_W = 64
CTX_LEN = 256
N_MOD = 9
D_FF = 5632
NORM_EPS = 1e-6

MLA_HEADS = 16
Q_LORA = 512
KV_LORA = 512
QK_NOPE = 128
QK_ROPE = 64
V_HEAD = 128
ROPE_AXIS = QK_ROPE // 2
ROPE_THETA = 10000.0
ATTN_SCALE = (QK_NOPE + QK_ROPE) ** -0.5
HEAD_PAD = 256

HY_WIDTH = 1024
HY_ORDER = 2
HY_EMB = 33
HY_BANDS = (HY_EMB - 1) // 2
HY_FILTER_W = 64
HY_TARGET = 1e-2
HY_FAST_DECAY = 0.3
HY_SLOW_DECAY = 1.5
HY_MAX_DECAY = math.log(HY_TARGET) / HY_FAST_DECAY
HY_MIN_DECAY = math.log(HY_TARGET) / HY_SLOW_DECAY
HY_MOD_SHIFT = 0.05

Q_END = Q_LORA
KV_END = Q_END + KV_LORA
KR_END = KV_END + QK_ROPE
HY_END = KR_END + 3 * HY_WIDTH

FFT_N = 2 * SEQ
FFT_N1 = 128
FFT_N2 = 128
FFT_HALF = SEQ // FFT_N2

V7X_VMEM_LIMIT = 60 * 1024 * 1024
LANES = 128


def _cparams(sem, vmem=V7X_VMEM_LIMIT):
    return pltpu.CompilerParams(dimension_semantics=sem, vmem_limit_bytes=vmem)


def _sigmoid(x):
    return 1.0 / (1.0 + jnp.exp(-x))


def _rms(x, g):
    var = jnp.mean(x * x, axis=-1, keepdims=True)
    return x * lax.rsqrt(var + NORM_EPS) * g


def _mod_kernel(c_ref, w_ref, b_ref, o_ref):
    c = c_ref[...]
    a = c * _sigmoid(c)
    o_ref[...] = jnp.dot(a, w_ref[...], preferred_element_type=F32) + b_ref[...]


def _modulation(c8, w_mod, b_mod):
    n = w_mod.shape[1]
    tn = 1024
    return pl.pallas_call(
        _mod_kernel,
        grid=(n // tn,),
        in_specs=[
            pl.BlockSpec((8, D_MODEL), lambda j: (0, 0)),
            pl.BlockSpec((D_MODEL, tn), lambda j: (0, j)),
            pl.BlockSpec((1, tn), lambda j: (0, j)),
        ],
        out_specs=pl.BlockSpec((8, tn), lambda j: (0, j)),
        out_shape=jax.ShapeDtypeStruct((8, n), F32),
        compiler_params=_cparams(("arbitrary",)),
        name="modulation",
    )(c8, w_mod, b_mod.reshape(1, n))


def _ffn_kernel(s_ref, mod_ref, g_ref, w1_ref, w3_ref, w2_ref, gf_ref, o_ref, h_scr,
                *, final_norm):
    f = pl.program_id(1)

    @pl.when(f == 0)
    def _():
        h = _rms(s_ref[...], g_ref[...])
        h = h * (1.0 + mod_ref[1:2, :]) + mod_ref[0:1, :]
        h_scr[...] = h.astype(BF16)
        o_ref[...] = jnp.zeros_like(o_ref)

    h = h_scr[...]
    a = jnp.dot(h, w1_ref[...].astype(BF16), preferred_element_type=F32)
    b = jnp.dot(h, w3_ref[...].astype(BF16), preferred_element_type=F32)
    act = (a * _sigmoid(a) * b).astype(BF16)
    o_ref[...] += jnp.dot(act, w2_ref[...].astype(BF16), preferred_element_type=F32)

    @pl.when(f == pl.num_programs(1) - 1)
    def _():
        out = s_ref[...] + 0.5 * mod_ref[2:3, :] * o_ref[...]
        if final_norm:
            out = _rms(out, gf_ref[...])
        o_ref[...] = out


FFN_TM = 1024


FFN_TF = 256


def _half_ffn(s, mod3, g, w13, w2, g_final, *, tm, final_norm):
    rows = s.shape[0]
    tf = FFN_TF
    nf = D_FF // tf
    return pl.pallas_call(
        functools.partial(_ffn_kernel, final_norm=final_norm),
        grid=(rows // tm, nf),
        in_specs=[
            pl.BlockSpec((tm, D_MODEL), lambda i, f: (i, 0)),
            pl.BlockSpec((8, D_MODEL), lambda i, f: (0, 0)),
            pl.BlockSpec((1, D_MODEL), lambda i, f: (0, 0)),
            pl.BlockSpec((D_MODEL, tf), lambda i, f: (0, f)),
            pl.BlockSpec((D_MODEL, tf), lambda i, f: (0, f + nf)),
            pl.BlockSpec((tf, D_MODEL), lambda i, f: (f, 0)),
            pl.BlockSpec((1, D_MODEL), lambda i, f: (0, 0)),
        ],
        out_specs=pl.BlockSpec((tm, D_MODEL), lambda i, f: (i, 0)),
        out_shape=jax.ShapeDtypeStruct((rows, D_MODEL), F32),
        scratch_shapes=[pltpu.VMEM((tm, D_MODEL), BF16)],
        compiler_params=_cparams(("arbitrary", "arbitrary")),
        name="half_ffn",
    )(s, mod3, g.reshape(1, D_MODEL), w13, w13, w2, g_final.reshape(1, D_MODEL))


def _prenorm_kernel(s_ref, mod_ref, g_ref, o_ref):
    h = _rms(s_ref[...], g_ref[...])
    o_ref[...] = (h * (1.0 + mod_ref[1:2, :]) + mod_ref[0:1, :]).astype(BF16)


def _prenorm(s, mod3, g, *, tm):
    rows = s.shape[0]
    return pl.pallas_call(
        _prenorm_kernel,
        grid=(rows // tm,),
        in_specs=[
            pl.BlockSpec((tm, D_MODEL), lambda i: (i, 0)),
            pl.BlockSpec((8, D_MODEL), lambda i: (0, 0)),
            pl.BlockSpec((1, D_MODEL), lambda i: (0, 0)),
        ],
        out_specs=pl.BlockSpec((tm, D_MODEL), lambda i: (i, 0)),
        out_shape=jax.ShapeDtypeStruct((rows, D_MODEL), BF16),
        compiler_params=_cparams(("arbitrary",)),
        name="prenorm",
    )(s, mod3, g.reshape(1, D_MODEL))


def _proj_hg_kernel(h_ref, wa_ref, wb_ref, o_ref, *, n_plain_tiles, lane_off):
    j = pl.program_id(1)
    w = jnp.concatenate([wa_ref[:, lane_off:], wb_ref[:, :lane_off]], axis=1).astype(BF16)
    acc = jnp.dot(h_ref[...], w, preferred_element_type=F32)
    o_ref[...] = jnp.where(j >= n_plain_tiles, _sigmoid(acc), acc).astype(BF16)


def _proj_hg(h, w_in, col0):
    rows = h.shape[0]
    n = 3 * HY_WIDTH + 2 * D_MODEL
    tm, tn = 1024, 1024
    blk0, lane_off = divmod(col0, tn)
    assert 0 < lane_off < LANES and col0 + n <= w_in.shape[1]
    return pl.pallas_call(
        functools.partial(_proj_hg_kernel, n_plain_tiles=3 * HY_WIDTH // tn, lane_off=lane_off),
        grid=(rows // tm, n // tn),
        in_specs=[
            pl.BlockSpec((tm, D_MODEL), lambda i, j: (i, 0)),
            pl.BlockSpec((D_MODEL, tn), lambda i, j: (0, j + blk0)),
            pl.BlockSpec((D_MODEL, LANES), lambda i, j: (0, (j + blk0 + 1) * (tn // LANES))),
        ],
        out_specs=pl.BlockSpec((tm, tn), lambda i, j: (i, j)),
        out_shape=jax.ShapeDtypeStruct((rows, n), BF16),
        compiler_params=_cparams(("arbitrary", "arbitrary")),
        name="proj_gate_hyena",
    )(h, w_in, w_in)


def _proj_attn_kernel(h_ref, win_ref, swap_ref, gq_ref, gkv_ref, wuq_ref, wukv_ref, tq_ref,
                      tk_ref, *refs, with_q):
    if with_q:
        q_ref, k_ref, v_ref, wa_scr = refs
    else:
        k_ref, v_ref, wa_scr = refs

    @pl.when(pl.program_id(0) == 0)
    def _():
        wa_scr[...] = win_ref[...].astype(BF16)

    p = jnp.dot(h_ref[...], wa_scr[...], preferred_element_type=F32)
    kvn = _rms(p[:, Q_LORA:Q_LORA + KV_LORA], gkv_ref[...]).astype(BF16)
    kr = p[:, KV_END:KV_END + 2 * QK_ROPE]
    kr_hi = kr.astype(BF16)
    kr_lo = (kr - kr_hi.astype(F32)).astype(BF16)
    kr2 = (jnp.dot(kr_hi, swap_ref[...], preferred_element_type=F32)
           + jnp.dot(kr_lo, swap_ref[...], preferred_element_type=F32))
    t = kr2 * tk_ref[...]
    krot = (t + pltpu.roll(t, QK_ROPE, 1)).astype(BF16)
    kn = jnp.dot(kvn, wukv_ref[:, :MLA_HEADS * QK_NOPE], preferred_element_type=F32)
    for hd in range(MLA_HEADS):
        k_ref[:, hd * HEAD_PAD:hd * HEAD_PAD + QK_NOPE] = (
            kn[:, hd * QK_NOPE:(hd + 1) * QK_NOPE].astype(BF16))
        k_ref[:, hd * HEAD_PAD + QK_NOPE:(hd + 1) * HEAD_PAD] = krot
    vv = jnp.dot(kvn, wukv_ref[:, MLA_HEADS * QK_NOPE:], preferred_element_type=F32)
    lane = lax.broadcasted_iota(jnp.int32, (h_ref.shape[0], HEAD_PAD - V_HEAD), 1)
    unit = jnp.where(lane == 0, 1.0, 0.0).astype(BF16)
    for hd in range(MLA_HEADS):
        v_ref[:, hd * HEAD_PAD:hd * HEAD_PAD + V_HEAD] = (
            vv[:, hd * V_HEAD:(hd + 1) * V_HEAD].astype(BF16))
        v_ref[:, hd * HEAD_PAD + V_HEAD:(hd + 1) * HEAD_PAD] = unit
    if with_q:
        qn = _rms(p[:, :Q_LORA], gq_ref[...]).astype(BF16)
        tq = tq_ref[...]
        for hd in range(MLA_HEADS):
            qh = jnp.dot(qn, wuq_ref[:, hd * HEAD_PAD:(hd + 1) * HEAD_PAD],
                         preferred_element_type=F32)
            q_ref[:, hd * HEAD_PAD:(hd + 1) * HEAD_PAD] = (qh * tq).astype(BF16)


def _proj_attn(h, w_in, swap_mat, g_q, g_kv, w_uq, w_ukv, tab_q, tab_k, *, tm, with_q):
    rows = h.shape[0]
    na = KV_END + 2 * QK_ROPE
    once = dict(pipeline_mode=pl.Buffered(1))
    hp = MLA_HEADS * HEAD_PAD
    hv = MLA_HEADS * HEAD_PAD
    const = lambda i: (0, 0)
    row = lambda i: (i, 0)
    out_shape = [jax.ShapeDtypeStruct((rows, hp), BF16), jax.ShapeDtypeStruct((rows, hv), BF16)]
    out_specs = [pl.BlockSpec((tm, hp), row), pl.BlockSpec((tm, hv), row)]
    if with_q:
        out_shape = [jax.ShapeDtypeStruct((rows, hp), BF16)] + out_shape
        out_specs = [pl.BlockSpec((tm, hp), row)] + out_specs
    return pl.pallas_call(
        functools.partial(_proj_attn_kernel, with_q=with_q),
        grid=(rows // tm,),
        in_specs=[
            pl.BlockSpec((tm, D_MODEL), row),
            pl.BlockSpec((D_MODEL, na), const, **once),
            pl.BlockSpec((2 * QK_ROPE, 2 * QK_ROPE), const),
            pl.BlockSpec((1, Q_LORA), const),
            pl.BlockSpec((1, KV_LORA), const),
            pl.BlockSpec((Q_LORA, hp), const, **once),
            pl.BlockSpec((KV_LORA, MLA_HEADS * (QK_NOPE + V_HEAD)), const, **once),
            pl.BlockSpec((tm, HEAD_PAD), row),
            pl.BlockSpec((tm, 2 * QK_ROPE), row),
        ],
        out_specs=out_specs,
        out_shape=out_shape,
        scratch_shapes=[pltpu.VMEM((D_MODEL, na), BF16)],
        compiler_params=_cparams(("arbitrary",)),
        name="proj_attn",
    )(h, w_in, swap_mat, g_q.reshape(1, Q_LORA), g_kv.reshape(1, KV_LORA), w_uq, w_ukv,
      tab_q, tab_k)


ATT_TQ = 2048
ATT_SUB = 512
ATT_KC = 1024
_NT = (((1,), (1,)), ((), ()))


def _attn_kernel(q_ref, kl_ref, kc_ref, vl_ref, vc_ref, o_ref):
    n_sub = ATT_TQ // ATT_SUB
    qs, ms, accs = [], [], []
    for r in range(n_sub):
        q = q_ref[r * ATT_SUB:(r + 1) * ATT_SUB, :]
        s = lax.dot_general(q, kc_ref[...], _NT, preferred_element_type=F32)
        m = jnp.max(s, axis=-1, keepdims=True)
        p = jnp.exp2(s - m)
        qs.append(q)
        ms.append(m)
        accs.append(jnp.dot(p.astype(BF16), vc_ref[...], preferred_element_type=F32))
    for c in range(SEQ // ATT_KC):
        for r in range(n_sub):
            s = lax.dot_general(qs[r], kl_ref[c * ATT_KC:(c + 1) * ATT_KC, :], _NT,
                                preferred_element_type=F32)
            m_new = jnp.maximum(ms[r], jnp.max(s, axis=-1, keepdims=True))
            alpha = jnp.exp2(ms[r] - m_new)
            p = jnp.exp2(s - m_new)
            accs[r] = accs[r] * alpha + jnp.dot(
                p.astype(BF16), vl_ref[c * ATT_KC:(c + 1) * ATT_KC, :],
                preferred_element_type=F32)
            ms[r] = m_new
    for r in range(n_sub):
        acc = accs[r]
        o_ref[r * ATT_SUB:(r + 1) * ATT_SUB, :] = (
            acc[:, :V_HEAD] / acc[:, V_HEAD:V_HEAD + 1]).astype(BF16)


def _attention(q, k_lat, k_ctx, v_lat, v_ctx):
    return pl.pallas_call(
        _attn_kernel,
        grid=(MLA_HEADS, SEQ // ATT_TQ),
        in_specs=[
            pl.BlockSpec((ATT_TQ, HEAD_PAD), lambda h, i: (i, h)),
            pl.BlockSpec((SEQ, HEAD_PAD), lambda h, i: (0, h)),
            pl.BlockSpec((CTX_LEN, HEAD_PAD), lambda h, i: (0, h)),
            pl.BlockSpec((SEQ, HEAD_PAD), lambda h, i: (0, h)),
            pl.BlockSpec((CTX_LEN, HEAD_PAD), lambda h, i: (0, h)),
        ],
        out_specs=pl.BlockSpec((ATT_TQ, V_HEAD), lambda h, i: (i, h)),
        out_shape=jax.ShapeDtypeStruct((SEQ, MLA_HEADS * V_HEAD), BF16),
        compiler_params=_cparams(("arbitrary", "arbitrary")),
        name="attention",
    )(q, k_lat, k_ctx, v_lat, v_ctx)


SC_HALO = 16


def _short_conv_kernel(x_ref, prev_ref, next_ref, w_ref, b_ref, o_ref):
    i = pl.program_id(0)
    tm = x_ref.shape[0]
    x = x_ref[...].astype(F32)
    prev_row = prev_ref[SC_HALO - 1:SC_HALO, :].astype(F32) * (i > 0).astype(F32)
    next_row = next_ref[0:1, :].astype(F32) * (i < pl.num_programs(0) - 1).astype(F32)
    rows = lax.broadcasted_iota(jnp.int32, x.shape, 0)
    up = jnp.where(rows == 0, prev_row, pltpu.roll(x, 1, 0))
    dn = jnp.where(rows == tm - 1, next_row, pltpu.roll(x, tm - 1, 0))
    y = up * w_ref[0:1, :] + x * w_ref[1:2, :] + dn * w_ref[2:3, :] + b_ref[...]
    for blk in range(o_ref.shape[0]):
        o_ref[blk] = y[:, blk * LANES:(blk + 1) * LANES]


def _short_conv(hg, conv_w, conv_b, *, col0):
    tm, tc = 512, 1024
    nb = 3 * HY_WIDTH // tc
    cb0 = col0 // tc
    hb = tm // SC_HALO
    last = SEQ // SC_HALO - 1
    w8 = jnp.zeros((8, 3 * HY_WIDTH), F32).at[:3].set(conv_w)
    return pl.pallas_call(
        _short_conv_kernel,
        grid=(SEQ // tm, nb),
        in_specs=[
            pl.BlockSpec((tm, tc), lambda i, j: (i, j + cb0)),
            pl.BlockSpec((SC_HALO, tc), lambda i, j: (jnp.maximum(i * hb - 1, 0), j + cb0)),
            pl.BlockSpec((SC_HALO, tc), lambda i, j: (jnp.minimum((i + 1) * hb, last), j + cb0)),
            pl.BlockSpec((8, tc), lambda i, j: (0, j)),
            pl.BlockSpec((1, tc), lambda i, j: (0, j)),
        ],
        out_specs=pl.BlockSpec((tc // LANES, tm, LANES), lambda i, j: (j, i, 0)),
        out_shape=jax.ShapeDtypeStruct((3 * HY_WIDTH // LANES, SEQ, LANES), F32),
        compiler_params=_cparams(("arbitrary", "arbitrary")),
        name="short_conv",
    )(hg, hg, hg, w8, conv_b.reshape(1, -1))


HY_ZW = 128


def _dot_split3(a, b):
    a_hi = a.astype(BF16)
    b_hi = b.astype(BF16)
    a_lo = (a - a_hi.astype(F32)).astype(BF16)
    b_lo = (b - b_hi.astype(F32)).astype(BF16)
    dot = functools.partial(jnp.dot, preferred_element_type=F32)
    return dot(a_hi, b_hi) + (dot(a_hi, b_lo) + dot(a_lo, b_hi))


def _filter_kernel(z_ref, w1_ref, b1_ref, w2_ref, b2_ref, w3_ref, fr_ref, dec_ref,
                   h_ref, ss_ref):
    hi = lax.Precision.HIGHEST
    z = z_ref[...]
    fr = fr_ref[...]
    h = jnp.sin(fr * (jnp.dot(z, w1_ref[...], precision=hi, preferred_element_type=F32)
                      + b1_ref[...]))
    h = jnp.sin(fr * (jnp.dot(h, w2_ref[...], precision=hi, preferred_element_type=F32)
                      + b2_ref[...]))
    h = _dot_split3(h, w3_ref[...])
    dist = z[:, HY_EMB:HY_EMB + 1]
    h = h * (jnp.exp(-dist * dec_ref[...]) + HY_MOD_SHIFT)
    for blk in range(h_ref.shape[0]):
        h_ref[blk] = h[:, blk * LANES:(blk + 1) * LANES]

    @pl.when(pl.program_id(0) == 0)
    def _():
        ss_ref[...] = jnp.zeros_like(ss_ref)

    ss_ref[...] += jnp.sum(h * h, axis=0, keepdims=True)


def _hyena_filters(z_p, w1p, b1, w2, b2, w3, freq, decay2):
    tl = 512
    nw = HY_ORDER * HY_WIDTH
    const = lambda i: (0, 0)
    return pl.pallas_call(
        _filter_kernel,
        grid=(SEQ // tl,),
        in_specs=[
            pl.BlockSpec((tl, HY_ZW), lambda i: (i, 0)),
            pl.BlockSpec((HY_ZW, HY_FILTER_W), const),
            pl.BlockSpec((1, HY_FILTER_W), const),
            pl.BlockSpec((HY_FILTER_W, HY_FILTER_W), const),
            pl.BlockSpec((1, HY_FILTER_W), const),
            pl.BlockSpec((HY_FILTER_W, nw), const),
            pl.BlockSpec((1, HY_FILTER_W), const),
            pl.BlockSpec((1, nw), const),
        ],
        out_specs=[pl.BlockSpec((nw // LANES, tl, LANES), lambda i: (0, i, 0)),
                   pl.BlockSpec((1, nw), const)],
        out_shape=[jax.ShapeDtypeStruct((nw // LANES, SEQ, LANES), F32),
                   jax.ShapeDtypeStruct((1, nw), F32)],
        compiler_params=_cparams(("arbitrary",)),
        name="hyena_filters",
    )(z_p, w1p, b1.reshape(1, -1), w2, b2.reshape(1, -1), w3, freq.reshape(1, -1), decay2)


FFT_LANES = 128
FFT_G = 32
FFT_STEPS = FFT_N2 // FFT_G
FFT_ROWS = 2 * FFT_N1
Y_PITCH = FFT_ROWS + 8
T_PITCH = FFT_N2 + 8


def _pad_rows_in(src_ref, dst_scr, scale=None):
    for n1 in range(FFT_HALF):
        v = src_ref[n1 * FFT_N2:(n1 + 1) * FFT_N2, :]
        dst_scr[n1 * T_PITCH:n1 * T_PITCH + FFT_N2, :] = v if scale is None else v * scale


def _fft_stage_a(t, src_scr, ma_ref, y_scr):
    for g in range(FFT_G):
        j = t * FFT_G + g
        xj = src_scr[pl.ds(j, FFT_HALF, stride=T_PITCH), :].astype(BF16)
        xj = jnp.concatenate([xj, jnp.zeros_like(xj)], axis=0)
        a = jnp.dot(ma_ref[g], xj, preferred_element_type=F32)
        y_scr[pl.ds(pl.multiple_of(j * Y_PITCH, 8), FFT_ROWS), :] = a


def _fft_load_pair(tb, g, y_scr):
    cols = []
    for k1 in (tb * FFT_G + g, tb * FFT_G + g + 1):
        re = y_scr[pl.ds(k1, FFT_N2, stride=Y_PITCH), :]
        im = y_scr[pl.ds(FFT_N1 + k1, FFT_N2, stride=Y_PITCH), :]
        cols.append(jnp.concatenate([re, im], axis=0))
    return jnp.concatenate(cols, axis=1).astype(BF16)


def _hyena_spectrum_kernel(h_ref, ss_ref, ma_ref, g2_ref, o_ref, y_scr, h_scr):
    t = pl.program_id(1)

    @pl.when(t == 0)
    def _():
        _pad_rows_in(h_ref, h_scr, lax.rsqrt(ss_ref[...] + NORM_EPS))

    @pl.when(t < FFT_STEPS)
    def _():
        _fft_stage_a(t, h_scr, ma_ref, y_scr)

    @pl.when(t >= FFT_STEPS)
    def _():
        tb = t - FFT_STEPS
        for g in range(0, FFT_G, 2):
            z = jnp.dot(g2_ref[...], _fft_load_pair(tb, g, y_scr), preferred_element_type=F32)
            o_ref[g] = z[:, :FFT_LANES].astype(BF16)
            o_ref[g + 1] = z[:, FFT_LANES:].astype(BF16)


def _hyena_spectrum(filt, sumsq, m_a, g2):
    nb = filt.shape[0]
    last = FFT_STEPS - 1
    return pl.pallas_call(
        _hyena_spectrum_kernel,
        grid=(nb, 2 * FFT_STEPS),
        in_specs=[
            pl.BlockSpec((None, SEQ, FFT_LANES), lambda c, t: (c, 0, 0)),
            pl.BlockSpec((1, FFT_LANES), lambda c, t: (0, c)),
            pl.BlockSpec((FFT_G, FFT_ROWS, LANES), lambda c, t: (jnp.minimum(t, last), 0, 0)),
            pl.BlockSpec((FFT_ROWS, FFT_ROWS), lambda c, t: (0, 0)),
        ],
        out_specs=pl.BlockSpec((None, FFT_G, FFT_ROWS, FFT_LANES),
                               lambda c, t: (c, jnp.maximum(t - FFT_STEPS, 0), 0, 0)),
        out_shape=jax.ShapeDtypeStruct((nb, FFT_N1, FFT_ROWS, FFT_LANES), BF16),
        scratch_shapes=[pltpu.VMEM((FFT_N2 * Y_PITCH, FFT_LANES), F32),
                        pltpu.VMEM((FFT_HALF * T_PITCH, FFT_LANES), F32)],
        compiler_params=_cparams(("arbitrary", "arbitrary")),
        name="hyena_spectrum",
    )(filt, sumsq, m_a, g2)


def _long_conv_kernel(u_ref, x_ref, skip_ref, ma_ref, hf_ref, g2_ref, g2i_ref, mc_ref,
                      o_ref, y_scr, stage_scr, u_scr, x_scr):
    t = pl.program_id(1)

    @pl.when(t == 0)
    def _():
        _pad_rows_in(u_ref, u_scr)
        _pad_rows_in(x_ref, x_scr)

    @pl.when(t < FFT_STEPS)
    def _():
        _fft_stage_a(t, u_scr, ma_ref, y_scr)

    @pl.when((t >= FFT_STEPS) & (t < 2 * FFT_STEPS))
    def _():
        tb = t - FFT_STEPS
        for g in range(0, FFT_G, 2):
            stage_scr[g // 2] = _fft_load_pair(tb, g, y_scr)
        for g in range(0, FFT_G, 2):
            z = jnp.dot(g2_ref[...], stage_scr[g // 2], preferred_element_type=F32)
            hf = jnp.concatenate([hf_ref[g], hf_ref[g + 1]], axis=1).astype(F32)
            zr, zi = z[:FFT_N2], z[FFT_N2:]
            hr, hi = hf[:FFT_N2], hf[FFT_N2:]
            p = jnp.concatenate([zr * hr - zi * hi, zr * hi + zi * hr], axis=0).astype(BF16)
            b = jnp.dot(g2i_ref[...], p, preferred_element_type=F32)
            for half, k1 in enumerate((tb * FFT_G + g, tb * FFT_G + g + 1)):
                lanes = slice(half * FFT_LANES, (half + 1) * FFT_LANES)
                y_scr[pl.ds(k1, FFT_N2, stride=Y_PITCH), :] = b[:FFT_N2, lanes]
                y_scr[pl.ds(FFT_N1 + k1, FFT_N2, stride=Y_PITCH), :] = b[FFT_N2:, lanes]

    @pl.when(t >= 2 * FFT_STEPS)
    def _():
        tc = t - 2 * FFT_STEPS
        for g in range(FFT_G):
            m2 = tc * FFT_G + g
            bm = y_scr[pl.ds(pl.multiple_of(m2 * Y_PITCH, 8), FFT_ROWS), :]
            y = jnp.dot(mc_ref[g], bm.astype(BF16), preferred_element_type=F32)
            rows = pl.ds(m2, FFT_HALF, stride=T_PITCH)
            u_scr[rows, :] = x_scr[rows, :] * (y + skip_ref[...] * u_scr[rows, :])

    @pl.when(t == 3 * FFT_STEPS - 1)
    def _():
        for n1 in range(FFT_HALF):
            o_ref[n1 * FFT_N2:(n1 + 1) * FFT_N2, :] = u_scr[n1 * T_PITCH:n1 * T_PITCH + FFT_N2, :]


def _long_conv_gate(u, ub, x, xb, skip, hf, hf0, m_a, g2, g2i, m_c):
    nb = HY_WIDTH // FFT_LANES
    last = FFT_STEPS - 1
    step = lambda t, phase: jnp.clip(t - phase * FFT_STEPS, 0, last)
    return pl.pallas_call(
        _long_conv_kernel,
        grid=(nb, 3 * FFT_STEPS),
        in_specs=[
            pl.BlockSpec((None, SEQ, FFT_LANES), lambda c, t: (c + ub, 0, 0)),
            pl.BlockSpec((None, SEQ, FFT_LANES), lambda c, t: (c + xb, 0, 0)),
            pl.BlockSpec((1, FFT_LANES), lambda c, t: (0, c)),
            pl.BlockSpec((FFT_G, FFT_ROWS, LANES), lambda c, t: (step(t, 0), 0, 0)),
            pl.BlockSpec((None, FFT_G, FFT_ROWS, FFT_LANES),
                         lambda c, t: (c + hf0, step(t, 1), 0, 0)),
            pl.BlockSpec((FFT_ROWS, FFT_ROWS), lambda c, t: (0, 0)),
            pl.BlockSpec((FFT_ROWS, FFT_ROWS), lambda c, t: (0, 0)),
            pl.BlockSpec((FFT_G, FFT_HALF, FFT_ROWS), lambda c, t: (step(t, 2), 0, 0)),
        ],
        out_specs=pl.BlockSpec((None, SEQ, FFT_LANES), lambda c, t: (c, 0, 0)),
        out_shape=jax.ShapeDtypeStruct((nb, SEQ, FFT_LANES), F32),
        scratch_shapes=[pltpu.VMEM((FFT_N2 * Y_PITCH, FFT_LANES), F32),
                        pltpu.VMEM((FFT_G // 2, FFT_ROWS, 2 * FFT_LANES), BF16),
                        pltpu.VMEM((FFT_HALF * T_PITCH, FFT_LANES), F32),
                        pltpu.VMEM((FFT_HALF * T_PITCH, FFT_LANES), F32)],
        compiler_params=_cparams(("arbitrary", "arbitrary")),
        name="long_conv_gate",
    )(u, x, skip.reshape(1, HY_WIDTH), m_a, hf, g2, g2i, m_c)


def _merge_kernel(a_ref, hy_ref, ga_ref, gh_ref, wa_ref, wh_ref, o_ref):
    ya = jnp.dot(a_ref[...], wa_ref[...], preferred_element_type=F32)
    hy = jnp.concatenate([hy_ref[blk].astype(BF16) for blk in range(hy_ref.shape[0])], axis=1)
    yh = jnp.dot(hy, wh_ref[...], preferred_element_type=F32)
    o_ref[...] = (ga_ref[...].astype(F32) * ya + gh_ref[...].astype(F32) * yh).astype(BF16)


def _merge(attn, hy, hg, gate_col0, w_attn_o, w_hy_o):
    tm, tn = 1024, 1024
    ga0 = gate_col0 // tn
    gh0 = (gate_col0 + D_MODEL) // tn
    row = lambda i, j: (i, 0)
    return pl.pallas_call(
        _merge_kernel,
        grid=(SEQ // tm, D_MODEL // tn),
        in_specs=[
            pl.BlockSpec((tm, MLA_HEADS * V_HEAD), row),
            pl.BlockSpec((HY_WIDTH // LANES, tm, LANES), lambda i, j: (0, i, 0)),
            pl.BlockSpec((tm, tn), lambda i, j: (i, j + ga0)),
            pl.BlockSpec((tm, tn), lambda i, j: (i, j + gh0)),
            pl.BlockSpec((MLA_HEADS * V_HEAD, tn), lambda i, j: (0, j)),
            pl.BlockSpec((HY_WIDTH, tn), lambda i, j: (0, j)),
        ],
        out_specs=pl.BlockSpec((tm, tn), lambda i, j: (i, j)),
        out_shape=jax.ShapeDtypeStruct((SEQ, D_MODEL), BF16),
        compiler_params=_cparams(("arbitrary", "arbitrary")),
        name="merge_branches",
    )(attn, hy, hg, hg, w_attn_o, w_hy_o)


def _out_proj_kernel(y_ref, w_ref, s_ref, mod_ref, o_ref):
    xm = jnp.dot(y_ref[...], w_ref[...], preferred_element_type=F32)
    o_ref[...] = s_ref[...] + mod_ref[0:1, :] * xm


def _out_proj(y, w_out, s, mod_gate):
    tm = 512
    row = lambda i: (i, 0)
    const = lambda i: (0, 0)
    return pl.pallas_call(
        _out_proj_kernel,
        grid=(SEQ // tm,),
        in_specs=[
            pl.BlockSpec((tm, D_MODEL), row),
            pl.BlockSpec((D_MODEL, D_MODEL), const),
            pl.BlockSpec((tm, D_MODEL), row),
            pl.BlockSpec((8, D_MODEL), const),
        ],
        out_specs=pl.BlockSpec((tm, D_MODEL), row),
        out_shape=jax.ShapeDtypeStruct((SEQ, D_MODEL), F32),
        compiler_params=_cparams(("arbitrary",)),
        name="out_proj",
    )(y, w_out, s, mod_gate)


def _rope_tables():
    t = np.arange(SEQ)
    pos = np.stack([t // GRID_W, t % GRID_W], axis=1).astype(np.float64)
    inv_freq = ROPE_THETA ** (-np.arange(0, ROPE_AXIS, 2, dtype=np.float64) / ROPE_AXIS)
    i = np.arange(QK_ROPE)
    ang = pos[:, i // ROPE_AXIS] * inv_freq[i % (ROPE_AXIS // 2)][None, :]
    cos, sin = np.cos(ang), np.sin(ang)
    tab_q = np.concatenate([np.ones((SEQ, QK_NOPE)), cos, sin], axis=1) * (ATTN_SCALE * math.log2(math.e))
    tab_k = np.concatenate([cos, sin], axis=1)
    tab_k_ctx = np.concatenate([np.ones((CTX_LEN, QK_ROPE)), np.zeros((CTX_LEN, QK_ROPE))], axis=1)
    return (jnp.asarray(tab_q, F32), jnp.asarray(tab_k, F32), jnp.asarray(tab_k_ctx, F32))


def _rope_swap():
    i = np.arange(QK_ROPE)
    first_half = (i % ROPE_AXIS) < ROPE_AXIS // 2
    partner = np.where(first_half, i + ROPE_AXIS // 2, i - ROPE_AXIS // 2)
    sign = np.where(first_half, -1.0, 1.0)
    return partner, sign


def _filter_features():
    pos = np.arange(SEQ, dtype=np.float64)[:, None]
    t01 = pos / (SEQ - 1)
    bands = np.linspace(1e-4, HY_BANDS - 1, HY_BANDS)[None, :]
    ang = bands * (2.0 * math.pi / SEQ) * pos
    dist = np.abs(pos - (SEQ // 2)) / (SEQ / 2.0)
    z = np.concatenate([t01, np.cos(ang), -np.sin(ang), dist], axis=1)
    z = np.pad(z, ((0, 0), (0, HY_ZW - z.shape[1])))
    return jnp.asarray(z, F32)


def _dft_tables():
    n1 = np.arange(FFT_HALF)
    k1 = np.arange(FFT_N1)
    n2 = np.arange(FFT_N2)
    n = FFT_N2 * n1[None, None, :] + n2[:, None, None]
    ph = (k1[None, :, None] * n) % FFT_N
    th = 2.0 * math.pi * ph / FFT_N
    m_a = np.concatenate([np.cos(th), -np.sin(th)], axis=1)
    m_a = np.pad(m_a, ((0, 0), (0, 0), (0, LANES - FFT_HALF)))
    ph2 = (n2[:, None] * n2[None, :]) % FFT_N2
    th2 = 2.0 * math.pi * ph2 / FFT_N2
    c, s = np.cos(th2), np.sin(th2)
    g2 = np.block([[c, s], [-s, c]])
    g2i = np.block([[c, -s], [s, c]])
    n = FFT_N2 * (n1[None, :, None] + FFT_HALF // 2) + n2[:, None, None]
    ph = (k1[None, None, :] * n) % FFT_N
    th = 2.0 * math.pi * ph / FFT_N
    m_c = np.concatenate([np.cos(th), -np.sin(th)], axis=2) / FFT_N
    return tuple(jnp.asarray(t, F32).astype(BF16) for t in (m_a, g2, g2i, m_c))


def _rows8(m, lo, hi):
    return jnp.zeros((8, D_MODEL), F32).at[:hi - lo].set(m[lo:hi])


def kernel(x, c, ctx, c_ctx, w_mod, b_mod, g_ffn1, w13_ffn1, w2_ffn1, g_mix, w_in, g_q, w_uq,
           g_kv, w_ukv, w_attn_o, hy_conv_w, hy_conv_b, hy_w1, hy_b1, hy_w2, hy_b2, hy_w3,
           hy_freq, hy_skip, w_hy_o, w_out, g_ffn2, w13_ffn2, w2_ffn2, g_final):
    xs = x[0]
    cs = ctx[0]
    li = 0

    c8 = jnp.zeros((8, D_MODEL), F32).at[0].set(c[0]).at[1].set(c_ctx)
    mod = _modulation(c8, w_mod[li], b_mod[li])
    mx = mod[0].reshape(N_MOD, D_MODEL)
    mc = mod[1].reshape(N_MOD, D_MODEL)

    w13 = w13_ffn1[li]
    w2 = w2_ffn1[li]
    x1 = _half_ffn(xs, _rows8(mx, 0, 3), g_ffn1[li], w13, w2, g_final, tm=FFN_TM, final_norm=False)
    c1 = _half_ffn(cs, _rows8(mc, 0, 3), g_ffn1[li], w13, w2, g_final, tm=CTX_LEN,
                   final_norm=False)

    hx = _prenorm(x1, _rows8(mx, 3, 5), g_mix[li], tm=512)
    hc = _prenorm(c1, _rows8(mc, 3, 5), g_mix[li], tm=CTX_LEN)

    win = w_in[li]
    partner, sign = _rope_swap()
    swap_np = np.zeros((2 * QK_ROPE, 2 * QK_ROPE))
    swap_np[np.arange(QK_ROPE), np.arange(QK_ROPE)] = 1.0
    swap_np[partner, QK_ROPE + np.arange(QK_ROPE)] = sign
    swap_mat = jnp.asarray(swap_np, BF16)
    wq =w_uq[li].reshape(Q_LORA, MLA_HEADS, QK_NOPE + QK_ROPE)
    wq_r = wq[:, :, QK_NOPE:]
    wq_p = jnp.concatenate([wq[:, :, :QK_NOPE], wq_r, wq_r[:, :, partner] * sign], axis=2)
    wq_p = wq_p.reshape(Q_LORA, MLA_HEADS * HEAD_PAD).astype(BF16)
    wkv = w_ukv[li].reshape(KV_LORA, MLA_HEADS, QK_NOPE + V_HEAD)
    wkv_p = jnp.concatenate([wkv[:, :, :QK_NOPE].reshape(KV_LORA, -1),
                             wkv[:, :, QK_NOPE:].reshape(KV_LORA, -1)], axis=1).astype(BF16)

    tab_q, tab_k, tab_k_ctx = _rope_tables()
    q, k_lat, v_lat = _proj_attn(hx, win, swap_mat, g_q[li], g_kv[li], wq_p, wkv_p, tab_q, tab_k,
                                 tm=512, with_q=True)
    k_ctx, v_ctx = _proj_attn(hc, win, swap_mat, g_q[li], g_kv[li], wq_p, wkv_p,
                              tab_q[:CTX_LEN], tab_k_ctx, tm=CTX_LEN, with_q=False)
    attn = _attention(q, k_lat, k_ctx, v_lat, v_ctx)

    hg = _proj_hg(hx, win, KR_END)
    u3 = _short_conv(hg, hy_conv_w[li], hy_conv_b[li], col0=0)
    w1p = jnp.zeros((HY_ZW, HY_FILTER_W), F32).at[:HY_EMB].set(hy_w1[li])
    decay = np.abs(np.linspace(HY_MIN_DECAY, HY_MAX_DECAY, HY_WIDTH))
    decay2 = jnp.asarray(np.tile(decay, HY_ORDER)[None, :], F32)
    filt, sumsq = _hyena_filters(_filter_features(), w1p, hy_b1[li], hy_w2[li], hy_b2[li],
                                 hy_w3[li], hy_freq[li], decay2)
    m_a, g2, g2i, m_c = _dft_tables()
    hf = _hyena_spectrum(filt, sumsq, m_a, g2)
    nb = HY_WIDTH // FFT_LANES
    z = _long_conv_gate(u3, 0, u3, nb, hy_skip[li][0], hf, 0, m_a, g2, g2i, m_c)
    hy = _long_conv_gate(z, 0, u3, 2 * nb, hy_skip[li][1], hf, nb, m_a, g2, g2i, m_c)

    y = _merge(attn, hy, hg, 3 * HY_WIDTH, w_attn_o[li].astype(BF16), w_hy_o[li].astype(BF16))
    x2 = _out_proj(y, w_out[li].astype(BF16), x1, _rows8(mx, 5, 6))

    out = _half_ffn(x2, _rows8(mx, 6, 9), g_ffn2[li], w13_ffn2[li], w2_ffn2[li], g_final,
                    tm=FFN_TM, final_norm=True)
    return out[None]
```

```python
import functools
import math

import numpy as np
import jax
import jax.numpy as jnp
from jax import lax
from jax.experimental import pallas as pl
from jax.experimental.pallas import tpu as pltpu

F32 = jnp.float32
BF16 = jnp.bfloat16

D_MODEL = 2048
SEQ = 8192
GRID_W = 64
CTX_LEN = 256
N_MOD = 9
D_FF = 5632
NORM_EPS = 1e-6

MLA_HEADS = 16
Q_LORA = 512
KV_LORA = 512
QK_NOPE = 128
QK_ROPE = 64
V_HEAD = 128
ROPE_AXIS = QK_ROPE // 2
ROPE_THETA = 10000.0
ATTN_SCALE = (QK_NOPE + QK_ROPE) ** -0.5
HEAD_PAD = 256

HY_WIDTH = 1024
HY_ORDER = 2
HY_EMB = 33
HY_BANDS = (HY_EMB - 1) // 2
HY_FILTER_W = 64
HY_TARGET = 1e-2
HY_FAST_DECAY = 0.3
HY_SLOW_DECAY = 1.5
HY_MAX_DECAY = math.log(HY_TARGET) / HY_FAST_DECAY
HY_MIN_DECAY = math.log(HY_TARGET) / HY_SLOW_DECAY
HY_MOD_SHIFT = 0.05

Q_END = Q_LORA
KV_END = Q_END + KV_LORA
KR_END = KV_END + QK_ROPE
HY_END = KR_END + 3 * HY_WIDTH

FFT_N = 2 * SEQ
FFT_N1 = 128
FFT_N2 = 128
FFT_HALF = SEQ // FFT_N2

V7X_VMEM_LIMIT = 60 * 1024 * 1024
LANES = 128


def _cparams(sem, vmem=V7X_VMEM_LIMIT):
    return pltpu.CompilerParams(dimension_semantics=sem, vmem_limit_bytes=vmem)


def _sigmoid(x):
    return 1.0 / (1.0 + jnp.exp(-x))


def _rms(x, g):
    var = jnp.mean(x * x, axis=-1, keepdims=True)
    return x * lax.rsqrt(var + NORM_EPS) * g


def _mod_kernel(c_ref, w_ref, b_ref, o_ref):
    c = c_ref[...]
    a = c * _sigmoid(c)
    o_ref[...] = jnp.dot(a, w_ref[...], preferred_element_type=F32) + b_ref[...]


def _modulation(c8, w_mod, b_mod):
    n = w_mod.shape[1]
    tn = 1024
    return pl.pallas_call(
        _mod_kernel,
        grid=(n // tn,),
        in_specs=[
            pl.BlockSpec((8, D_MODEL), lambda j: (0, 0)),
            pl.BlockSpec((D_MODEL, tn), lambda j: (0, j)),
            pl.BlockSpec((1, tn), lambda j: (0, j)),
        ],
        out_specs=pl.BlockSpec((8, tn), lambda j: (0, j)),
        out_shape=jax.ShapeDtypeStruct((8, n), F32),
        compiler_params=_cparams(("arbitrary",)),
        name="modulation",
    )(c8, w_mod, b_mod.reshape(1, n))


def _ffn_kernel(s_ref, mod_ref, g_ref, w1_ref, w3_ref, w2_ref, gf_ref, o_ref, h_scr,
                *, final_norm):
    f = pl.program_id(1)

    @pl.when(f == 0)
    def _():
        h = _rms(s_ref[...], g_ref[...])
        h = h * (1.0 + mod_ref[1:2, :]) + mod_ref[0:1, :]
        h_scr[...] = h.astype(BF16)
        o_ref[...] = jnp.zeros_like(o_ref)

    h = h_scr[...]
    a = jnp.dot(h, w1_ref[...].astype(BF16), preferred_element_type=F32)
    b = jnp.dot(h, w3_ref[...].astype(BF16), preferred_element_type=F32)
    act = (a * _sigmoid(a) * b).astype(BF16)
    o_ref[...] += jnp.dot(act, w2_ref[...].astype(BF16), preferred_element_type=F32)

    @pl.when(f == pl.num_programs(1) - 1)
    def _():
        out = s_ref[...] + 0.5 * mod_ref[2:3, :] * o_ref[...]
        if final_norm:
            out = _rms(out, gf_ref[...])
        o_ref[...] = out


FFN_TM = 1024


FFN_TF = 256


def _half_ffn(s, mod3, g, w13, w2, g_final, *, tm, final_norm):
    rows = s.shape[0]
    tf = FFN_TF
    nf = D_FF // tf
    return pl.pallas_call(
        functools.partial(_ffn_kernel, final_norm=final_norm),
        grid=(rows // tm, nf),
        in_specs=[
            pl.BlockSpec((tm, D_MODEL), lambda i, f: (i, 0)),
            pl.BlockSpec((8, D_MODEL), lambda i, f: (0, 0)),
            pl.BlockSpec((1, D_MODEL), lambda i, f: (0, 0)),
            pl.BlockSpec((D_MODEL, tf), lambda i, f: (0, f)),
            pl.BlockSpec((D_MODEL, tf), lambda i, f: (0, f + nf)),
            pl.BlockSpec((tf, D_MODEL), lambda i, f: (f, 0)),
            pl.BlockSpec((1, D_MODEL), lambda i, f: (0, 0)),
        ],
        out_specs=pl.BlockSpec((tm, D_MODEL), lambda i, f: (i, 0)),
        out_shape=jax.ShapeDtypeStruct((rows, D_MODEL), F32),
        scratch_shapes=[pltpu.VMEM((tm, D_MODEL), BF16)],
        compiler_params=_cparams(("arbitrary", "arbitrary")),
        name="half_ffn",
    )(s, mod3, g.reshape(1, D_MODEL), w13, w13, w2, g_final.reshape(1, D_MODEL))


def _prenorm_kernel(s_ref, mod_ref, g_ref, o_ref):
    h = _rms(s_ref[...], g_ref[...])
    o_ref[...] = (h * (1.0 + mod_ref[1:2, :]) + mod_ref[0:1, :]).astype(BF16)


def _prenorm(s, mod3, g, *, tm):
    rows = s.shape[0]
    return pl.pallas_call(
        _prenorm_kernel,
        grid=(rows // tm,),
        in_specs=[
            pl.BlockSpec((tm, D_MODEL), lambda i: (i, 0)),
            pl.BlockSpec((8, D_MODEL), lambda i: (0, 0)),
            pl.BlockSpec((1, D_MODEL), lambda i: (0, 0)),
        ],
        out_specs=pl.BlockSpec((tm, D_MODEL), lambda i: (i, 0)),
        out_shape=jax.ShapeDtypeStruct((rows, D_MODEL), BF16),
        compiler_params=_cparams(("arbitrary",)),
        name="prenorm",
    )(s, mod3, g.reshape(1, D_MODEL))


_NT = (((1,), (1,)), ((), ()))


def _proj_hg_kernel(h_ref, wa_ref, wb_ref, o_ref, *, n_plain_tiles, row_off):
    j = pl.program_id(1)
    w = jnp.concatenate([wa_ref[row_off:, :], wb_ref[:row_off, :]], axis=0).astype(BF16)
    acc = lax.dot_general(h_ref[...], w, _NT, preferred_element_type=F32)
    o_ref[...] = jnp.where(j >= n_plain_tiles, _sigmoid(acc), acc).astype(BF16)


def _proj_hg(h, w_in_t, row0):
    rows = h.shape[0]
    n = 3 * HY_WIDTH + 2 * D_MODEL
    tm, tn = 1024, 1024
    blk0, row_off = divmod(row0, tn)
    assert 0 < row_off < LANES and row_off % 8 == 0 and row0 + n <= w_in_t.shape[0]
    return pl.pallas_call(
        functools.partial(_proj_hg_kernel, n_plain_tiles=3 * HY_WIDTH // tn, row_off=row_off),
        grid=(rows // tm, n // tn),
        in_specs=[
            pl.BlockSpec((tm, D_MODEL), lambda i, j: (i, 0)),
            pl.BlockSpec((tn, D_MODEL), lambda i, j: (j + blk0, 0)),
            pl.BlockSpec((LANES, D_MODEL), lambda i, j: ((j + blk0 + 1) * (tn // LANES), 0)),
        ],
        out_specs=pl.BlockSpec((tm, tn), lambda i, j: (i, j)),
        out_shape=jax.ShapeDtypeStruct((rows, n), BF16),
        compiler_params=_cparams(("arbitrary", "arbitrary")),
        name="proj_gate_hyena",
    )(h, w_in_t, w_in_t)


def _proj_attn_kernel(h_ref, win_ref, swap_ref, gq_ref, gkv_ref, wuq_ref, wukv_ref, tq_ref,
                      tk_ref, *refs, with_q):
    if with_q:
        q_ref, k_ref, v_ref, wa_scr = refs
    else:
        k_ref, v_ref, wa_scr = refs

    @pl.when(pl.program_id(0) == 0)
    def _():
        wa_scr[...] = win_ref[...].astype(BF16)

    p = lax.dot_general(h_ref[...], wa_scr[...], _NT, preferred_element_type=F32)
    kvn = _rms(p[:, Q_LORA:Q_LORA + KV_LORA], gkv_ref[...]).astype(BF16)
    kr = p[:, KV_END:KV_END + 2 * QK_ROPE]
    kr_hi = kr.astype(BF16)
    kr_lo = (kr - kr_hi.astype(F32)).astype(BF16)
    kr2 = (jnp.dot(kr_hi, swap_ref[...], preferred_element_type=F32)
           + jnp.dot(kr_lo, swap_ref[...], preferred_element_type=F32))
    t = kr2 * tk_ref[...]
    krot = (t + pltpu.roll(t, QK_ROPE, 1)).astype(BF16)
    kn = jnp.dot(kvn, wukv_ref[:, :MLA_HEADS * QK_NOPE], preferred_element_type=F32)
    for hd in range(MLA_HEADS):
        k_ref[:, hd * HEAD_PAD:hd * HEAD_PAD + QK_NOPE] = (
            kn[:, hd * QK_NOPE:(hd + 1) * QK_NOPE].astype(BF16))
        k_ref[:, hd * HEAD_PAD + QK_NOPE:(hd + 1) * HEAD_PAD] = krot
    vv = jnp.dot(kvn, wukv_ref[:, MLA_HEADS * QK_NOPE:], preferred_element_type=F32)
    lane = lax.broadcasted_iota(jnp.int32, (h_ref.shape[0], HEAD_PAD - V_HEAD), 1)
    unit = jnp.where(lane == 0, 1.0, 0.0).astype(BF16)
    for hd in range(MLA_HEADS):
        v_ref[:, hd * HEAD_PAD:hd * HEAD_PAD + V_HEAD] = (
            vv[:, hd * V_HEAD:(hd + 1) * V_HEAD].astype(BF16))
        v_ref[:, hd * HEAD_PAD + V_HEAD:(hd + 1) * HEAD_PAD] = unit
    if with_q:
        qn = _rms(p[:, :Q_LORA], gq_ref[...]).astype(BF16)
        tq = tq_ref[...]
        for hd in range(MLA_HEADS):
            qh = jnp.dot(qn, wuq_ref[:, hd * HEAD_PAD:(hd + 1) * HEAD_PAD],
                         preferred_element_type=F32)
            q_ref[:, hd * HEAD_PAD:(hd + 1) * HEAD_PAD] = (qh * tq).astype(BF16)


def _proj_attn(h, w_in_t, swap_mat, g_q, g_kv, w_uq, w_ukv, tab_q, tab_k, *, tm, with_q):
    rows = h.shape[0]
    na = KV_END + 2 * QK_ROPE
    once = dict(pipeline_mode=pl.Buffered(1))
    hp = MLA_HEADS * HEAD_PAD
    hv = MLA_HEADS * HEAD_PAD
    const = lambda i: (0, 0)
    row = lambda i: (i, 0)
    out_shape = [jax.ShapeDtypeStruct((rows, hp), BF16), jax.ShapeDtypeStruct((rows, hv), BF16)]
    out_specs = [pl.BlockSpec((tm, hp), row), pl.BlockSpec((tm, hv), row)]
    if with_q:
        out_shape = [jax.ShapeDtypeStruct((rows, hp), BF16)] + out_shape
        out_specs = [pl.BlockSpec((tm, hp), row)] + out_specs
    return pl.pallas_call(
        functools.partial(_proj_attn_kernel, with_q=with_q),
        grid=(rows // tm,),
        in_specs=[
            pl.BlockSpec((tm, D_MODEL), row),
            pl.BlockSpec((na, D_MODEL), const, **once),
            pl.BlockSpec((2 * QK_ROPE, 2 * QK_ROPE), const),
            pl.BlockSpec((1, Q_LORA), const),
            pl.BlockSpec((1, KV_LORA), const),
            pl.BlockSpec((Q_LORA, hp), const, **once),
            pl.BlockSpec((KV_LORA, MLA_HEADS * (QK_NOPE + V_HEAD)), const, **once),
            pl.BlockSpec((tm, HEAD_PAD), row),
            pl.BlockSpec((tm, 2 * QK_ROPE), row),
        ],
        out_specs=out_specs,
        out_shape=out_shape,
        scratch_shapes=[pltpu.VMEM((na, D_MODEL), BF16)],
        compiler_params=_cparams(("arbitrary",)),
        name="proj_attn",
    )(h, w_in_t, swap_mat, g_q.reshape(1, Q_LORA), g_kv.reshape(1, KV_LORA), w_uq, w_ukv,
      tab_q, tab_k)


ATT_TQ = 2048
ATT_SUB = 512
ATT_KC = 1024
def _attn_kernel(q_ref, kl_ref, kc_ref, vl_ref, vc_ref, o_ref):
    n_sub = ATT_TQ // ATT_SUB
    qs, ms, accs = [], [], []
    for r in range(n_sub):
        q = q_ref[r * ATT_SUB:(r + 1) * ATT_SUB, :]
        s = lax.dot_general(q, kc_ref[...], _NT, preferred_element_type=F32)
        m = jnp.max(s, axis=-1, keepdims=True)
        p = jnp.exp2(s - m)
        qs.append(q)
        ms.append(m)
        accs.append(jnp.dot(p.astype(BF16), vc_ref[...], preferred_element_type=F32))
    for c in range(SEQ // ATT_KC):
        for r in range(n_sub):
            s = lax.dot_general(qs[r], kl_ref[c * ATT_KC:(c + 1) * ATT_KC, :], _NT,
                                preferred_element_type=F32)
            m_new = jnp.maximum(ms[r], jnp.max(s, axis=-1, keepdims=True))
            alpha = jnp.exp2(ms[r] - m_new)
            p = jnp.exp2(s - m_new)
            accs[r] = accs[r] * alpha + jnp.dot(
                p.astype(BF16), vl_ref[c * ATT_KC:(c + 1) * ATT_KC, :],
                preferred_element_type=F32)
            ms[r] = m_new
    for r in range(n_sub):
        acc = accs[r]
        o_ref[r * ATT_SUB:(r + 1) * ATT_SUB, :] = (
            acc[:, :V_HEAD] / acc[:, V_HEAD:V_HEAD + 1]).astype(BF16)


def _attention(q, k_lat, k_ctx, v_lat, v_ctx):
    return pl.pallas_call(
        _attn_kernel,
        grid=(MLA_HEADS, SEQ // ATT_TQ),
        in_specs=[
            pl.BlockSpec((ATT_TQ, HEAD_PAD), lambda h, i: (i, h)),
            pl.BlockSpec((SEQ, HEAD_PAD), lambda h, i: (0, h)),
            pl.BlockSpec((CTX_LEN, HEAD_PAD), lambda h, i: (0, h)),
            pl.BlockSpec((SEQ, HEAD_PAD), lambda h, i: (0, h)),
            pl.BlockSpec((CTX_LEN, HEAD_PAD), lambda h, i: (0, h)),
        ],
        out_specs=pl.BlockSpec((ATT_TQ, V_HEAD), lambda h, i: (i, h)),
        out_shape=jax.ShapeDtypeStruct((SEQ, MLA_HEADS * V_HEAD), BF16),
        compiler_params=_cparams(("arbitrary", "arbitrary")),
        name="attention",
    )(q, k_lat, k_ctx, v_lat, v_ctx)


SC_HALO = 16


def _short_conv_kernel(x_ref, prev_ref, next_ref, w_ref, b_ref, o_ref):
    i = pl.program_id(0)
    tm = x_ref.shape[0]
    x = x_ref[...].astype(F32)
    prev_row = prev_ref[SC_HALO - 1:SC_HALO, :].astype(F32) * (i > 0).astype(F32)
    next_row = next_ref[0:1, :].astype(F32) * (i < pl.num_programs(0) - 1).astype(F32)
    rows = lax.broadcasted_iota(jnp.int32, x.shape, 0)
    up = jnp.where(rows == 0, prev_row, pltpu.roll(x, 1, 0))
    dn = jnp.where(rows == tm - 1, next_row, pltpu.roll(x, tm - 1, 0))
    y = up * w_ref[0:1, :] + x * w_ref[1:2, :] + dn * w_ref[2:3, :] + b_ref[...]
    for blk in range(o_ref.shape[0]):
        o_ref[blk] = y[:, blk * LANES:(blk + 1) * LANES]


def _short_conv(hg, conv_w, conv_b, *, col0):
    tm, tc = 512, 1024
    nb = 3 * HY_WIDTH // tc
    cb0 = col0 // tc
    hb = tm // SC_HALO
    last = SEQ // SC_HALO - 1
    w8 = jnp.zeros((8, 3 * HY_WIDTH), F32).at[:3].set(conv_w)
    return pl.pallas_call(
        _short_conv_kernel,
        grid=(SEQ // tm, nb),
        in_specs=[
            pl.BlockSpec((tm, tc), lambda i, j: (i, j + cb0)),
            pl.BlockSpec((SC_HALO, tc), lambda i, j: (jnp.maximum(i * hb - 1, 0), j + cb0)),
            pl.BlockSpec((SC_HALO, tc), lambda i, j: (jnp.minimum((i + 1) * hb, last), j + cb0)),
            pl.BlockSpec((8, tc), lambda i, j: (0, j)),
            pl.BlockSpec((1, tc), lambda i, j: (0, j)),
        ],
        out_specs=pl.BlockSpec((tc // LANES, tm, LANES), lambda i, j: (j, i, 0)),
        out_shape=jax.ShapeDtypeStruct((3 * HY_WIDTH // LANES, SEQ, LANES), F32),
        compiler_params=_cparams(("arbitrary", "arbitrary")),
        name="short_conv",
    )(hg, hg, hg, w8, conv_b.reshape(1, -1))


HY_ZW = 128


def _dot_split3(a, b):
    a_hi = a.astype(BF16)
    b_hi = b.astype(BF16)
    a_lo = (a - a_hi.astype(F32)).astype(BF16)
    b_lo = (b - b_hi.astype(F32)).astype(BF16)
    dot = functools.partial(jnp.dot, preferred_element_type=F32)
    return dot(a_hi, b_hi) + (dot(a_hi, b_lo) + dot(a_lo, b_hi))


def _filter_kernel(z_ref, w1_ref, b1_ref, w2_ref, b2_ref, w3_ref, fr_ref, dec_ref,
                   h_ref, ss_ref):
    hi = lax.Precision.HIGHEST
    z = z_ref[...]
    fr = fr_ref[...]
    h = jnp.sin(fr * (jnp.dot(z, w1_ref[...], precision=hi, preferred_element_type=F32)
                      + b1_ref[...]))
    h = jnp.sin(fr * (jnp.dot(h, w2_ref[...], precision=hi, preferred_element_type=F32)
                      + b2_ref[...]))
    h = _dot_split3(h, w3_ref[...])
    dist = z[:, HY_EMB:HY_EMB + 1]
    h = h * (jnp.exp(-dist * dec_ref[...]) + HY_MOD_SHIFT)
    for blk in range(h_ref.shape[0]):
        h_ref[blk] = h[:, blk * LANES:(blk + 1) * LANES]

    @pl.when(pl.program_id(0) == 0)
    def _():
        ss_ref[...] = jnp.zeros_like(ss_ref)

    ss_ref[...] += jnp.sum(h * h, axis=0, keepdims=True)


def _hyena_filters(z_p, w1p, b1, w2, b2, w3, freq, decay2):
    tl = 512
    nw = HY_ORDER * HY_WIDTH
    const = lambda i: (0, 0)
    return pl.pallas_call(
        _filter_kernel,
        grid=(SEQ // tl,),
        in_specs=[
            pl.BlockSpec((tl, HY_ZW), lambda i: (i, 0)),
            pl.BlockSpec((HY_ZW, HY_FILTER_W), const),
            pl.BlockSpec((1, HY_FILTER_W), const),
            pl.BlockSpec((HY_FILTER_W, HY_FILTER_W), const),
            pl.BlockSpec((1, HY_FILTER_W), const),
            pl.BlockSpec((HY_FILTER_W, nw), const),
            pl.BlockSpec((1, HY_FILTER_W), const),
            pl.BlockSpec((1, nw), const),
        ],
        out_specs=[pl.BlockSpec((nw // LANES, tl, LANES), lambda i: (0, i, 0)),
                   pl.BlockSpec((1, nw), const)],
        out_shape=[jax.ShapeDtypeStruct((nw // LANES, SEQ, LANES), F32),
                   jax.ShapeDtypeStruct((1, nw), F32)],
        compiler_params=_cparams(("arbitrary",)),
        name="hyena_filters",
    )(z_p, w1p, b1.reshape(1, -1), w2, b2.reshape(1, -1), w3, freq.reshape(1, -1), decay2)


FFT_LANES = 128
FFT_G = 32
FFT_STEPS = FFT_N2 // FFT_G
FFT_ROWS = 2 * FFT_N1
Y_PITCH = FFT_ROWS + 8
T_PITCH = FFT_N2 + 8


def _pad_rows_in(src_ref, dst_scr, scale=None):
    for n1 in range(FFT_HALF):
        v = src_ref[n1 * FFT_N2:(n1 + 1) * FFT_N2, :]
        dst_scr[n1 * T_PITCH:n1 * T_PITCH + FFT_N2, :] = v if scale is None else v * scale


def _fft_stage_a(t, src_scr, ma_ref, y_scr):
    for g in range(FFT_G):
        j = t * FFT_G + g
        xj = src_scr[pl.ds(j, FFT_HALF, stride=T_PITCH), :].astype(BF16)
        xj = jnp.concatenate([xj, jnp.zeros_like(xj)], axis=0)
        a = jnp.dot(ma_ref[g], xj, preferred_element_type=F32)
        y_scr[pl.ds(pl.multiple_of(j * Y_PITCH, 8), FFT_ROWS), :] = a


def _fft_load_pair(tb, g, y_scr):
    cols = []
    for k1 in (tb * FFT_G + g, tb * FFT_G + g + 1):
        re = y_scr[pl.ds(k1, FFT_N2, stride=Y_PITCH), :]
        im = y_scr[pl.ds(FFT_N1 + k1, FFT_N2, stride=Y_PITCH), :]
        cols.append(jnp.concatenate([re, im], axis=0))
    return jnp.concatenate(cols, axis=1).astype(BF16)


def _hyena_spectrum_kernel(h_ref, ss_ref, ma_ref, g2_ref, o_ref, y_scr, h_scr):
    t = pl.program_id(1)

    @pl.when(t == 0)
    def _():
        _pad_rows_in(h_ref, h_scr, lax.rsqrt(ss_ref[...] + NORM_EPS))

    @pl.when(t < FFT_STEPS)
    def _():
        _fft_stage_a(t, h_scr, ma_ref, y_scr)

    @pl.when(t >= FFT_STEPS)
    def _():
        tb = t - FFT_STEPS
        for g in range(0, FFT_G, 2):
            z = jnp.dot(g2_ref[...], _fft_load_pair(tb, g, y_scr), preferred_element_type=F32)
            o_ref[g] = z[:, :FFT_LANES].astype(BF16)
            o_ref[g + 1] = z[:, FFT_LANES:].astype(BF16)


def _hyena_spectrum(filt, sumsq, m_a, g2):
    nb = filt.shape[0]
    last = FFT_STEPS - 1
    return pl.pallas_call(
        _hyena_spectrum_kernel,
        grid=(nb, 2 * FFT_STEPS),
        in_specs=[
            pl.BlockSpec((None, SEQ, FFT_LANES), lambda c, t: (c, 0, 0)),
            pl.BlockSpec((1, FFT_LANES), lambda c, t: (0, c)),
            pl.BlockSpec((FFT_G, FFT_ROWS, LANES), lambda c, t: (jnp.minimum(t, last), 0, 0)),
            pl.BlockSpec((FFT_ROWS, FFT_ROWS), lambda c, t: (0, 0)),
        ],
        out_specs=pl.BlockSpec((None, FFT_G, FFT_ROWS, FFT_LANES),
                               lambda c, t: (c, jnp.maximum(t - FFT_STEPS, 0), 0, 0)),
        out_shape=jax.ShapeDtypeStruct((nb, FFT_N1, FFT_ROWS, FFT_LANES), BF16),
        scratch_shapes=[pltpu.VMEM((FFT_N2 * Y_PITCH, FFT_LANES), F32),
                        pltpu.VMEM((FFT_HALF * T_PITCH, FFT_LANES), F32)],
        compiler_params=_cparams(("arbitrary", "arbitrary")),
        name="hyena_spectrum",
    )(filt, sumsq, m_a, g2)


def _long_conv_kernel(u_ref, x_ref, skip_ref, ma_ref, hf_ref, g2_ref, g2i_ref, mc_ref,
                      o_ref, y_scr, stage_scr, u_scr, x_scr):
    t = pl.program_id(1)

    @pl.when(t == 0)
    def _():
        _pad_rows_in(u_ref, u_scr)
        _pad_rows_in(x_ref, x_scr)

    @pl.when(t < FFT_STEPS)
    def _():
        _fft_stage_a(t, u_scr, ma_ref, y_scr)

    @pl.when((t >= FFT_STEPS) & (t < 2 * FFT_STEPS))
    def _():
        tb = t - FFT_STEPS
        for g in range(0, FFT_G, 2):
            stage_scr[g // 2] = _fft_load_pair(tb, g, y_scr)
        for g in range(0, FFT_G, 2):
            z = jnp.dot(g2_ref[...], stage_scr[g // 2], preferred_element_type=F32)
            hf = jnp.concatenate([hf_ref[g], hf_ref[g + 1]], axis=1).astype(F32)
            zr, zi = z[:FFT_N2], z[FFT_N2:]
            hr, hi = hf[:FFT_N2], hf[FFT_N2:]
            p = jnp.concatenate([zr * hr - zi * hi, zr * hi + zi * hr], axis=0).astype(BF16)
            b = jnp.dot(g2i_ref[...], p, preferred_element_type=F32)
            for half, k1 in enumerate((tb * FFT_G + g, tb * FFT_G + g + 1)):
                lanes = slice(half * FFT_LANES, (half + 1) * FFT_LANES)
                y_scr[pl.ds(k1, FFT_N2, stride=Y_PITCH), :] = b[:FFT_N2, lanes]
                y_scr[pl.ds(FFT_N1 + k1, FFT_N2, stride=Y_PITCH), :] = b[FFT_N2:, lanes]

    @pl.when(t >= 2 * FFT_STEPS)
    def _():
        tc = t - 2 * FFT_STEPS
        for g in range(FFT_G):
            m2 = tc * FFT_G + g
            bm = y_scr[pl.ds(pl.multiple_of(m2 * Y_PITCH, 8), FFT_ROWS), :]
            y = jnp.dot(mc_ref[g], bm.astype(BF16), preferred_element_type=F32)
            rows = pl.ds(m2, FFT_HALF, stride=T_PITCH)
            u_scr[rows, :] = x_scr[rows, :] * (y + skip_ref[...] * u_scr[rows, :])

    @pl.when(t == 3 * FFT_STEPS - 1)
    def _():
        for n1 in range(FFT_HALF):
            o_ref[n1 * FFT_N2:(n1 + 1) * FFT_N2, :] = u_scr[n1 * T_PITCH:n1 * T_PITCH + FFT_N2, :]


def _long_conv_gate(u, ub, x, xb, skip, hf, hf0, m_a, g2, g2i, m_c):
    nb = HY_WIDTH // FFT_LANES
    last = FFT_STEPS - 1
    step = lambda t, phase: jnp.clip(t - phase * FFT_STEPS, 0, last)
    return pl.pallas_call(
        _long_conv_kernel,
        grid=(nb, 3 * FFT_STEPS),
        in_specs=[
            pl.BlockSpec((None, SEQ, FFT_LANES), lambda c, t: (c + ub, 0, 0)),
            pl.BlockSpec((None, SEQ, FFT_LANES), lambda c, t: (c + xb, 0, 0)),
            pl.BlockSpec((1, FFT_LANES), lambda c, t: (0, c)),
            pl.BlockSpec((FFT_G, FFT_ROWS, LANES), lambda c, t: (step(t, 0), 0, 0)),
            pl.BlockSpec((None, FFT_G, FFT_ROWS, FFT_LANES),
                         lambda c, t: (c + hf0, step(t, 1), 0, 0)),
            pl.BlockSpec((FFT_ROWS, FFT_ROWS), lambda c, t: (0, 0)),
            pl.BlockSpec((FFT_ROWS, FFT_ROWS), lambda c, t: (0, 0)),
            pl.BlockSpec((FFT_G, FFT_HALF, FFT_ROWS), lambda c, t: (step(t, 2), 0, 0)),
        ],
        out_specs=pl.BlockSpec((None, SEQ, FFT_LANES), lambda c, t: (c, 0, 0)),
        out_shape=jax.ShapeDtypeStruct((nb, SEQ, FFT_LANES), F32),
        scratch_shapes=[pltpu.VMEM((FFT_N2 * Y_PITCH, FFT_LANES), F32),
                        pltpu.VMEM((FFT_G // 2, FFT_ROWS, 2 * FFT_LANES), BF16),
                        pltpu.VMEM((FFT_HALF * T_PITCH, FFT_LANES), F32),
                        pltpu.VMEM((FFT_HALF * T_PITCH, FFT_LANES), F32)],
        compiler_params=_cparams(("arbitrary", "arbitrary")),
        name="long_conv_gate",
    )(u, x, skip.reshape(1, HY_WIDTH), m_a, hf, g2, g2i, m_c)


def _merge_kernel(a_ref, hy_ref, ga_ref, gh_ref, wa_ref, wh_ref, o_ref):
    ya = jnp.dot(a_ref[...], wa_ref[...], preferred_element_type=F32)
    hy = jnp.concatenate([hy_ref[blk].astype(BF16) for blk in range(hy_ref.shape[0])], axis=1)
    yh = jnp.dot(hy, wh_ref[...], preferred_element_type=F32)
    o_ref[...] = (ga_ref[...].astype(F32) * ya + gh_ref[...].astype(F32) * yh).astype(BF16)


def _merge(attn, hy, hg, gate_col0, w_attn_o, w_hy_o):
    tm, tn = 1024, 1024
    ga0 = gate_col0 // tn
    gh0 = (gate_col0 + D_MODEL) // tn
    row = lambda i, j: (i, 0)
    return pl.pallas_call(
        _merge_kernel,
        grid=(SEQ // tm, D_MODEL // tn),
        in_specs=[
            pl.BlockSpec((tm, MLA_HEADS * V_HEAD), row),
            pl.BlockSpec((HY_WIDTH // LANES, tm, LANES), lambda i, j: (0, i, 0)),
            pl.BlockSpec((tm, tn), lambda i, j: (i, j + ga0)),
            pl.BlockSpec((tm, tn), lambda i, j: (i, j + gh0)),
            pl.BlockSpec((MLA_HEADS * V_HEAD, tn), lambda i, j: (0, j)),
            pl.BlockSpec((HY_WIDTH, tn), lambda i, j: (0, j)),
        ],
        out_specs=pl.BlockSpec((tm, tn), lambda i, j: (i, j)),
        out_shape=jax.ShapeDtypeStruct((SEQ, D_MODEL), BF16),
        compiler_params=_cparams(("arbitrary", "arbitrary")),
        name="merge_branches",
    )(attn, hy, hg, hg, w_attn_o, w_hy_o)


def _out_proj_kernel(y_ref, w_ref, s_ref, mod_ref, o_ref):
    xm = jnp.dot(y_ref[...], w_ref[...], preferred_element_type=F32)
    o_ref[...] = s_ref[...] + mod_ref[0:1, :] * xm


def _out_proj(y, w_out, s, mod_gate):
    tm = 512
    row = lambda i: (i, 0)
    const = lambda i: (0, 0)
    return pl.pallas_call(
        _out_proj_kernel,
        grid=(SEQ // tm,),
        in_specs=[
            pl.BlockSpec((tm, D_MODEL), row),
            pl.BlockSpec((D_MODEL, D_MODEL), const),
            pl.BlockSpec((tm, D_MODEL), row),
            pl.BlockSpec((8, D_MODEL), const),
        ],
        out_specs=pl.BlockSpec((tm, D_MODEL), row),
        out_shape=jax.ShapeDtypeStruct((SEQ, D_MODEL), F32),
        compiler_params=_cparams(("arbitrary",)),
        name="out_proj",
    )(y, w_out, s, mod_gate)


def _rope_tables():
    t = np.arange(SEQ)
    pos = np.stack([t // GRID_W, t % GRID_W], axis=1).astype(np.float64)
    inv_freq = ROPE_THETA ** (-np.arange(0, ROPE_AXIS, 2, dtype=np.float64) / ROPE_AXIS)
    i = np.arange(QK_ROPE)
    ang = pos[:, i // ROPE_AXIS] * inv_freq[i % (ROPE_AXIS // 2)][None, :]
    cos, sin = np.cos(ang), np.sin(ang)
    tab_q = np.concatenate([np.ones((SEQ, QK_NOPE)), cos, sin], axis=1) * (ATTN_SCALE * math.log2(math.e))
    tab_k = np.concatenate([cos, sin], axis=1)
    tab_k_ctx = np.concatenate([np.ones((CTX_LEN, QK_ROPE)), np.zeros((CTX_LEN, QK_ROPE))], axis=1)
    return (jnp.asarray(tab_q, F32), jnp.asarray(tab_k, F32), jnp.asarray(tab_k_ctx, F32))


def _rope_swap():
    i = np.arange(QK_ROPE)
    first_half = (i % ROPE_AXIS) < ROPE_AXIS // 2
    partner = np.where(first_half, i + ROPE_AXIS // 2, i - ROPE_AXIS // 2)
    sign = np.where(first_half, -1.0, 1.0)
    return partner, sign


def _filter_features():
    pos = np.arange(SEQ, dtype=np.float64)[:, None]
    t01 = pos / (SEQ - 1)
    bands = np.linspace(1e-4, HY_BANDS - 1, HY_BANDS)[None, :]
    ang = bands * (2.0 * math.pi / SEQ) * pos
    dist = np.abs(pos - (SEQ // 2)) / (SEQ / 2.0)
    z = np.concatenate([t01, np.cos(ang), -np.sin(ang), dist], axis=1)
    z = np.pad(z, ((0, 0), (0, HY_ZW - z.shape[1])))
    return jnp.asarray(z, F32)


def _dft_tables():
    n1 = np.arange(FFT_HALF)
    k1 = np.arange(FFT_N1)
    n2 = np.arange(FFT_N2)
    n = FFT_N2 * n1[None, None, :] + n2[:, None, None]
    ph = (k1[None, :, None] * n) % FFT_N
    th = 2.0 * math.pi * ph / FFT_N
    m_a = np.concatenate([np.cos(th), -np.sin(th)], axis=1)
    m_a = np.pad(m_a, ((0, 0), (0, 0), (0, LANES - FFT_HALF)))
    ph2 = (n2[:, None] * n2[None, :]) % FFT_N2
    th2 = 2.0 * math.pi * ph2 / FFT_N2
    c, s = np.cos(th2), np.sin(th2)
    g2 = np.block([[c, s], [-s, c]])
    g2i = np.block([[c, -s], [s, c]])
    n = FFT_N2 * (n1[None, :, None] + FFT_HALF // 2) + n2[:, None, None]
    ph = (k1[None, None, :] * n) % FFT_N
    th = 2.0 * math.pi * ph / FFT_N
    m_c = np.concatenate([np.cos(th), -np.sin(th)], axis=2) / FFT_N
    return tuple(jnp.asarray(t, F32).astype(BF16) for t in (m_a, g2, g2i, m_c))


def _rows8(m, lo, hi):
    return jnp.zeros((8, D_MODEL), F32).at[:hi - lo].set(m[lo:hi])


def kernel(x, c, ctx, c_ctx, w_mod, b_mod, g_ffn1, w13_ffn1, w2_ffn1, g_mix, w_in, g_q, w_uq,
           g_kv, w_ukv, w_attn_o, hy_conv_w, hy_conv_b, hy_w1, hy_b1, hy_w2, hy_b2, hy_w3,
           hy_freq, hy_skip, w_hy_o, w_out, g_ffn2, w13_ffn2, w2_ffn2, g_final):
    xs = x[0]
    cs = ctx[0]
    li = 0

    c8 = jnp.zeros((8, D_MODEL), F32).at[0].set(c[0]).at[1].set(c_ctx)
    mod = _modulation(c8, w_mod[li], b_mod[li])
    mx = mod[0].reshape(N_MOD, D_MODEL)
    mc = mod[1].reshape(N_MOD, D_MODEL)

    w13 = w13_ffn1[li]
    w2 = w2_ffn1[li]
    x1 = _half_ffn(xs, _rows8(mx, 0, 3), g_ffn1[li], w13, w2, g_final, tm=FFN_TM, final_norm=False)
    c1 = _half_ffn(cs, _rows8(mc, 0, 3), g_ffn1[li], w13, w2, g_final, tm=CTX_LEN,
                   final_norm=False)

    hx = _prenorm(x1, _rows8(mx, 3, 5), g_mix[li], tm=512)
    hc = _prenorm(c1, _rows8(mc, 3, 5), g_mix[li], tm=CTX_LEN)

    win_t = jnp.swapaxes(w_in[li], 0, 1)
    partner, sign = _rope_swap()
    swap_np = np.zeros((2 * QK_ROPE, 2 * QK_ROPE))
    swap_np[np.arange(QK_ROPE), np.arange(QK_ROPE)] = 1.0
    swap_np[partner, QK_ROPE + np.arange(QK_ROPE)] = sign
    swap_mat = jnp.asarray(swap_np, BF16)
    wq =w_uq[li].reshape(Q_LORA, MLA_HEADS, QK_NOPE + QK_ROPE)
    wq_r = wq[:, :, QK_NOPE:]
    wq_p = jnp.concatenate([wq[:, :, :QK_NOPE], wq_r, wq_r[:, :, partner] * sign], axis=2)
    wq_p = wq_p.reshape(Q_LORA, MLA_HEADS * HEAD_PAD).astype(BF16)
    wkv = w_ukv[li].reshape(KV_LORA, MLA_HEADS, QK_NOPE + V_HEAD)
    wkv_p = jnp.concatenate([wkv[:, :, :QK_NOPE].reshape(KV_LORA, -1),
                             wkv[:, :, QK_NOPE:].reshape(KV_LORA, -1)], axis=1).astype(BF16)

    tab_q, tab_k, tab_k_ctx = _rope_tables()
    q, k_lat, v_lat = _proj_attn(hx, win_t, swap_mat, g_q[li], g_kv[li], wq_p, wkv_p, tab_q, tab_k,
                                 tm=512, with_q=True)
    k_ctx, v_ctx = _proj_attn(hc, win_t, swap_mat, g_q[li], g_kv[li], wq_p, wkv_p,
                              tab_q[:CTX_LEN], tab_k_ctx, tm=CTX_LEN, with_q=False)
    attn = _attention(q, k_lat, k_ctx, v_lat, v_ctx)

    hg = _proj_hg(hx, win_t, KR_END)
    u3 = _short_conv(hg, hy_conv_w[li], hy_conv_b[li], col0=0)
    w1p = jnp.zeros((HY_ZW, HY_FILTER_W), F32).at[:HY_EMB].set(hy_w1[li])
    decay = np.abs(np.linspace(HY_MIN_DECAY, HY_MAX_DECAY, HY_WIDTH))
    decay2 = jnp.asarray(np.tile(decay, HY_ORDER)[None, :], F32)
    filt, sumsq = _hyena_filters(_filter_features(), w1p, hy_b1[li], hy_w2[li], hy_b2[li],
                                 hy_w3[li], hy_freq[li], decay2)
    m_a, g2, g2i, m_c = _dft_tables()
    hf = _hyena_spectrum(filt, sumsq, m_a, g2)
    nb = HY_WIDTH // FFT_LANES
    z = _long_conv_gate(u3, 0, u3, nb, hy_skip[li][0], hf, 0, m_a, g2, g2i, m_c)
    hy = _long_conv_gate(z, 0, u3, 2 * nb, hy_skip[li][1], hf, nb, m_a, g2, g2i, m_c)

    y = _merge(attn, hy, hg, 3 * HY_WIDTH, w_attn_o[li].astype(BF16), w_hy_o[li].astype(BF16))
    x2 = _out_proj(y, w_out[li].astype(BF16), x1, _rows8(mx, 5, 6))

    out = _half_ffn(x2, _rows8(mx, 6, 9), g_ffn2[li], w13_ffn2[li], w2_ffn2[li], g_final,
                    tm=FFN_TM, final_norm=True)
    return out[None]
```

```python
import functools
import math

import numpy as np
import jax
import jax.numpy as jnp
from jax import lax
from jax.experimental import pallas as pl
from jax.experimental.pallas import tpu as pltpu

F32 = jnp.float32
BF16 = jnp.bfloat16

D_MODEL = 2048
SEQ = 8192
GRID_W = 64
CTX_LEN = 256
N_MOD = 9
D_FF = 5632
NORM_EPS = 1e-6

MLA_HEADS = 16
Q_LORA = 512
KV_LORA = 512
QK_NOPE = 128
QK_ROPE = 64
V_HEAD = 128
ROPE_AXIS = QK_ROPE // 2
ROPE_THETA = 10000.0
ATTN_SCALE = (QK_NOPE + QK_ROPE) ** -0.5
HEAD_PAD = 256

HY_WIDTH = 1024
HY_ORDER = 2
HY_EMB = 33
HY_BANDS = (HY_EMB - 1) // 2
HY_FILTER_W = 64
HY_TARGET = 1e-2
HY_FAST_DECAY = 0.3
HY_SLOW_DECAY = 1.5
HY_MAX_DECAY = math.log(HY_TARGET) / HY_FAST_DECAY
HY_MIN_DECAY = math.log(HY_TARGET) / HY_SLOW_DECAY
HY_MOD_SHIFT = 0.05

Q_END = Q_LORA
KV_END = Q_END + KV_LORA
KR_END = KV_END + QK_ROPE
HY_END = KR_END + 3 * HY_WIDTH

FFT_N = 2 * SEQ
FFT_N1 = 128
FFT_N2 = 128
FFT_HALF = SEQ // FFT_N2

V7X_VMEM_LIMIT = 60 * 1024 * 1024
LANES = 128


def _cparams(sem, vmem=V7X_VMEM_LIMIT):
    return pltpu.CompilerParams(dimension_semantics=sem, vmem_limit_bytes=vmem)


def _sigmoid(x):
    return 1.0 / (1.0 + jnp.exp(-x))


def _rms(x, g):
    var = jnp.mean(x * x, axis=-1, keepdims=True)
    return x * lax.rsqrt(var + NORM_EPS) * g


def _mod_kernel(c_ref, w_ref, b_ref, o_ref):
    c = c_ref[...]
    a = c * _sigmoid(c)
    o_ref[...] = jnp.dot(a, w_ref[...], preferred_element_type=F32) + b_ref[...]


def _modulation(c8, w_mod, b_mod):
    n = w_mod.shape[1]
    tn = 1024
    return pl.pallas_call(
        _mod_kernel,
        grid=(n // tn,),
        in_specs=[
            pl.BlockSpec((8, D_MODEL), lambda j: (0, 0)),
            pl.BlockSpec((D_MODEL, tn), lambda j: (0, j)),
            pl.BlockSpec((1, tn), lambda j: (0, j)),
        ],
        out_specs=pl.BlockSpec((8, tn), lambda j: (0, j)),
        out_shape=jax.ShapeDtypeStruct((8, n), F32),
        compiler_params=_cparams(("arbitrary",)),
        name="modulation",
    )(c8, w_mod, b_mod.reshape(1, n))


def _ffn_kernel(s_ref, mod_ref, g_ref, w1_ref, w3_ref, w2_ref, gf_ref, o_ref, h_scr,
                *, final_norm):
    f = pl.program_id(1)

    @pl.when(f == 0)
    def _():
        h = _rms(s_ref[...], g_ref[...])
        h = h * (1.0 + mod_ref[1:2, :]) + mod_ref[0:1, :]
        h_scr[...] = h.astype(BF16)
        o_ref[...] = jnp.zeros_like(o_ref)

    h = h_scr[...]
    a = jnp.dot(h, w1_ref[...].astype(BF16), preferred_element_type=F32)
    b = jnp.dot(h, w3_ref[...].astype(BF16), preferred_element_type=F32)
    act = (a * _sigmoid(a) * b).astype(BF16)
    o_ref[...] += jnp.dot(act, w2_ref[...].astype(BF16), preferred_element_type=F32)

    @pl.when(f == pl.num_programs(1) - 1)
    def _():
        out = s_ref[...] + 0.5 * mod_ref[2:3, :] * o_ref[...]
        if final_norm:
            out = _rms(out, gf_ref[...])
        o_ref[...] = out


FFN_TM = 1024


FFN_TF = 256


def _half_ffn(s, mod3, g, w13, w2, g_final, *, tm, final_norm):
    rows = s.shape[0]
    tf = FFN_TF
    nf = D_FF // tf
    return pl.pallas_call(
        functools.partial(_ffn_kernel, final_norm=final_norm),
        grid=(rows // tm, nf),
        in_specs=[
            pl.BlockSpec((tm, D_MODEL), lambda i, f: (i, 0)),
            pl.BlockSpec((8, D_MODEL), lambda i, f: (0, 0)),
            pl.BlockSpec((1, D_MODEL), lambda i, f: (0, 0)),
            pl.BlockSpec((D_MODEL, tf), lambda i, f: (0, f)),
            pl.BlockSpec((D_MODEL, tf), lambda i, f: (0, f + nf)),
            pl.BlockSpec((tf, D_MODEL), lambda i, f: (f, 0)),
            pl.BlockSpec((1, D_MODEL), lambda i, f: (0, 0)),
        ],
        out_specs=pl.BlockSpec((tm, D_MODEL), lambda i, f: (i, 0)),
        out_shape=jax.ShapeDtypeStruct((rows, D_MODEL), F32),
        scratch_shapes=[pltpu.VMEM((tm, D_MODEL), BF16)],
        compiler_params=_cparams(("arbitrary", "arbitrary")),
        name="half_ffn",
    )(s, mod3, g.reshape(1, D_MODEL), w13, w13, w2, g_final.reshape(1, D_MODEL))


def _prenorm_kernel(s_ref, mod_ref, g_ref, o_ref):
    h = _rms(s_ref[...], g_ref[...])
    o_ref[...] = (h * (1.0 + mod_ref[1:2, :]) + mod_ref[0:1, :]).astype(BF16)


def _prenorm(s, mod3, g, *, tm):
    rows = s.shape[0]
    return pl.pallas_call(
        _prenorm_kernel,
        grid=(rows // tm,),
        in_specs=[
            pl.BlockSpec((tm, D_MODEL), lambda i: (i, 0)),
            pl.BlockSpec((8, D_MODEL), lambda i: (0, 0)),
            pl.BlockSpec((1, D_MODEL), lambda i: (0, 0)),
        ],
        out_specs=pl.BlockSpec((tm, D_MODEL), lambda i: (i, 0)),
        out_shape=jax.ShapeDtypeStruct((rows, D_MODEL), BF16),
        compiler_params=_cparams(("arbitrary",)),
        name="prenorm",
    )(s, mod3, g.reshape(1, D_MODEL))


_NT = (((1,), (1,)), ((), ()))


def _proj_hg_kernel(h_ref, wa_ref, wb_ref, o_ref, *, n_plain_tiles, row_off):
    j = pl.program_id(1)
    w = jnp.concatenate([wa_ref[row_off:, :], wb_ref[:row_off, :]], axis=0).astype(BF16)
    acc = lax.dot_general(h_ref[...], w, _NT, preferred_element_type=F32)
    o_ref[...] = jnp.where(j >= n_plain_tiles, _sigmoid(acc), acc).astype(BF16)


def _proj_hg(h, w_in_t, row0):
    rows = h.shape[0]
    n = 3 * HY_WIDTH + 2 * D_MODEL
    tm, tn = 1024, 1024
    blk0, row_off = divmod(row0, tn)
    assert 0 < row_off < LANES and row_off % 8 == 0 and row0 + n <= w_in_t.shape[0]
    return pl.pallas_call(
        functools.partial(_proj_hg_kernel, n_plain_tiles=3 * HY_WIDTH // tn, row_off=row_off),
        grid=(rows // tm, n // tn),
        in_specs=[
            pl.BlockSpec((tm, D_MODEL), lambda i, j: (i, 0)),
            pl.BlockSpec((tn, D_MODEL), lambda i, j: (j + blk0, 0)),
            pl.BlockSpec((LANES, D_MODEL), lambda i, j: ((j + blk0 + 1) * (tn // LANES), 0)),
        ],
        out_specs=pl.BlockSpec((tm, tn), lambda i, j: (i, j)),
        out_shape=jax.ShapeDtypeStruct((rows, n), BF16),
        compiler_params=_cparams(("arbitrary", "arbitrary")),
        name="proj_gate_hyena",
    )(h, w_in_t, w_in_t)


def _proj_attn_kernel(h_ref, win_ref, swap_ref, gq_ref, gkv_ref, wuq_ref, wukv_ref, tq_ref,
                      tk_ref, *refs, with_q):
    if with_q:
        q_ref, k_ref, v_ref, wa_scr = refs
    else:
        k_ref, v_ref, wa_scr = refs

    @pl.when(pl.program_id(0) == 0)
    def _():
        wa_scr[...] = win_ref[...].astype(BF16)

    p = lax.dot_general(h_ref[...], wa_scr[...], _NT, preferred_element_type=F32)
    kvn = _rms(p[:, Q_LORA:Q_LORA + KV_LORA], gkv_ref[...]).astype(BF16)
    kr = p[:, KV_END:KV_END + 2 * QK_ROPE]
    kr_hi = kr.astype(BF16)
    kr_lo = (kr - kr_hi.astype(F32)).astype(BF16)
    kr2 = (jnp.dot(kr_hi, swap_ref[...], preferred_element_type=F32)
           + jnp.dot(kr_lo, swap_ref[...], preferred_element_type=F32))
    t = kr2 * tk_ref[...]
    krot = (t + pltpu.roll(t, QK_ROPE, 1)).astype(BF16)
    kn = jnp.dot(kvn, wukv_ref[:, :MLA_HEADS * QK_NOPE], preferred_element_type=F32)
    for hd in range(MLA_HEADS):
        k_ref[:, hd * HEAD_PAD:hd * HEAD_PAD + QK_NOPE] = (
            kn[:, hd * QK_NOPE:(hd + 1) * QK_NOPE].astype(BF16))
        k_ref[:, hd * HEAD_PAD + QK_NOPE:(hd + 1) * HEAD_PAD] = krot
    vv = jnp.dot(kvn, wukv_ref[:, MLA_HEADS * QK_NOPE:], preferred_element_type=F32)
    lane = lax.broadcasted_iota(jnp.int32, (h_ref.shape[0], HEAD_PAD - V_HEAD), 1)
    unit = jnp.where(lane == 0, 1.0, 0.0).astype(BF16)
    for hd in range(MLA_HEADS):
        v_ref[:, hd * HEAD_PAD:hd * HEAD_PAD + V_HEAD] = (
            vv[:, hd * V_HEAD:(hd + 1) * V_HEAD].astype(BF16))
        v_ref[:, hd * HEAD_PAD + V_HEAD:(hd + 1) * HEAD_PAD] = unit
    if with_q:
        qn = _rms(p[:, :Q_LORA], gq_ref[...]).astype(BF16)
        tq = tq_ref[...]
        for hd in range(MLA_HEADS):
            qh = jnp.dot(qn, wuq_ref[:, hd * HEAD_PAD:(hd + 1) * HEAD_PAD],
                         preferred_element_type=F32)
            q_ref[:, hd * HEAD_PAD:(hd + 1) * HEAD_PAD] = (qh * tq).astype(BF16)


def _proj_attn(h, w_in_t, swap_mat, g_q, g_kv, w_uq, w_ukv, tab_q, tab_k, *, tm, with_q):
    rows = h.shape[0]
    na = KV_END + 2 * QK_ROPE
    once = dict(pipeline_mode=pl.Buffered(1))
    hp = MLA_HEADS * HEAD_PAD
    hv = MLA_HEADS * HEAD_PAD
    const = lambda i: (0, 0)
    row = lambda i: (i, 0)
    out_shape = [jax.ShapeDtypeStruct((rows, hp), BF16), jax.ShapeDtypeStruct((rows, hv), BF16)]
    out_specs = [pl.BlockSpec((tm, hp), row), pl.BlockSpec((tm, hv), row)]
    if with_q:
        out_shape = [jax.ShapeDtypeStruct((rows, hp), BF16)] + out_shape
        out_specs = [pl.BlockSpec((tm, hp), row)] + out_specs
    return pl.pallas_call(
        functools.partial(_proj_attn_kernel, with_q=with_q),
        grid=(rows // tm,),
        in_specs=[
            pl.BlockSpec((tm, D_MODEL), row),
            pl.BlockSpec((na, D_MODEL), const, **once),
            pl.BlockSpec((2 * QK_ROPE, 2 * QK_ROPE), const),
            pl.BlockSpec((1, Q_LORA), const),
            pl.BlockSpec((1, KV_LORA), const),
            pl.BlockSpec((Q_LORA, hp), const, **once),
            pl.BlockSpec((KV_LORA, MLA_HEADS * (QK_NOPE + V_HEAD)), const, **once),
            pl.BlockSpec((tm, HEAD_PAD), row),
            pl.BlockSpec((tm, 2 * QK_ROPE), row),
        ],
        out_specs=out_specs,
        out_shape=out_shape,
        scratch_shapes=[pltpu.VMEM((na, D_MODEL), BF16)],
        compiler_params=_cparams(("arbitrary",)),
        name="proj_attn",
    )(h, w_in_t, swap_mat, g_q.reshape(1, Q_LORA), g_kv.reshape(1, KV_LORA), w_uq, w_ukv,
      tab_q, tab_k)


ATT_TQ = 2048
ATT_SUB = 1024
ATT_KC = 1024
def _attn_kernel(q_ref, kl_ref, kc_ref, vl_ref, vc_ref, o_ref):
    n_sub = ATT_TQ // ATT_SUB
    qs, ms, accs = [], [], []
    for r in range(n_sub):
        q = q_ref[r * ATT_SUB:(r + 1) * ATT_SUB, :]
        s = lax.dot_general(q, kc_ref[...], _NT, preferred_element_type=F32)
        m = jnp.max(s, axis=-1, keepdims=True)
        p = jnp.exp2(s - m)
        qs.append(q)
        ms.append(m)
        accs.append(jnp.dot(p.astype(BF16), vc_ref[...], preferred_element_type=F32))
    for c in range(SEQ // ATT_KC):
        for r in range(n_sub):
            s = lax.dot_general(qs[r], kl_ref[c * ATT_KC:(c + 1) * ATT_KC, :], _NT,
                                preferred_element_type=F32)
            m_new = jnp.maximum(ms[r], jnp.max(s, axis=-1, keepdims=True))
            alpha = jnp.exp2(ms[r] - m_new)
            p = jnp.exp2(s - m_new)
            accs[r] = accs[r] * alpha + jnp.dot(
                p.astype(BF16), vl_ref[c * ATT_KC:(c + 1) * ATT_KC, :],
                preferred_element_type=F32)
            ms[r] = m_new
    for r in range(n_sub):
        acc = accs[r]
        o_ref[r * ATT_SUB:(r + 1) * ATT_SUB, :] = (
            acc[:, :V_HEAD] / acc[:, V_HEAD:V_HEAD + 1]).astype(BF16)


def _attention(q, k_lat, k_ctx, v_lat, v_ctx):
    return pl.pallas_call(
        _attn_kernel,
        grid=(MLA_HEADS, SEQ // ATT_TQ),
        in_specs=[
            pl.BlockSpec((ATT_TQ, HEAD_PAD), lambda h, i: (i, h)),
            pl.BlockSpec((SEQ, HEAD_PAD), lambda h, i: (0, h)),
            pl.BlockSpec((CTX_LEN, HEAD_PAD), lambda h, i: (0, h)),
            pl.BlockSpec((SEQ, HEAD_PAD), lambda h, i: (0, h)),
            pl.BlockSpec((CTX_LEN, HEAD_PAD), lambda h, i: (0, h)),
        ],
        out_specs=pl.BlockSpec((ATT_TQ, V_HEAD), lambda h, i: (i, h)),
        out_shape=jax.ShapeDtypeStruct((SEQ, MLA_HEADS * V_HEAD), BF16),
        compiler_params=_cparams(("arbitrary", "arbitrary")),
        name="attention",
    )(q, k_lat, k_ctx, v_lat, v_ctx)


SC_HALO = 16


def _short_conv_kernel(x_ref, prev_ref, next_ref, w_ref, b_ref, o_ref):
    i = pl.program_id(0)
    tm = x_ref.shape[0]
    x = x_ref[...].astype(F32)
    prev_row = prev_ref[SC_HALO - 1:SC_HALO, :].astype(F32) * (i > 0).astype(F32)
    next_row = next_ref[0:1, :].astype(F32) * (i < pl.num_programs(0) - 1).astype(F32)
    rows = lax.broadcasted_iota(jnp.int32, x.shape, 0)
    up = jnp.where(rows == 0, prev_row, pltpu.roll(x, 1, 0))
    dn = jnp.where(rows == tm - 1, next_row, pltpu.roll(x, tm - 1, 0))
    y = up * w_ref[0:1, :] + x * w_ref[1:2, :] + dn * w_ref[2:3, :] + b_ref[...]
    for blk in range(o_ref.shape[0]):
        o_ref[blk] = y[:, blk * LANES:(blk + 1) * LANES].astype(o_ref.dtype)


def _short_conv(hg, conv_w, conv_b, *, col0):
    tm, tc = 512, 1024
    nb = 3 * HY_WIDTH // tc
    cb0 = col0 // tc
    hb = tm // SC_HALO
    last = SEQ // SC_HALO - 1
    w8 = jnp.zeros((8, 3 * HY_WIDTH), F32).at[:3].set(conv_w)
    return pl.pallas_call(
        _short_conv_kernel,
        grid=(SEQ // tm, nb),
        in_specs=[
            pl.BlockSpec((tm, tc), lambda i, j: (i, j + cb0)),
            pl.BlockSpec((SC_HALO, tc), lambda i, j: (jnp.maximum(i * hb - 1, 0), j + cb0)),
            pl.BlockSpec((SC_HALO, tc), lambda i, j: (jnp.minimum((i + 1) * hb, last), j + cb0)),
            pl.BlockSpec((8, tc), lambda i, j: (0, j)),
            pl.BlockSpec((1, tc), lambda i, j: (0, j)),
        ],
        out_specs=pl.BlockSpec((tc // LANES, tm, LANES), lambda i, j: (j, i, 0)),
        out_shape=jax.ShapeDtypeStruct((3 * HY_WIDTH // LANES, SEQ, LANES), BF16),
        compiler_params=_cparams(("arbitrary", "arbitrary")),
        name="short_conv",
    )(hg, hg, hg, w8, conv_b.reshape(1, -1))


HY_ZW = 128


def _dot_split3(a, b):
    a_hi = a.astype(BF16)
    b_hi = b.astype(BF16)
    a_lo = (a - a_hi.astype(F32)).astype(BF16)
    b_lo = (b - b_hi.astype(F32)).astype(BF16)
    dot = functools.partial(jnp.dot, preferred_element_type=F32)
    return dot(a_hi, b_hi) + (dot(a_hi, b_lo) + dot(a_lo, b_hi))


def _filter_kernel(z_ref, w1_ref, b1_ref, w2_ref, b2_ref, w3_ref, fr_ref, dec_ref,
                   h_ref, ss_ref):
    hi = lax.Precision.HIGHEST
    z = z_ref[...]
    fr = fr_ref[...]
    h = jnp.sin(fr * (jnp.dot(z, w1_ref[...], precision=hi, preferred_element_type=F32)
                      + b1_ref[...]))
    h = jnp.sin(fr * (jnp.dot(h, w2_ref[...], precision=hi, preferred_element_type=F32)
                      + b2_ref[...]))
    h = _dot_split3(h, w3_ref[...])
    dist = z[:, HY_EMB:HY_EMB + 1]
    h = h * (jnp.exp(-dist * dec_ref[...]) + HY_MOD_SHIFT)
    for blk in range(h_ref.shape[0]):
        h_ref[blk] = h[:, blk * LANES:(blk + 1) * LANES]

    @pl.when(pl.program_id(0) == 0)
    def _():
        ss_ref[...] = jnp.zeros_like(ss_ref)

    ss_ref[...] += jnp.sum(h * h, axis=0, keepdims=True)


def _hyena_filters(z_p, w1p, b1, w2, b2, w3, freq, decay2):
    tl = 512
    nw = HY_ORDER * HY_WIDTH
    const = lambda i: (0, 0)
    return pl.pallas_call(
        _filter_kernel,
        grid=(SEQ // tl,),
        in_specs=[
            pl.BlockSpec((tl, HY_ZW), lambda i: (i, 0)),
            pl.BlockSpec((HY_ZW, HY_FILTER_W), const),
            pl.BlockSpec((1, HY_FILTER_W), const),
            pl.BlockSpec((HY_FILTER_W, HY_FILTER_W), const),
            pl.BlockSpec((1, HY_FILTER_W), const),
            pl.BlockSpec((HY_FILTER_W, nw), const),
            pl.BlockSpec((1, HY_FILTER_W), const),
            pl.BlockSpec((1, nw), const),
        ],
        out_specs=[pl.BlockSpec((nw // LANES, tl, LANES), lambda i: (0, i, 0)),
                   pl.BlockSpec((1, nw), const)],
        out_shape=[jax.ShapeDtypeStruct((nw // LANES, SEQ, LANES), F32),
                   jax.ShapeDtypeStruct((1, nw), F32)],
        compiler_params=_cparams(("arbitrary",)),
        name="hyena_filters",
    )(z_p, w1p, b1.reshape(1, -1), w2, b2.reshape(1, -1), w3, freq.reshape(1, -1), decay2)


FFT_LANES = 128
FFT_G = 32
FFT_STEPS = FFT_N2 // FFT_G
FFT_ROWS = 2 * FFT_N1
Y_PITCH = FFT_ROWS + 8
T_PITCH = FFT_N2 + 8


def _pad_rows_in(src_ref, dst_scr, scale=None):
    for n1 in range(FFT_HALF):
        v = src_ref[n1 * FFT_N2:(n1 + 1) * FFT_N2, :].astype(F32)
        dst_scr[n1 * T_PITCH:n1 * T_PITCH + FFT_N2, :] = v if scale is None else v * scale


def _fft_stage_a(t, src_scr, ma_ref, y_scr):
    for g in range(FFT_G):
        j = t * FFT_G + g
        xj = src_scr[pl.ds(j, FFT_HALF, stride=T_PITCH), :].astype(BF16)
        xj = jnp.concatenate([xj, jnp.zeros_like(xj)], axis=0)
        a = jnp.dot(ma_ref[g], xj, preferred_element_type=F32)
        y_scr[pl.ds(pl.multiple_of(j * Y_PITCH, 8), FFT_ROWS), :] = a


def _fft_load_pair(tb, g, y_scr):
    cols = []
    for k1 in (tb * FFT_G + g, tb * FFT_G + g + 1):
        re = y_scr[pl.ds(k1, FFT_N2, stride=Y_PITCH), :]
        im = y_scr[pl.ds(FFT_N1 + k1, FFT_N2, stride=Y_PITCH), :]
        cols.append(jnp.concatenate([re, im], axis=0))
    return jnp.concatenate(cols, axis=1).astype(BF16)


def _hyena_spectrum_kernel(h_ref, ss_ref, ma_ref, g2_ref, o_ref, y_scr, h_scr):
    t = pl.program_id(1)

    @pl.when(t == 0)
    def _():
        _pad_rows_in(h_ref, h_scr, lax.rsqrt(ss_ref[...] + NORM_EPS))

    @pl.when(t < FFT_STEPS)
    def _():
        _fft_stage_a(t, h_scr, ma_ref, y_scr)

    @pl.when(t >= FFT_STEPS)
    def _():
        tb = t - FFT_STEPS
        for g in range(0, FFT_G, 2):
            z = jnp.dot(g2_ref[...], _fft_load_pair(tb, g, y_scr), preferred_element_type=F32)
            o_ref[g] = z[:, :FFT_LANES].astype(BF16)
            o_ref[g + 1] = z[:, FFT_LANES:].astype(BF16)


def _hyena_spectrum(filt, sumsq, m_a, g2):
    nb = filt.shape[0]
    last = FFT_STEPS - 1
    return pl.pallas_call(
        _hyena_spectrum_kernel,
        grid=(nb, 2 * FFT_STEPS),
        in_specs=[
            pl.BlockSpec((None, SEQ, FFT_LANES), lambda c, t: (c, 0, 0)),
            pl.BlockSpec((1, FFT_LANES), lambda c, t: (0, c)),
            pl.BlockSpec((FFT_G, FFT_ROWS, LANES), lambda c, t: (jnp.minimum(t, last), 0, 0)),
            pl.BlockSpec((FFT_ROWS, FFT_ROWS), lambda c, t: (0, 0)),
        ],
        out_specs=pl.BlockSpec((None, FFT_G, FFT_ROWS, FFT_LANES),
                               lambda c, t: (c, jnp.maximum(t - FFT_STEPS, 0), 0, 0)),
        out_shape=jax.ShapeDtypeStruct((nb, FFT_N1, FFT_ROWS, FFT_LANES), BF16),
        scratch_shapes=[pltpu.VMEM((FFT_N2 * Y_PITCH, FFT_LANES), F32),
                        pltpu.VMEM((FFT_HALF * T_PITCH, FFT_LANES), F32)],
        compiler_params=_cparams(("arbitrary", "arbitrary")),
        name="hyena_spectrum",
    )(filt, sumsq, m_a, g2)


def _long_conv_kernel(u_ref, x_ref, skip_ref, ma_ref, hf_ref, g2_ref, g2i_ref, mc_ref,
                      o_ref, y_scr, stage_scr, u_scr, x_scr):
    t = pl.program_id(1)

    @pl.when(t == 0)
    def _():
        _pad_rows_in(u_ref, u_scr)
        _pad_rows_in(x_ref, x_scr)

    @pl.when(t < FFT_STEPS)
    def _():
        _fft_stage_a(t, u_scr, ma_ref, y_scr)

    @pl.when((t >= FFT_STEPS) & (t < 2 * FFT_STEPS))
    def _():
        tb = t - FFT_STEPS
        for g in range(0, FFT_G, 2):
            stage_scr[g // 2] = _fft_load_pair(tb, g, y_scr)
        for g in range(0, FFT_G, 2):
            z = jnp.dot(g2_ref[...], stage_scr[g // 2], preferred_element_type=F32)
            hf = jnp.concatenate([hf_ref[g], hf_ref[g + 1]], axis=1).astype(F32)
            zr, zi = z[:FFT_N2], z[FFT_N2:]
            hr, hi = hf[:FFT_N2], hf[FFT_N2:]
            p = jnp.concatenate([zr * hr - zi * hi, zr * hi + zi * hr], axis=0).astype(BF16)
            b = jnp.dot(g2i_ref[...], p, preferred_element_type=F32)
            for half, k1 in enumerate((tb * FFT_G + g, tb * FFT_G + g + 1)):
                lanes = slice(half * FFT_LANES, (half + 1) * FFT_LANES)
                y_scr[pl.ds(k1, FFT_N2, stride=Y_PITCH), :] = b[:FFT_N2, lanes]
                y_scr[pl.ds(FFT_N1 + k1, FFT_N2, stride=Y_PITCH), :] = b[FFT_N2:, lanes]

    @pl.when(t >= 2 * FFT_STEPS)
    def _():
        tc = t - 2 * FFT_STEPS
        for g in range(FFT_G):
            m2 = tc * FFT_G + g
            bm = y_scr[pl.ds(pl.multiple_of(m2 * Y_PITCH, 8), FFT_ROWS), :]
            y = jnp.dot(mc_ref[g], bm.astype(BF16), preferred_element_type=F32)
            rows = pl.ds(m2, FFT_HALF, stride=T_PITCH)
            u_scr[rows, :] = x_scr[rows, :] * (y + skip_ref[...] * u_scr[rows, :])

    @pl.when(t == 3 * FFT_STEPS - 1)
    def _():
        for n1 in range(FFT_HALF):
            o_ref[n1 * FFT_N2:(n1 + 1) * FFT_N2, :] = u_scr[n1 * T_PITCH:n1 * T_PITCH + FFT_N2, :]


def _long_conv_gate(u, ub, x, xb, skip, hf, hf0, m_a, g2, g2i, m_c):
    nb = HY_WIDTH // FFT_LANES
    last = FFT_STEPS - 1
    step = lambda t, phase: jnp.clip(t - phase * FFT_STEPS, 0, last)
    return pl.pallas_call(
        _long_conv_kernel,
        grid=(nb, 3 * FFT_STEPS),
        in_specs=[
            pl.BlockSpec((None, SEQ, FFT_LANES), lambda c, t: (c + ub, 0, 0)),
            pl.BlockSpec((None, SEQ, FFT_LANES), lambda c, t: (c + xb, 0, 0)),
            pl.BlockSpec((1, FFT_LANES), lambda c, t: (0, c)),
            pl.BlockSpec((FFT_G, FFT_ROWS, LANES), lambda c, t: (step(t, 0), 0, 0)),
            pl.BlockSpec((None, FFT_G, FFT_ROWS, FFT_LANES),
                         lambda c, t: (c + hf0, step(t, 1), 0, 0)),
            pl.BlockSpec((FFT_ROWS, FFT_ROWS), lambda c, t: (0, 0)),
            pl.BlockSpec((FFT_ROWS, FFT_ROWS), lambda c, t: (0, 0)),
            pl.BlockSpec((FFT_G, FFT_HALF, FFT_ROWS), lambda c, t: (step(t, 2), 0, 0)),
        ],
        out_specs=pl.BlockSpec((None, SEQ, FFT_LANES), lambda c, t: (c, 0, 0)),
        out_shape=jax.ShapeDtypeStruct((nb, SEQ, FFT_LANES), F32),
        scratch_shapes=[pltpu.VMEM((FFT_N2 * Y_PITCH, FFT_LANES), F32),
                        pltpu.VMEM((FFT_G // 2, FFT_ROWS, 2 * FFT_LANES), BF16),
                        pltpu.VMEM((FFT_HALF * T_PITCH, FFT_LANES), F32),
                        pltpu.VMEM((FFT_HALF * T_PITCH, FFT_LANES), F32)],
        compiler_params=_cparams(("arbitrary", "arbitrary")),
        name="long_conv_gate",
    )(u, x, skip.reshape(1, HY_WIDTH), m_a, hf, g2, g2i, m_c)


def _merge_kernel(a_ref, hy_ref, ga_ref, gh_ref, wa_ref, wh_ref, o_ref):
    ya = jnp.dot(a_ref[...], wa_ref[...], preferred_element_type=F32)
    hy = jnp.concatenate([hy_ref[blk].astype(BF16) for blk in range(hy_ref.shape[0])], axis=1)
    yh = jnp.dot(hy, wh_ref[...], preferred_element_type=F32)
    o_ref[...] = (ga_ref[...].astype(F32) * ya + gh_ref[...].astype(F32) * yh).astype(BF16)


def _merge(attn, hy, hg, gate_col0, w_attn_o, w_hy_o):
    tm, tn = 1024, 1024
    ga0 = gate_col0 // tn
    gh0 = (gate_col0 + D_MODEL) // tn
    row = lambda i, j: (i, 0)
    return pl.pallas_call(
        _merge_kernel,
        grid=(SEQ // tm, D_MODEL // tn),
        in_specs=[
            pl.BlockSpec((tm, MLA_HEADS * V_HEAD), row),
            pl.BlockSpec((HY_WIDTH // LANES, tm, LANES), lambda i, j: (0, i, 0)),
            pl.BlockSpec((tm, tn), lambda i, j: (i, j + ga0)),
            pl.BlockSpec((tm, tn), lambda i, j: (i, j + gh0)),
            pl.BlockSpec((MLA_HEADS * V_HEAD, tn), lambda i, j: (0, j)),
            pl.BlockSpec((HY_WIDTH, tn), lambda i, j: (0, j)),
        ],
        out_specs=pl.BlockSpec((tm, tn), lambda i, j: (i, j)),
        out_shape=jax.ShapeDtypeStruct((SEQ, D_MODEL), BF16),
        compiler_params=_cparams(("arbitrary", "arbitrary")),
        name="merge_branches",
    )(attn, hy, hg, hg, w_attn_o, w_hy_o)


def _out_proj_kernel(y_ref, w_ref, s_ref, mod_ref, o_ref):
    xm = jnp.dot(y_ref[...], w_ref[...], preferred_element_type=F32)
    o_ref[...] = s_ref[...] + mod_ref[0:1, :] * xm


def _out_proj(y, w_out, s, mod_gate):
    tm = 512
    row = lambda i: (i, 0)
    const = lambda i: (0, 0)
    return pl.pallas_call(
        _out_proj_kernel,
        grid=(SEQ // tm,),
        in_specs=[
            pl.BlockSpec((tm, D_MODEL), row),
            pl.BlockSpec((D_MODEL, D_MODEL), const),
            pl.BlockSpec((tm, D_MODEL), row),
            pl.BlockSpec((8, D_MODEL), const),
        ],
        out_specs=pl.BlockSpec((tm, D_MODEL), row),
        out_shape=jax.ShapeDtypeStruct((SEQ, D_MODEL), F32),
        compiler_params=_cparams(("arbitrary",)),
        name="out_proj",
    )(y, w_out, s, mod_gate)


def _rope_tables():
    t = np.arange(SEQ)
    pos = np.stack([t // GRID_W, t % GRID_W], axis=1).astype(np.float64)
    inv_freq = ROPE_THETA ** (-np.arange(0, ROPE_AXIS, 2, dtype=np.float64) / ROPE_AXIS)
    i = np.arange(QK_ROPE)
    ang = pos[:, i // ROPE_AXIS] * inv_freq[i % (ROPE_AXIS // 2)][None, :]
    cos, sin = np.cos(ang), np.sin(ang)
    tab_q = np.concatenate([np.ones((SEQ, QK_NOPE)), cos, sin], axis=1) * (ATTN_SCALE * math.log2(math.e))
    tab_k = np.concatenate([cos, sin], axis=1)
    tab_k_ctx = np.concatenate([np.ones((CTX_LEN, QK_ROPE)), np.zeros((CTX_LEN, QK_ROPE))], axis=1)
    return (jnp.asarray(tab_q, F32), jnp.asarray(tab_k, F32), jnp.asarray(tab_k_ctx, F32))


def _rope_swap():
    i = np.arange(QK_ROPE)
    first_half = (i % ROPE_AXIS) < ROPE_AXIS // 2
    partner = np.where(first_half, i + ROPE_AXIS // 2, i - ROPE_AXIS // 2)
    sign = np.where(first_half, -1.0, 1.0)
    return partner, sign


def _filter_features():
    pos = np.arange(SEQ, dtype=np.float64)[:, None]
    t01 = pos / (SEQ - 1)
    bands = np.linspace(1e-4, HY_BANDS - 1, HY_BANDS)[None, :]
    ang = bands * (2.0 * math.pi / SEQ) * pos
    dist = np.abs(pos - (SEQ // 2)) / (SEQ / 2.0)
    z = np.concatenate([t01, np.cos(ang), -np.sin(ang), dist], axis=1)
    z = np.pad(z, ((0, 0), (0, HY_ZW - z.shape[1])))
    return jnp.asarray(z, F32)


def _dft_tables():
    n1 = np.arange(FFT_HALF)
    k1 = np.arange(FFT_N1)
    n2 = np.arange(FFT_N2)
    n = FFT_N2 * n1[None, None, :] + n2[:, None, None]
    ph = (k1[None, :, None] * n) % FFT_N
    th = 2.0 * math.pi * ph / FFT_N
    m_a = np.concatenate([np.cos(th), -np.sin(th)], axis=1)
    m_a = np.pad(m_a, ((0, 0), (0, 0), (0, LANES - FFT_HALF)))
    ph2 = (n2[:, None] * n2[None, :]) % FFT_N2
    th2 = 2.0 * math.pi * ph2 / FFT_N2
    c, s = np.cos(th2), np.sin(th2)
    g2 = np.block([[c, s], [-s, c]])
    g2i = np.block([[c, -s], [s, c]])
    n = FFT_N2 * (n1[None, :, None] + FFT_HALF // 2) + n2[:, None, None]
    ph = (k1[None, None, :] * n) % FFT_N
    th = 2.0 * math.pi * ph / FFT_N
    m_c = np.concatenate([np.cos(th), -np.sin(th)], axis=2) / FFT_N
    return tuple(jnp.asarray(t, F32).astype(BF16) for t in (m_a, g2, g2i, m_c))


def _rows8(m, lo, hi):
    return jnp.zeros((8, D_MODEL), F32).at[:hi - lo].set(m[lo:hi])


def kernel(x, c, ctx, c_ctx, w_mod, b_mod, g_ffn1, w13_ffn1, w2_ffn1, g_mix, w_in, g_q, w_uq,
           g_kv, w_ukv, w_attn_o, hy_conv_w, hy_conv_b, hy_w1, hy_b1, hy_w2, hy_b2, hy_w3,
           hy_freq, hy_skip, w_hy_o, w_out, g_ffn2, w13_ffn2, w2_ffn2, g_final):
    xs = x[0]
    cs = ctx[0]
    li = 0

    c8 = jnp.zeros((8, D_MODEL), F32).at[0].set(c[0]).at[1].set(c_ctx)
    mod = _modulation(c8, w_mod[li], b_mod[li])
    mx = mod[0].reshape(N_MOD, D_MODEL)
    mc = mod[1].reshape(N_MOD, D_MODEL)

    w13 = w13_ffn1[li]
    w2 = w2_ffn1[li]
    x1 = _half_ffn(xs, _rows8(mx, 0, 3), g_ffn1[li], w13, w2, g_final, tm=FFN_TM, final_norm=False)
    c1 = _half_ffn(cs, _rows8(mc, 0, 3), g_ffn1[li], w13, w2, g_final, tm=CTX_LEN,
                   final_norm=False)

    hx = _prenorm(x1, _rows8(mx, 3, 5), g_mix[li], tm=512)
    hc = _prenorm(c1, _rows8(mc, 3, 5), g_mix[li], tm=CTX_LEN)

    win_t = jnp.swapaxes(w_in[li], 0, 1)
    partner, sign = _rope_swap()
    swap_np = np.zeros((2 * QK_ROPE, 2 * QK_ROPE))
    swap_np[np.arange(QK_ROPE), np.arange(QK_ROPE)] = 1.0
    swap_np[partner, QK_ROPE + np.arange(QK_ROPE)] = sign
    swap_mat = jnp.asarray(swap_np, BF16)
    wq =w_uq[li].reshape(Q_LORA, MLA_HEADS, QK_NOPE + QK_ROPE)
    wq_r = wq[:, :, QK_NOPE:]
    wq_p = jnp.concatenate([wq[:, :, :QK_NOPE], wq_r, wq_r[:, :, partner] * sign], axis=2)
    wq_p = wq_p.reshape(Q_LORA, MLA_HEADS * HEAD_PAD).astype(BF16)
    wkv = w_ukv[li].reshape(KV_LORA, MLA_HEADS, QK_NOPE + V_HEAD)
    wkv_p = jnp.concatenate([wkv[:, :, :QK_NOPE].reshape(KV_LORA, -1),
                             wkv[:, :, QK_NOPE:].reshape(KV_LORA, -1)], axis=1).astype(BF16)

    tab_q, tab_k, tab_k_ctx = _rope_tables()
    q, k_lat, v_lat = _proj_attn(hx, win_t, swap_mat, g_q[li], g_kv[li], wq_p, wkv_p, tab_q, tab_k,
                                 tm=512, with_q=True)
    k_ctx, v_ctx = _proj_attn(hc, win_t, swap_mat, g_q[li], g_kv[li], wq_p, wkv_p,
                              tab_q[:CTX_LEN], tab_k_ctx, tm=CTX_LEN, with_q=False)
    attn = _attention(q, k_lat, k_ctx, v_lat, v_ctx)

    hg = _proj_hg(hx, win_t, KR_END)
    u3 = _short_conv(hg, hy_conv_w[li], hy_conv_b[li], col0=0)
    w1p = jnp.zeros((HY_ZW, HY_FILTER_W), F32).at[:HY_EMB].set(hy_w1[li])
    decay = np.abs(np.linspace(HY_MIN_DECAY, HY_MAX_DECAY, HY_WIDTH))
    decay2 = jnp.asarray(np.tile(decay, HY_ORDER)[None, :], F32)
    filt, sumsq = _hyena_filters(_filter_features(), w1p, hy_b1[li], hy_w2[li], hy_b2[li],
                                 hy_w3[li], hy_freq[li], decay2)
    m_a, g2, g2i, m_c = _dft_tables()
    hf = _hyena_spectrum(filt, sumsq, m_a, g2)
    nb = HY_WIDTH // FFT_LANES
    z = _long_conv_gate(u3, 0, u3, nb, hy_skip[li][0], hf, 0, m_a, g2, g2i, m_c)
    hy = _long_conv_gate(z, 0, u3, 2 * nb, hy_skip[li][1], hf, nb, m_a, g2, g2i, m_c)

    y = _merge(attn, hy, hg, 3 * HY_WIDTH, w_attn_o[li].astype(BF16), w_hy_o[li].astype(BF16))
    x2 = _out_proj(y, w_out[li].astype(BF16), x1, _rows8(mx, 5, 6))

    out = _half_ffn(x2, _rows8(mx, 6, 9), g_ffn2[li], w13_ffn2[li], w2_ffn2[li], g_final,
                    tm=FFN_TM, final_norm=True)
    return out[None]
```
